```python
import math
import jax
import jax.numpy as jnp
from jax import lax
import numpy as np

D_MODEL = 1024
BATCH = 2
SEQ = 8192
DEPTH = 2
DEC_BATCH = 128
DEC_SEQ = 8
PAST_LEN = 2048
PAGE_SIZE = 128

N_MEM = 256
RET_HEADS = 4
RET_DK = 64
RET_DV = 64
SB_HEADS = 4
SB_DH = 64
DIFF_HEADS = 4
DIFF_DH = 32
HGRN_HEADS = 4
HGRN_DK = 64
HGRN_DV = 64
CA_HEADS = 4
CA_DH = 64
D_FF = 4 * D_MODEL
N_BRANCH = 4
ROPE_THETA = 10000.0
NORM_EPS = 1e-6
REC_CHUNK = 64
Q_BLOCK = 128
MASK_VALUE = -1e30

RET_W = RET_HEADS * RET_DV
SB_W = SB_HEADS * SB_DH
DIFF_W = DIFF_HEADS * 2 * DIFF_DH
HGRN_W = HGRN_HEADS * HGRN_DV
CA_W = CA_HEADS * CA_DH
IN_SIZES = (RET_HEADS * RET_DK, RET_HEADS * RET_DK, RET_W, RET_W,
            SB_W, SB_W, SB_W,
            DIFF_W, DIFF_W, DIFF_W,
            HGRN_HEADS * HGRN_DK, HGRN_HEADS * HGRN_DK, HGRN_W, HGRN_W)
D_IN = sum(IN_SIZES)
IN_SPLITS = tuple(sum(IN_SIZES[:i + 1]) for i in range(len(IN_SIZES) - 1))

kernel_name = 'hybrid_parallel_gated_decoder_step'


def rmsnorm(x, g):
    xf = x.astype(jnp.float32)
    y = xf * lax.rsqrt(jnp.mean(jnp.square(xf), axis=-1, keepdims=True) + NORM_EPS)
    return (y * g.astype(jnp.float32)).astype(x.dtype)


def rope(x, pos):
    d = x.shape[-1]
    half = d // 2
    inv = ROPE_THETA ** (-jnp.arange(half, dtype=jnp.float32) * 2.0 / d)
    ang = pos.astype(jnp.float32)[:, None] * inv[None, :]
    cos = jnp.cos(ang)[:, None, :]
    sin = jnp.sin(ang)[:, None, :]
    xf = x.astype(jnp.float32)
    x1, x2 = xf[..., :half], xf[..., half:]
    return jnp.concatenate([x1 * cos - x2 * sin, x1 * sin + x2 * cos], axis=-1).astype(x.dtype)


def chunked_gated_recurrence(q, k, v, log_f, s0):
    B, L, H, _ = q.shape
    dv = v.shape[-1]
    c = math.gcd(REC_CHUNK, L)
    n = L // c

    def to_chunks(a):
        return a.reshape(B, n, c, H, a.shape[-1]).transpose(1, 0, 3, 2, 4)

    causal = jnp.tril(jnp.ones((c, c), dtype=bool))[:, :, None]

    def step(S, inp):
        qc, kc, vc, fc = inp
        b = jnp.cumsum(fc.astype(jnp.float32), axis=2)
        rel = b[:, :, :, None, :] - b[:, :, None, :, :]
        decay = jnp.where(causal, jnp.exp(jnp.where(causal, rel, 0.0)), 0.0)
        scores = jnp.sum(qc[:, :, :, None, :] * kc[:, :, None, :, :] * decay, axis=-1)
        o = (jnp.einsum('bhij,bhjv->bhiv', scores, vc)
             + jnp.einsum('bhik,bhkv->bhiv', qc * jnp.exp(b), S))
        b_last = b[:, :, -1:, :]
        k_dec = kc * jnp.exp(b_last - b)
        S_new = jnp.exp(b_last)[:, :, 0, :, None] * S + jnp.einsum('bhjk,bhjv->bhkv', k_dec, vc)
        return S_new, o

    S_fin, oc = lax.scan(step, s0.astype(jnp.float32), tuple(map(to_chunks, (q, k, v, log_f))))
    o = oc.transpose(1, 0, 3, 2, 4).reshape(B, L, H, dv)
    return o.astype(v.dtype), S_fin


def map_query_blocks(block_fn, q, q_pos):
    B, L = q.shape[:2]
    qb = math.gcd(Q_BLOCK, L)
    n = L // qb
    if n == 1:
        return block_fn(q, q_pos)
    qs = q.reshape(B, n, qb, *q.shape[2:]).swapaxes(0, 1)
    ps = q_pos.reshape(n, qb)
    out = lax.map(lambda a: block_fn(a[0], a[1]), (qs, ps))
    return out.swapaxes(0, 1).reshape(B, L, *out.shape[3:])


def stick_breaking_block(q, q_pos, k, v, k_pos):
    z = jnp.einsum('bqhd,bkhd->bhqk', q, k).astype(jnp.float32) * (q.shape[-1] ** -0.5)
    mask = k_pos[None, :] < q_pos[:, None]
    u = jnp.where(mask, jax.nn.log_sigmoid(-z), 0.0)
    between = lax.cumsum(u, axis=3, reverse=True) - u
    w = jnp.where(mask, jnp.exp(jax.nn.log_sigmoid(z) + between), 0.0)
    return jnp.einsum('bhqk,bkhd->bqhd', w.astype(v.dtype), v)


def diff_attn_block(q, q_pos, k, v, k_pos, lam):
    scale = DIFF_DH ** -0.5
    mask = k_pos[None, :] <= q_pos[:, None]

    def attn_map(qh, kh):
        s = jnp.einsum('bqhd,bkhd->bhqk', qh, kh).astype(jnp.float32) * scale
        return jax.nn.softmax(jnp.where(mask, s, MASK_VALUE), axis=-1)

    a = (attn_map(q[..., :DIFF_DH], k[..., :DIFF_DH])
         - lam * attn_map(q[..., DIFF_DH:], k[..., DIFF_DH:]))
    return jnp.einsum('bhqk,bkhd->bqhd', a.astype(v.dtype), v)


def memory_kv(mem, p):
    Bm, M, _ = mem.shape
    m = rmsnorm(mem, p['mem_norm'])
    mk = (m @ p['w_ca_k']).reshape(Bm, M, CA_HEADS, CA_DH)
    mv = (m @ p['w_ca_v']).reshape(Bm, M, CA_HEADS, CA_DH)
    return mk, mv


def trunk_layer(l, x, pos, mem_k, mem_v, past, ret_s0, hgrn_s0, p):
    B, L, _ = x.shape
    f32 = jnp.float32
    h = rmsnorm(x, p['mix_pre_norm'])
    (r_q, r_k, r_v, r_g, s_q, s_k, s_v, d_q, d_k, d_v,
     g_q, g_f, g_i, g_g) = jnp.split(h @ p['w_in'], IN_SPLITS, axis=-1)

    q = rope(r_q.reshape(B, L, RET_HEADS, RET_DK), pos)
    k = rope(r_k.reshape(B, L, RET_HEADS, RET_DK), pos) * (RET_DK ** -0.5)
    v = r_v.reshape(B, L, RET_HEADS, RET_DV)
    log_gamma = jnp.log(1.0 - 2.0 ** (-5.0 - jnp.arange(RET_HEADS, dtype=f32)))
    log_decay = jnp.broadcast_to(log_gamma[None, None, :, None], (B, L, RET_HEADS, 1))
    o, ret_state = chunked_gated_recurrence(q, k, v, log_decay, ret_s0)
    o_ret = rmsnorm(o, p['ret_norm'].reshape(RET_HEADS, RET_DV)).reshape(B, L, RET_W) * jax.nn.silu(r_g)

    sb_q = s_q.reshape(B, L, SB_HEADS, SB_DH)
    sb_k_new = s_k.reshape(B, L, SB_HEADS, SB_DH)
    sb_v_new = s_v.reshape(B, L, SB_HEADS, SB_DH)
    df_q = rope(d_q.reshape(B, L, 2 * DIFF_HEADS, DIFF_DH), pos).reshape(B, L, DIFF_HEADS, 2 * DIFF_DH)
    df_k_new = rope(d_k.reshape(B, L, 2 * DIFF_HEADS, DIFF_DH), pos).reshape(B, L, DIFF_HEADS, 2 * DIFF_DH)
    df_v_new = d_v.reshape(B, L, DIFF_HEADS, 2 * DIFF_DH)
    if past is None:
        sb_k, sb_v, df_k, df_v, k_pos = sb_k_new, sb_v_new, df_k_new, df_v_new, pos
    else:
        p_sb_k, p_sb_v, p_df_k, p_df_v, past_pos = past
        sb_k = jnp.concatenate([p_sb_k, sb_k_new], axis=1)
        sb_v = jnp.concatenate([p_sb_v, sb_v_new], axis=1)
        df_k = jnp.concatenate([p_df_k, df_k_new], axis=1)
        df_v = jnp.concatenate([p_df_v, df_v_new], axis=1)
        k_pos = jnp.concatenate([past_pos, pos])
    o_sb = map_query_blocks(lambda qb, pb: stick_breaking_block(qb, pb, sb_k, sb_v, k_pos),
                            sb_q, pos).reshape(B, L, SB_W)
    lam_init = 0.8 - 0.6 * math.exp(-0.3 * l)
    lam_p = p['diff_lambda'].astype(f32)
    lam = jnp.exp(jnp.sum(lam_p[0] * lam_p[1])) - jnp.exp(jnp.sum(lam_p[2] * lam_p[3])) + lam_init
    o = map_query_blocks(lambda qb, pb: diff_attn_block(qb, pb, df_k, df_v, k_pos, lam), df_q, pos)
    o_diff = (rmsnorm(o, p['diff_norm']) * (1.0 - lam_init)).reshape(B, L, DIFF_W)

    lb = p['hgrn_lb'].reshape(HGRN_HEADS, HGRN_DK)
    z = g_f.reshape(B, L, HGRN_HEADS, HGRN_DK).astype(f32)
    log_f = jnp.log(lb + (1.0 - lb) * jax.nn.sigmoid(z))
    k_in = (1.0 - lb) * jax.nn.sigmoid(-z)
    o, hgrn_state = chunked_gated_recurrence(g_q.reshape(B, L, HGRN_HEADS, HGRN_DK), k_in,
                                             g_i.reshape(B, L, HGRN_HEADS, HGRN_DV), log_f, hgrn_s0)
    o_hgrn = rmsnorm(o, p['hgrn_norm'].reshape(HGRN_HEADS, HGRN_DV)).reshape(B, L, HGRN_W) * jax.nn.silu(g_g)

    gates = jax.nn.sigmoid((h @ p['w_gate']).astype(f32)).reshape(B, L, N_BRANCH, D_MODEL)
    merged = (gates[:, :, 0] * (o_ret @ p['w_br_ret'])
              + gates[:, :, 1] * (o_sb @ p['w_br_sb'])
              + gates[:, :, 2] * (o_diff @ p['w_br_diff'])
              + gates[:, :, 3] * (o_hgrn @ p['w_br_hgrn']))
    x = x + rmsnorm(merged.astype(x.dtype) @ p['w_out'], p['mix_post_norm'])

    h = rmsnorm(x, p['ca_pre_norm'])
    cq = (h @ p['w_ca_q']).reshape(B, L, CA_HEADS, CA_DH)
    s = jnp.einsum('blhd,bmhd->bhlm', cq, mem_k).astype(f32) * (CA_DH ** -0.5)
    a = jax.nn.softmax(s, axis=-1)
    co = jnp.einsum('bhlm,bmhd->blhd', a.astype(mem_v.dtype), mem_v).reshape(B, L, CA_W)
    x = x + rmsnorm(co @ p['w_ca_o'], p['ca_post_norm'])

    h = rmsnorm(x, p['mlp_pre_norm'])
    u = jnp.square(jax.nn.relu(h @ p['w_mlp_up']))
    x = x + rmsnorm(u @ p['w_mlp_down'], p['mlp_post_norm'])
    return x, (sb_k_new, sb_v_new, df_k_new, df_v_new, ret_state, hgrn_state)


def setup_inputs(seed: int = 0) -> dict:
    key = jax.random.key(seed)
    keys = iter(jax.random.split(key, 48))
    f32 = jnp.float32

    def nrm(shape, scale):
        return scale * jax.random.normal(next(keys), shape, f32)

    def gain(width):
        return 1.0 + nrm((DEPTH, width), 0.05)

    n_pages = PAST_LEN // PAGE_SIZE
    n_used = DEC_BATCH * n_pages
    n_pool = n_used + n_used // 4
    page_table = jax.random.permutation(next(keys), n_pool)[:n_used].reshape(DEC_BATCH, n_pages).astype(jnp.int32)
    sd = D_MODEL ** -0.5
    return {
        'x_prompt': nrm((BATCH, SEQ, D_MODEL), 1.0),
        'x_sample': nrm((DEC_BATCH, DEC_SEQ, D_MODEL), 1.0),
        'mem_prompt': nrm((BATCH, N_MEM, D_MODEL), 1.0),
        'cache_sb_k': nrm((DEPTH, n_pool, PAGE_SIZE, SB_HEADS, SB_DH), 1.0),
        'cache_sb_v': nrm((DEPTH, n_pool, PAGE_SIZE, SB_HEADS, SB_DH), 1.0),
        'cache_diff_k': nrm((DEPTH, n_pool, PAGE_SIZE, DIFF_HEADS, 2 * DIFF_DH), 1.0),
        'cache_diff_v': nrm((DEPTH, n_pool, PAGE_SIZE, DIFF_HEADS, 2 * DIFF_DH), 1.0),
        'cache_mem_k': nrm((DEPTH, DEC_BATCH, N_MEM, CA_HEADS, CA_DH), 1.0),
        'cache_mem_v': nrm((DEPTH, DEC_BATCH, N_MEM, CA_HEADS, CA_DH), 1.0),
        'state_ret': nrm((DEPTH, DEC_BATCH, RET_HEADS, RET_DK, RET_DV), 0.3),
        'state_hgrn': nrm((DEPTH, DEC_BATCH, HGRN_HEADS, HGRN_DK, HGRN_DV), 0.3),
        'page_table': page_table,
        'mix_pre_norm': gain(D_MODEL),
        'mix_post_norm': gain(D_MODEL),
        'ca_pre_norm': gain(D_MODEL),
        'ca_post_norm': gain(D_MODEL),
        'mlp_pre_norm': gain(D_MODEL),
        'mlp_post_norm': gain(D_MODEL),
        'mem_norm': gain(D_MODEL),
        'w_in': nrm((DEPTH, D_MODEL, D_IN), sd),
        'w_gate': nrm((DEPTH, D_MODEL, N_BRANCH * D_MODEL), sd),
        'ret_norm': gain(RET_W),
        'diff_lambda': nrm((DEPTH, 4, DIFF_DH), 0.1),
        'diff_norm': gain(2 * DIFF_DH),
        'hgrn_lb_logits': nrm((DEPTH, HGRN_HEADS * HGRN_DK), 0.5),
        'hgrn_norm': gain(HGRN_W),
        'w_br_ret': nrm((DEPTH, RET_W, D_MODEL), RET_W ** -0.5),
        'w_br_sb': nrm((DEPTH, SB_W, D_MODEL), SB_W ** -0.5),
        'w_br_diff': nrm((DEPTH, DIFF_W, D_MODEL), DIFF_W ** -0.5),
        'w_br_hgrn': nrm((DEPTH, HGRN_W, D_MODEL), HGRN_W ** -0.5),
        'w_out': nrm((DEPTH, D_MODEL, D_MODEL), sd),
        'w_ca_q': nrm((DEPTH, D_MODEL, CA_W), sd),
        'w_ca_k': nrm((DEPTH, D_MODEL, CA_W), sd),
        'w_ca_v': nrm((DEPTH, D_MODEL, CA_W), sd),
        'w_ca_o': nrm((DEPTH, CA_W, D_MODEL), CA_W ** -0.5),
        'w_mlp_up': nrm((DEPTH, D_MODEL, D_FF), sd),
        'w_mlp_down': nrm((DEPTH, D_FF, D_MODEL), D_FF ** -0.5),
    }


def reference(x_prompt, x_sample, mem_prompt, cache_sb_k, cache_sb_v, cache_diff_k, cache_diff_v,
              cache_mem_k, cache_mem_v, state_ret, state_hgrn, page_table,
              mix_pre_norm, mix_post_norm, ca_pre_norm, ca_post_norm, mlp_pre_norm, mlp_post_norm,
              mem_norm, w_in, w_gate, ret_norm, diff_lambda, diff_norm, hgrn_lb_logits, hgrn_norm,
              w_br_ret, w_br_sb, w_br_diff, w_br_hgrn, w_out,
              w_ca_q, w_ca_k, w_ca_v, w_ca_o, w_mlp_up, w_mlp_down):
    f32 = jnp.float32
    B, L = x_prompt.shape[:2]
    DB, LS = x_sample.shape[:2]
    n_pages = page_table.shape[1]
    past_len = n_pages * PAGE_SIZE
    pos_prompt = jnp.arange(L, dtype=jnp.int32)
    past_pos = jnp.arange(past_len, dtype=jnp.int32)
    pos_sample = past_len + jnp.arange(LS, dtype=jnp.int32)
    lb_w = jax.nn.softmax(hgrn_lb_logits.astype(f32), axis=0)
    hgrn_lb = jnp.cumsum(lb_w, axis=0) - lb_w[0:1]

    def gather_past(cache_l):
        return cache_l[page_table].reshape(DB, past_len, *cache_l.shape[2:])

    xp, xs = x_prompt, x_sample
    outs_p = []
    outs_s = []
    for l in range(DEPTH):
        p = {
            'mix_pre_norm': mix_pre_norm[l], 'mix_post_norm': mix_post_norm[l],
            'ca_pre_norm': ca_pre_norm[l], 'ca_post_norm': ca_post_norm[l],
            'mlp_pre_norm': mlp_pre_norm[l], 'mlp_post_norm': mlp_post_norm[l],
            'mem_norm': mem_norm[l], 'w_in': w_in[l], 'w_gate': w_gate[l],
            'ret_norm': ret_norm[l], 'diff_lambda': diff_lambda[l], 'diff_norm': diff_norm[l],
            'hgrn_lb': hgrn_lb[l], 'hgrn_norm': hgrn_norm[l],
            'w_br_ret': w_br_ret[l], 'w_br_sb': w_br_sb[l], 'w_br_diff': w_br_diff[l],
            'w_br_hgrn': w_br_hgrn[l], 'w_out': w_out[l],
            'w_ca_q': w_ca_q[l], 'w_ca_k': w_ca_k[l], 'w_ca_v': w_ca_v[l], 'w_ca_o': w_ca_o[l],
            'w_mlp_up': w_mlp_up[l], 'w_mlp_down': w_mlp_down[l],
        }
        mk_p, mv_p = memory_kv(mem_prompt, p)
        ret0 = jnp.zeros((B, RET_HEADS, RET_DK, RET_DV), f32)
        hgrn0 = jnp.zeros((B, HGRN_HEADS, HGRN_DK, HGRN_DV), f32)
        xp, new_p = trunk_layer(l, xp, pos_prompt, mk_p, mv_p, None, ret0, hgrn0, p)
        outs_p.append(new_p + (mk_p, mv_p))
        past = (gather_past(cache_sb_k[l]), gather_past(cache_sb_v[l]),
                gather_past(cache_diff_k[l]), gather_past(cache_diff_v[l]), past_pos)
        xs, new_s = trunk_layer(l, xs, pos_sample, cache_mem_k[l], cache_mem_v[l], past,
                                state_ret[l], state_hgrn[l], p)
        outs_s.append(new_s)

    def stack(outs, i):
        return jnp.stack([o[i] for o in outs], axis=0)

    return (xp, xs,
            stack(outs_p, 0), stack(outs_p, 1), stack(outs_p, 2), stack(outs_p, 3),
            stack(outs_p, 6), stack(outs_p, 7), stack(outs_p, 4), stack(outs_p, 5),
            stack(outs_s, 0), stack(outs_s, 1), stack(outs_s, 2), stack(outs_s, 3),
            stack(outs_s, 4), stack(outs_s, 5))
```

```python
import functools
import math

import jax
import jax.numpy as jnp
import numpy as np
from jax import lax
from jax.experimental import pallas as pl
from jax.experimental.pallas import tpu as pltpu

F32 = jnp.float32
BF16 = jnp.bfloat16

D_MODEL = 1024
N_MEM = 256
HEADS = 4
HEAD_W = 64
MIX_W = HEADS * HEAD_W
DIFF_DH = 32
D_FF = 4 * D_MODEL
N_BRANCH = 4
PAGE_SIZE = 128
ROPE_THETA = 10000.0
NORM_EPS = 1e-6
MASK_VALUE = -1e30
N_SLABS = 14
D_IN = N_SLABS * MIX_W
(S_RQ, S_RK, S_RV, S_RG, S_SQ, S_SK, S_SV, S_DQ, S_DK, S_DV,
 S_GQ, S_GF, S_GI, S_GG) = range(N_SLABS)

SB_DEAD_LOG = -104.0

V7X_VMEM_BYTES = 64 * 1024 * 1024
VMEM_LIMIT = (V7X_VMEM_BYTES * 7) // 8

_NT = (((1,), (1,)), ((), ()))
_TN = (((0,), (0,)), ((), ()))


def _params(*sem):
    return pltpu.CompilerParams(dimension_semantics=sem, vmem_limit_bytes=VMEM_LIMIT)


def _dot(a, b):
    return jnp.dot(a, b, preferred_element_type=F32)


def _dot_nt(a, b):
    return lax.dot_general(a, b, _NT, preferred_element_type=F32)


def _dot_tn(a, b):
    return lax.dot_general(a, b, _TN, preferred_element_type=F32)


def _split_dot(x, m16, terms=2, left=False):
    out = None
    r = x
    for _ in range(terms):
        p = r.astype(BF16)
        d = _dot(m16, p) if left else _dot(p, m16)
        out = d if out is None else out + d
        r = r - p.astype(F32)
    return out


def _rms(x, g):
    ms = jnp.mean(x * x, axis=-1, keepdims=True)
    return x * lax.rsqrt(ms + NORM_EPS) * g


def _lane_head(width=MIX_W):
    return lax.broadcasted_iota(jnp.int32, (1, width), 1) // HEAD_W


def _block_diag16():
    r = lax.broadcasted_iota(jnp.int32, (MIX_W, MIX_W), 0) // HEAD_W
    c = lax.broadcasted_iota(jnp.int32, (MIX_W, MIX_W), 1) // HEAD_W
    return jnp.where(r == c, 1.0, 0.0).astype(BF16)


def _head_rms(o, gain, bd16):
    ms = _split_dot(o * o, bd16) * (1.0 / HEAD_W)
    return o * lax.rsqrt(ms + NORM_EPS) * gain


def _log_sigmoid(z):
    return jnp.minimum(z, 0.0) - jnp.log(1.0 + jnp.exp(-jnp.abs(z)))


def _rope_slab(s, cos, sin, half):
    lane = lax.broadcasted_iota(jnp.int32, (1, MIX_W), 1)
    first = (lane % (2 * half)) < half
    partner = jnp.where(first, pltpu.roll(s, MIX_W - half, 1), pltpu.roll(s, half, 1))
    return s * cos + partner * sin


def _norm_proj_kernel(x_ref, g_ref, w_ref, o32_ref):
    h = _rms(x_ref[...], g_ref[...]).astype(BF16)
    o32_ref[...] = _dot(h, w_ref[...])


def _mixer_proj_kernel(x_ref, g_ref, w_ref, tab_ref, o32_ref, o16_ref):
    h = _rms(x_ref[...], g_ref[...]).astype(BF16)
    y = _dot(h, w_ref[...])
    cos_r = tab_ref[:, 0 * MIX_W:1 * MIX_W]
    sin_r = tab_ref[:, 1 * MIX_W:2 * MIX_W]
    cos_d = tab_ref[:, 2 * MIX_W:3 * MIX_W]
    sin_d = tab_ref[:, 3 * MIX_W:4 * MIX_W]
    for c in range(N_SLABS):
        s = y[:, c * MIX_W:(c + 1) * MIX_W]
        if c == S_RQ:
            s = _rope_slab(s, cos_r, sin_r, HEAD_W // 2)
        elif c == S_RK:
            s = _rope_slab(s, cos_r, sin_r, HEAD_W // 2) * (HEAD_W ** -0.5)
        elif c in (S_DQ, S_DK):
            s = _rope_slab(s, cos_d, sin_d, DIFF_DH // 2)
        o32_ref[:, c * MIX_W:(c + 1) * MIX_W] = s
        o16_ref[:, c * MIX_W:(c + 1) * MIX_W] = s.astype(BF16)


def _row_tile(n, want):
    t = math.gcd(n, want)
    assert t % 8 == 0
    return t


def _norm_proj(x, gain, w16):
    n, d = x.shape
    wout = w16.shape[1]
    tm = _row_tile(n, 256)
    return pl.pallas_call(
        _norm_proj_kernel,
        grid=(n // tm,),
        in_specs=[pl.BlockSpec((tm, d), lambda i: (i, 0)),
                  pl.BlockSpec((1, d), lambda i: (0, 0)),
                  pl.BlockSpec((d, wout), lambda i: (0, 0))],
        out_specs=pl.BlockSpec((tm, wout), lambda i: (i, 0)),
        out_shape=jax.ShapeDtypeStruct((n, wout), F32),
        compiler_params=_params("parallel"),
        name="norm_proj",
    )(x, gain, w16)


def _mixer_proj(x, gain, w16, tab):
    n, d = x.shape
    tm = _row_tile(n, 256)
    tab_blocks = tab.shape[0] // tm
    return pl.pallas_call(
        _mixer_proj_kernel,
        grid=(n // tm,),
        in_specs=[pl.BlockSpec((tm, d), lambda i: (i, 0)),
                  pl.BlockSpec((1, d), lambda i: (0, 0)),
                  pl.BlockSpec((d, D_IN), lambda i: (0, 0)),
                  pl.BlockSpec((tm, 4 * MIX_W), lambda i: (i % tab_blocks, 0))],
        out_specs=[pl.BlockSpec((tm, D_IN), lambda i: (i, 0)),
                   pl.BlockSpec((tm, D_IN), lambda i: (i, 0))],
        out_shape=[jax.ShapeDtypeStruct((n, D_IN), F32),
                   jax.ShapeDtypeStruct((n, D_IN), BF16)],
        compiler_params=_params("parallel"),
        name="mixer_proj",
    )(x, gain, w16, tab)


def _rope_table(pos):
    def one(group):
        half = group // 2
        inv = ROPE_THETA ** (-jnp.arange(half, dtype=F32) * 2.0 / group)
        ang = pos.astype(F32)[:, None] * inv[None, :]
        cos, sin = jnp.cos(ang), jnp.sin(ang)
        reps = MIX_W // group
        return (jnp.tile(jnp.concatenate([cos, cos], -1), (1, reps)),
                jnp.tile(jnp.concatenate([-sin, sin], -1), (1, reps)))
    cr, sr = one(HEAD_W)
    cd, sd = one(DIFF_DH)
    return jnp.concatenate([cr, sr, cd, sd], axis=-1)


_RET_LOG_GAMMA = [float(np.log(1.0 - 2.0 ** (-5.0 - h))) for h in range(HEADS)]


def _state_step(st, q_dec, k_dec, v, decay_row):
    inter = _dot_nt(q_dec.astype(BF16), st.astype(BF16))
    upd = _dot_tn(v.astype(BF16), k_dec.astype(BF16))
    r = lax.broadcasted_iota(jnp.int32, (MIX_W, MIX_W), 0) // HEAD_W
    c = lax.broadcasted_iota(jnp.int32, (MIX_W, MIX_W), 1) // HEAD_W
    return inter, st * decay_row + jnp.where(r == c, upd, 0.0)


def _retention_kernel(q_ref, k_ref, v_ref, g_ref, gain_ref, st0_ref, o_ref, st_ref, *, chunk):
    @pl.when(pl.program_id(1) == 0)
    def _():
        st_ref[...] = st0_ref[...]

    q, k, v = q_ref[...], k_ref[...], v_ref[...]
    head = _lane_head()
    lg = jnp.zeros((1, MIX_W), F32)
    for h in range(HEADS):
        lg = jnp.where(head == h, _RET_LOG_GAMMA[h], lg)
    steps = lax.broadcasted_iota(jnp.int32, (chunk, 1), 0).astype(F32) + 1.0
    b = steps * lg
    b_last = float(chunk) * lg
    o, st_new = _state_step(st_ref[0], q * jnp.exp(b), k * jnp.exp(b_last - b), v,
                            jnp.exp(b_last))
    st_ref[0] = st_new

    ri = lax.broadcasted_iota(jnp.int32, (chunk, chunk), 0)
    ci = lax.broadcasted_iota(jnp.int32, (chunk, chunk), 1)
    causal = ri >= ci
    dist = (ri - ci).astype(F32)
    k16, v16 = k.astype(BF16), v.astype(BF16)
    for h in range(HEADS):
        sc = _dot_nt(jnp.where(head == h, q, 0.0).astype(BF16), k16)
        dec = jnp.where(causal, jnp.exp(dist * _RET_LOG_GAMMA[h]), 0.0)
        o = o + jnp.where(head == h, _dot((sc * dec).astype(BF16), v16), 0.0)

    g = g_ref[...]
    o_ref[...] = _head_rms(o, gain_ref[...], _block_diag16()) * (g * jax.nn.sigmoid(g))


def _hgrn_kernel(q_ref, f_ref, v_ref, g_ref, gain_ref, lbl_ref, st0_ref, o_ref, st_ref,
                 kk_scr, b_scr, *, chunk, layer):
    @pl.when(pl.program_id(1) == 0)
    def _():
        st_ref[...] = st0_ref[...]

    lbl = lbl_ref[...]
    e = jnp.exp(lbl - jnp.max(lbl, axis=0, keepdims=True))
    lb_w = e / jnp.sum(e, axis=0, keepdims=True)
    lb = lb_w[0:1, :]
    for i in range(1, layer + 1):
        lb = lb + lb_w[i:i + 1, :]
    lb = lb - lb_w[0:1, :]

    q, z, v = q_ref[...], f_ref[...], v_ref[...]
    log_f = jnp.log(lb + (1.0 - lb) * jax.nn.sigmoid(z))
    k = (1.0 - lb) * jax.nn.sigmoid(-z)
    ri = lax.broadcasted_iota(jnp.int32, (chunk, chunk), 0)
    ci = lax.broadcasted_iota(jnp.int32, (chunk, chunk), 1)
    tril16 = jnp.where(ri >= ci, 1.0, 0.0).astype(BF16)
    if chunk >= 16:
        b = _split_dot(log_f, tril16, terms=3, left=True)
    else:
        b = _small_cumsum(log_f, chunk)
    b_last = b[chunk - 1:chunk, :]
    inter, st_new = _state_step(st_ref[0], q * jnp.exp(b), k * jnp.exp(b_last - b), v,
                                jnp.exp(b_last))
    st_ref[0] = st_new

    kk_scr[...] = k
    b_scr[...] = b
    bd16 = _block_diag16()
    rows = lax.broadcasted_iota(jnp.int32, (chunk, 1), 0)

    def pair(j, o):
        kj = kk_scr[pl.ds(j, 1), :]
        bj = b_scr[pl.ds(j, 1), :]
        vj = v_ref[pl.ds(j, 1), :]
        t = jnp.where(rows >= j, q * kj * jnp.exp(jnp.minimum(b - bj, 0.0)), 0.0)
        return o + _dot(t.astype(BF16), bd16) * vj

    o = lax.fori_loop(0, chunk, pair, inter)
    g = g_ref[...]
    o_ref[...] = _head_rms(o, gain_ref[...], bd16) * (g * jax.nn.sigmoid(g))


def _small_cumsum(x, rows):
    idx = lax.broadcasted_iota(jnp.int32, (rows, 1), 0)
    out = jnp.zeros_like(x)
    for j in range(rows):
        out = out + jnp.where(idx >= j, x[j:j + 1, :], 0.0)
    return out


def _slab_spec(rows, slab, nchunk):
    return pl.BlockSpec((rows, MIX_W), lambda s, c: (s * nchunk + c, slab))


def _recurrence(kind, proj32, nseq, seqlen, chunk, gain, st0, lb_logits=None, layer=0):
    nchunk = seqlen // chunk
    n = nseq * seqlen
    row = pl.BlockSpec((1, MIX_W), lambda s, c: (0, 0))
    st_spec = pl.BlockSpec((1, MIX_W, MIX_W), lambda s, c: (s, 0, 0))
    out_specs = [pl.BlockSpec((chunk, MIX_W), lambda s, c: (s * nchunk + c, 0)), st_spec]
    out_shape = [jax.ShapeDtypeStruct((n, MIX_W), F32),
                 jax.ShapeDtypeStruct((nseq, MIX_W, MIX_W), F32)]
    if kind == "ret":
        slabs = (S_RQ, S_RK, S_RV, S_RG)
        body = functools.partial(_retention_kernel, chunk=chunk)
        extra_specs, extra, scratch = [row, st_spec], (gain, st0), []
    else:
        slabs = (S_GQ, S_GF, S_GI, S_GG)
        body = functools.partial(_hgrn_kernel, chunk=chunk, layer=layer)
        extra_specs = [row, pl.BlockSpec(lb_logits.shape, lambda s, c: (0, 0)), st_spec]
        extra = (gain, lb_logits, st0)
        scratch = [pltpu.VMEM((chunk, MIX_W), F32), pltpu.VMEM((chunk, MIX_W), F32)]
    return pl.pallas_call(
        body,
        grid=(nseq, nchunk),
        in_specs=[_slab_spec(chunk, s, nchunk) for s in slabs] + extra_specs,
        out_specs=out_specs,
        out_shape=out_shape,
        scratch_shapes=scratch,
        compiler_params=_params("parallel", "arbitrary"),
        name="recurrence_" + kind,
    )(proj32, proj32, proj32, proj32, *extra)


def _state_to_bd(s):
    b = s.shape[0]
    st = jnp.swapaxes(s, 2, 3)
    eye = jnp.eye(HEADS, dtype=s.dtype)
    bd = st[:, :, :, None, :] * eye[None, :, None, :, None]
    return bd.reshape(b, MIX_W, MIX_W)


def _bd_to_state(bd):
    b = bd.shape[0]
    r = bd.reshape(b, HEADS, HEAD_W, HEADS, HEAD_W)
    st = jnp.stack([r[:, h, :, h, :] for h in range(HEADS)], axis=1)
    return jnp.swapaxes(st, 2, 3)


def _sb_prompt_kernel(q_ref, k_ref, v_ref, o_ref, acc_ref, *, tq, tk):
    qi = pl.program_id(1)
    head = _lane_head()
    q = q_ref[...].astype(F32) * (HEAD_W ** -0.5)
    qh = [jnp.where(head == h, q, 0.0).astype(BF16) for h in range(HEADS)]
    ri = lax.broadcasted_iota(jnp.int32, (tk, tk), 0)
    ci = lax.broadcasted_iota(jnp.int32, (tk, tk), 1)
    later16 = jnp.where(ri > ci, 1.0, 0.0).astype(BF16)
    qpos = qi * tq + lax.broadcasted_iota(jnp.int32, (tq, 1), 0)
    acc_ref[...] = jnp.zeros_like(acc_ref)

    def cond(state):
        return jnp.logical_and(state[0] >= 0, state[1] > SB_DEAD_LOG)

    def body(state):
        kb = state[0]
        carries = state[2:]
        start = pl.multiple_of(kb * tk, tk)
        kblk = k_ref[pl.ds(start, tk), :]
        vblk = v_ref[pl.ds(start, tk), :]
        mask = (start + lax.broadcasted_iota(jnp.int32, (1, tk), 1)) < qpos
        new_carries = []
        alive = jnp.float32(-jnp.inf)
        for h in range(HEADS):
            z = _dot_nt(qh[h], kblk)
            ls = _log_sigmoid(z)
            u = jnp.where(mask, ls - z, 0.0)
            between = carries[h] + _split_dot(u, later16)
            w = jnp.where(mask, jnp.exp(ls + between), 0.0)
            acc_ref[...] += jnp.where(head == h, _dot(w.astype(BF16), vblk), 0.0)
            c_new = carries[h] + jnp.sum(u, axis=1, keepdims=True)
            new_carries.append(c_new)
            alive = jnp.maximum(alive, jnp.max(c_new))
        return (kb - 1, alive) + tuple(new_carries)

    zero = jnp.zeros((tq, 1), F32)
    lax.while_loop(cond, body, (qi * tq // tk, jnp.float32(0.0), zero, zero, zero, zero))
    o_ref[...] = acc_ref[...]


def _sb_prompt(proj32, proj16, nseq, seqlen):
    tq = tk = 256
    nq = seqlen // tq
    return pl.pallas_call(
        functools.partial(_sb_prompt_kernel, tq=tq, tk=tk),
        grid=(nseq, nq),
        in_specs=[pl.BlockSpec((tq, MIX_W), lambda b, i: (b * nq + i, S_SQ)),
                  pl.BlockSpec((seqlen, MIX_W), lambda b, i: (b, S_SK)),
                  pl.BlockSpec((seqlen, MIX_W), lambda b, i: (b, S_SV))],
        out_specs=pl.BlockSpec((tq, MIX_W), lambda b, i: (b * nq + i, 0)),
        out_shape=jax.ShapeDtypeStruct((nseq * seqlen, MIX_W), F32),
        scratch_shapes=[pltpu.VMEM((tq, MIX_W), F32)],
        compiler_params=_params("parallel", "arbitrary"),
        name="sb_prompt",
    )(proj32, proj16, proj16)


def _diff_lambda(lam_ref, layer):
    lp = lam_ref[...]
    lam_init = 0.8 - 0.6 * math.exp(-0.3 * layer)
    lam = (jnp.exp(jnp.sum(lp[0:1, :] * lp[1:2, :], axis=1, keepdims=True))
           - jnp.exp(jnp.sum(lp[2:3, :] * lp[3:4, :], axis=1, keepdims=True)) + lam_init)
    return lam, lam_init


def _combo_masks():
    lane = lax.broadcasted_iota(jnp.int32, (1, MIX_W), 1)
    return [(lane // DIFF_DH) == (2 * (c // 2) + (c % 2)) for c in range(2 * HEADS)]


def _diff_prompt_kernel(q_ref, k_ref, v_ref, lam_ref, gain_ref, o_ref, m_ref, acc_ref,
                        *, tq, tk, layer):
    qi = pl.program_id(1)
    head = _lane_head()
    q = q_ref[...] * (DIFF_DH ** -0.5)
    qc = [jnp.where(mk, q, 0.0).astype(BF16) for mk in _combo_masks()]
    m_ref[...] = jnp.full(m_ref.shape, MASK_VALUE, F32)
    acc_ref[...] = jnp.zeros_like(acc_ref)
    qpos = qi * tq + lax.broadcasted_iota(jnp.int32, (tq, 1), 0)

    def block(kb, masked):
        start = pl.multiple_of(kb * tk, tk)
        kblk = k_ref[pl.ds(start, tk), :]
        vblk = v_ref[pl.ds(start, tk), :]
        vaug = [jnp.where(head == h, vblk, jnp.ones_like(vblk)) for h in range(HEADS)]
        if masked:
            mask = (start + lax.broadcasted_iota(jnp.int32, (1, tk), 1)) <= qpos
        for c in range(2 * HEADS):
            s = _dot_nt(qc[c], kblk)
            if masked:
                s = jnp.where(mask, s, MASK_VALUE)
            m_old = m_ref[c]
            m_new = jnp.maximum(m_old, jnp.max(s, axis=1, keepdims=True))
            p = jnp.exp(s - m_new)
            acc_ref[c] = jnp.exp(m_old - m_new) * acc_ref[c] + _dot(p.astype(BF16), vaug[c // 2])
            m_ref[c] = m_new

    def unmasked(kb, carry):
        block(kb, False)
        return carry

    diag = qi * tq // tk
    lax.fori_loop(0, diag, unmasked, 0)
    block(diag, True)

    lam, lam_init = _diff_lambda(lam_ref, layer)
    o = jnp.zeros((tq, MIX_W), F32)
    for h in range(HEADS):
        a1, a2 = acc_ref[2 * h], acc_ref[2 * h + 1]
        o1 = a1 / pltpu.roll(a1, HEAD_W, 1)
        o2 = a2 / pltpu.roll(a2, HEAD_W, 1)
        o = jnp.where(head == h, o1 - lam * o2, o)
    o_ref[...] = _head_rms(o, gain_ref[...], _block_diag16()) * (1.0 - lam_init)


def _diff_prompt(proj32, proj16, nseq, seqlen, lam_p, gain, layer):
    tq = tk = 256
    nq = seqlen // tq
    return pl.pallas_call(
        functools.partial(_diff_prompt_kernel, tq=tq, tk=tk, layer=layer),
        grid=(nseq, nq),
        in_specs=[pl.BlockSpec((tq, MIX_W), lambda b, i: (b * nq + i, S_DQ)),
                  pl.BlockSpec((seqlen, MIX_W), lambda b, i: (b, S_DK)),
                  pl.BlockSpec((seqlen, MIX_W), lambda b, i: (b, S_DV)),
                  pl.BlockSpec(lam_p.shape, lambda b, i: (0, 0)),
                  pl.BlockSpec((1, MIX_W), lambda b, i: (0, 0))],
        out_specs=pl.BlockSpec((tq, MIX_W), lambda b, i: (b * nq + i, 0)),
        out_shape=jax.ShapeDtypeStruct((nseq * seqlen, MIX_W), F32),
        scratch_shapes=[pltpu.VMEM((2 * HEADS, tq, 1), F32),
                        pltpu.VMEM((2 * HEADS, tq, MIX_W), F32)],
        compiler_params=_params("parallel", "arbitrary"),
        name="diff_prompt",
    )(proj32, proj16, proj16, lam_p, gain)


def _stack_rows(x, masks):
    return jnp.concatenate([jnp.where(mk, x, 0.0) for mk in masks], axis=0)


def _sample_attn_kernel(pt_ref, q_ref, sk_ref, sv_ref, dq_ref, dk_ref, dv_ref,
                        psk_ref, psv_ref, pdk_ref, pdv_ref, lam_ref, gain_ref,
                        osb_ref, odf_ref,
                        sb_acc, sb_carry, df_m, df_l, df_acc, *, ls, n_pages, layer):
    del pt_ref
    p = pl.program_id(1)
    head = _lane_head()
    sb_masks = [head == h for h in range(HEADS)]
    qs = _stack_rows(q_ref[...] * (HEAD_W ** -0.5), sb_masks).astype(BF16)
    qd = _stack_rows(dq_ref[...] * (DIFF_DH ** -0.5), _combo_masks()).astype(BF16)
    ri = lax.broadcasted_iota(jnp.int32, (PAGE_SIZE, PAGE_SIZE), 0)
    ci = lax.broadcasted_iota(jnp.int32, (PAGE_SIZE, PAGE_SIZE), 1)
    later16 = jnp.where(ri > ci, 1.0, 0.0).astype(BF16)

    def sb_block(kblk, vblk, mask):
        z = _dot_nt(qs, kblk)
        lsg = _log_sigmoid(z)
        u = lsg - z
        if mask is not None:
            u = jnp.where(mask, u, 0.0)
        between = sb_carry[...] + _split_dot(u, later16)
        w = jnp.exp(lsg + between)
        if mask is not None:
            w = jnp.where(mask, w, 0.0)
        sb_acc[...] += _dot(w.astype(BF16), vblk)
        sb_carry[...] += jnp.sum(u, axis=1, keepdims=True)

    def df_block(kblk, vblk, mask):
        s = _dot_nt(qd, kblk)
        if mask is not None:
            s = jnp.where(mask, s, MASK_VALUE)
        m_old = df_m[...]
        m_new = jnp.maximum(m_old, jnp.max(s, axis=1, keepdims=True))
        pr = jnp.exp(s - m_new)
        alpha = jnp.exp(m_old - m_new)
        df_l[...] = alpha * df_l[...] + jnp.sum(pr, axis=1, keepdims=True)
        df_acc[...] = alpha * df_acc[...] + _dot(pr.astype(BF16), vblk)
        df_m[...] = m_new

    def pad_page(x):
        return jnp.concatenate([x, jnp.zeros((PAGE_SIZE - ls, MIX_W), F32)], axis=0).astype(BF16)

    @pl.when(p == 0)
    def _():
        sb_acc[...] = jnp.zeros_like(sb_acc)
        sb_carry[...] = jnp.zeros_like(sb_carry)
        df_m[...] = jnp.full(df_m.shape, MASK_VALUE, F32)
        df_l[...] = jnp.zeros_like(df_l)
        df_acc[...] = jnp.zeros_like(df_acc)
        key = lax.broadcasted_iota(jnp.int32, (1, PAGE_SIZE), 1)
        t_sb = lax.broadcasted_iota(jnp.int32, (HEADS * ls, 1), 0) % ls
        t_df = lax.broadcasted_iota(jnp.int32, (2 * HEADS * ls, 1), 0) % ls
        sb_block(pad_page(sk_ref[...]), pad_page(sv_ref[...]), key < t_sb)
        df_block(pad_page(dk_ref[...]), pad_page(dv_ref[...]), key <= t_df)

    @pl.when(jnp.max(sb_carry[...]) > SB_DEAD_LOG)
    def _():
        sb_block(psk_ref[...].astype(BF16), psv_ref[...].astype(BF16), None)

    df_block(pdk_ref[...].astype(BF16), pdv_ref[...].astype(BF16), None)

    @pl.when(p == n_pages - 1)
    def _():
        acc = sb_acc[...]
        o = jnp.zeros((ls, MIX_W), F32)
        for h in range(HEADS):
            o = jnp.where(head == h, acc[h * ls:(h + 1) * ls, :], o)
        osb_ref[...] = o
        lam, lam_init = _diff_lambda(lam_ref, layer)
        on = df_acc[...] / df_l[...]
        o = jnp.zeros((ls, MIX_W), F32)
        for h in range(HEADS):
            o1 = on[(2 * h) * ls:(2 * h + 1) * ls, :]
            o2 = on[(2 * h + 1) * ls:(2 * h + 2) * ls, :]
            o = jnp.where(head == h, o1 - lam * o2, o)
        odf_ref[...] = _head_rms(o, gain_ref[...], _block_diag16()) * (1.0 - lam_init)


def _sample_attn(proj32, page_table, caches, nseq, ls, lam_p, gain, layer):
    n_pages = page_table.shape[1]

    def slab(c):
        return pl.BlockSpec((ls, MIX_W), lambda b, p, pt: (b, c))

    page = pl.BlockSpec((None, None, PAGE_SIZE, MIX_W),
                        lambda b, p, pt: (layer, pt[b, n_pages - 1 - p], 0, 0))
    out = pl.BlockSpec((ls, MIX_W), lambda b, p, pt: (b, 0))
    rows_sb, rows_df = HEADS * ls, 2 * HEADS * ls
    grid_spec = pltpu.PrefetchScalarGridSpec(
        num_scalar_prefetch=1,
        grid=(nseq, n_pages),
        in_specs=[slab(S_SQ), slab(S_SK), slab(S_SV), slab(S_DQ), slab(S_DK), slab(S_DV),
                  page, page, page, page,
                  pl.BlockSpec(lam_p.shape, lambda b, p, pt: (0, 0)),
                  pl.BlockSpec((1, MIX_W), lambda b, p, pt: (0, 0))],
        out_specs=[out, out],
        scratch_shapes=[pltpu.VMEM((rows_sb, MIX_W), F32), pltpu.VMEM((rows_sb, 1), F32),
                        pltpu.VMEM((rows_df, 1), F32), pltpu.VMEM((rows_df, 1), F32),
                        pltpu.VMEM((rows_df, MIX_W), F32)])
    return pl.pallas_call(
        functools.partial(_sample_attn_kernel, ls=ls, n_pages=n_pages, layer=layer),
        grid_spec=grid_spec,
        out_shape=[jax.ShapeDtypeStruct((nseq * ls, MIX_W), F32)] * 2,
        compiler_params=_params("parallel", "arbitrary"),
        name="sample_attn",
    )(page_table, *([proj32] * 6), *caches, lam_p, gain)


def _merge_kernel(x_ref, b0_ref, b1_ref, b2_ref, b3_ref, pre_ref, wg_ref, wbr_ref, wo_ref,
                  post_ref, o_ref):
    x = x_ref[...]
    h = _rms(x, pre_ref[...]).astype(BF16)
    merged = None
    for i, b_ref in enumerate((b0_ref, b1_ref, b2_ref, b3_ref)):
        gate = jax.nn.sigmoid(_dot(h, wg_ref[:, i * D_MODEL:(i + 1) * D_MODEL]))
        term = gate * _dot(b_ref[...].astype(BF16), wbr_ref[i])
        merged = term if merged is None else merged + term
    o_ref[...] = x + _rms(_dot(merged.astype(BF16), wo_ref[...]), post_ref[...])


def _const_spec(shape):
    nd = len(shape)
    return pl.BlockSpec(shape, lambda i: (0,) * nd, pipeline_mode=pl.Buffered(1))


def _merge(x, branches, pre, wg16, wbr16, wo16, post):
    n = x.shape[0]
    tm = _row_tile(n, 256)
    tok = pl.BlockSpec((tm, D_MODEL), lambda i: (i, 0))
    br = pl.BlockSpec((tm, MIX_W), lambda i: (i, 0))
    return pl.pallas_call(
        _merge_kernel,
        grid=(n // tm,),
        in_specs=[tok, br, br, br, br, _const_spec((1, D_MODEL)), _const_spec(wg16.shape),
                  _const_spec(wbr16.shape), _const_spec(wo16.shape), _const_spec((1, D_MODEL))],
        out_specs=tok,
        out_shape=jax.ShapeDtypeStruct((n, D_MODEL), F32),
        compiler_params=_params("parallel"),
        name="merge",
    )(x, *branches, pre, wg16, wbr16, wo16, post)


def _cross_attn_kernel(q_ref, mk_ref, mv_ref, o_ref):
    head = _lane_head()
    q = q_ref[...] * (HEAD_W ** -0.5)
    mk = mk_ref[...].astype(BF16)
    mv = mv_ref[...].astype(BF16)
    o = jnp.zeros(q.shape, F32)
    for h in range(HEADS):
        s = _dot_nt(jnp.where(head == h, q, 0.0).astype(BF16), mk)
        p = jnp.exp(s - jnp.max(s, axis=1, keepdims=True))
        pv = _dot(p.astype(BF16), mv) / jnp.sum(p, axis=1, keepdims=True)
        o = jnp.where(head == h, pv, o)
    o_ref[...] = o


def _cross_attn(cq, mk, mv, nseq, seqlen):
    tm = _row_tile(seqlen, 512)
    nt = seqlen // tm
    mem = pl.BlockSpec((None, N_MEM, MIX_W), lambda b, i: (b, 0, 0))
    tok = pl.BlockSpec((tm, MIX_W), lambda b, i: (b * nt + i, 0))
    return pl.pallas_call(
        _cross_attn_kernel,
        grid=(nseq, nt),
        in_specs=[tok, mem, mem],
        out_specs=tok,
        out_shape=jax.ShapeDtypeStruct(cq.shape, F32),
        compiler_params=_params("parallel", "parallel"),
        name="cross_attn",
    )(cq, mk, mv)


def _post_kernel(x_ref, co_ref, wco_ref, cpost_ref, mpre_ref, wup_ref, wdn_ref, mpost_ref, o_ref):
    x = x_ref[...] + _rms(_dot(co_ref[...].astype(BF16), wco_ref[...]), cpost_ref[...])
    h = _rms(x, mpre_ref[...]).astype(BF16)
    u = jnp.square(jnp.maximum(_dot(h, wup_ref[...]), 0.0))
    o_ref[...] = x + _rms(_dot(u.astype(BF16), wdn_ref[...]), mpost_ref[...])


def _post(x, co, wco16, cpost, mpre, wup16, wdn16, mpost):
    n = x.shape[0]
    tm = _row_tile(n, 256)
    tok = pl.BlockSpec((tm, D_MODEL), lambda i: (i, 0))
    return pl.pallas_call(
        _post_kernel,
        grid=(n // tm,),
        in_specs=[tok, pl.BlockSpec((tm, MIX_W), lambda i: (i, 0)), _const_spec(wco16.shape),
                  _const_spec((1, D_MODEL)), _const_spec((1, D_MODEL)), _const_spec(wup16.shape),
                  _const_spec(wdn16.shape), _const_spec((1, D_MODEL))],
        out_specs=tok,
        out_shape=jax.ShapeDtypeStruct((n, D_MODEL), F32),
        compiler_params=_params("parallel"),
        name="ca_out_mlp",
    )(x, co, wco16, cpost, mpre, wup16, wdn16, mpost)


def _trunk_layer(layer, x, nseq, seqlen, rope_tab, mem_k, mem_v, st_ret0, st_hgrn0, w,
                 attn_fn, rec_chunk):
    proj32, proj16 = _mixer_proj(x, w["mix_pre_norm"], w["w_in"], rope_tab)
    o_ret, st_ret = _recurrence("ret", proj32, nseq, seqlen, rec_chunk["ret"], w["ret_norm"], st_ret0)
    o_hgrn, st_hgrn = _recurrence("hgrn", proj32, nseq, seqlen, rec_chunk["hgrn"], w["hgrn_norm"],
                                  st_hgrn0, w["hgrn_lb_logits"], layer)
    o_sb, o_diff = attn_fn(proj32, proj16)
    x = _merge(x, (o_ret, o_sb, o_diff, o_hgrn), w["mix_pre_norm"], w["w_gate"], w["w_br"],
               w["w_out"], w["mix_post_norm"])
    cq = _norm_proj(x, w["ca_pre_norm"], w["w_ca_q"])
    co = _cross_attn(cq, mem_k, mem_v, nseq, seqlen)
    x = _post(x, co, w["w_ca_o"], w["ca_post_norm"], w["mlp_pre_norm"], w["w_mlp_up"],
              w["w_mlp_down"], w["mlp_post_norm"])
    return x, proj32, st_ret, st_hgrn


def kernel(x_prompt, x_sample, mem_prompt, cache_sb_k, cache_sb_v, cache_diff_k, cache_diff_v, cache_mem_k, cache_mem_v, state_ret, state_hgrn, page_table, mix_pre_norm, mix_post_norm, ca_pre_norm, ca_post_norm, mlp_pre_norm, mlp_post_norm, mem_norm, w_in, w_gate, ret_norm, diff_lambda, diff_norm, hgrn_lb_logits, hgrn_norm, w_br_ret, w_br_sb, w_br_diff, w_br_hgrn, w_out, w_ca_q, w_ca_k, w_ca_v, w_ca_o, w_mlp_up, w_mlp_down):
    depth = w_in.shape[0]
    B, L, _ = x_prompt.shape
    DB, LS, _ = x_sample.shape
    n_pages = page_table.shape[1]
    past_len = n_pages * PAGE_SIZE

    tab_p = _rope_table(jnp.arange(L, dtype=jnp.int32))
    tab_s = jnp.tile(_rope_table(past_len + jnp.arange(LS, dtype=jnp.int32)), (256 // LS, 1))
    caches = [c.reshape(c.shape[0], c.shape[1], PAGE_SIZE, MIX_W)
              for c in (cache_sb_k, cache_sb_v, cache_diff_k, cache_diff_v)]
    w_br = jnp.stack([w_br_ret, w_br_sb, w_br_diff, w_br_hgrn], axis=1).astype(BF16)
    w_ca_kv = jnp.concatenate([w_ca_k, w_ca_v], axis=-1).astype(BF16)
    diff_gain = jnp.tile(diff_norm, (1, HEADS))

    xp = x_prompt.reshape(B * L, D_MODEL)
    xs = x_sample.reshape(DB * LS, D_MODEL)
    mem = mem_prompt.reshape(B * N_MEM, D_MODEL)
    zero_state = jnp.zeros((B, MIX_W, MIX_W), F32)
    outs_p, outs_s = [], []
    for l in range(depth):
        row = lambda a: a[l][None, :]
        w = {
            "mix_pre_norm": row(mix_pre_norm), "mix_post_norm": row(mix_post_norm),
            "ca_pre_norm": row(ca_pre_norm), "ca_post_norm": row(ca_post_norm),
            "mlp_pre_norm": row(mlp_pre_norm), "mlp_post_norm": row(mlp_post_norm),
            "ret_norm": row(ret_norm), "hgrn_norm": row(hgrn_norm),
            "hgrn_lb_logits": hgrn_lb_logits,
            "w_in": w_in[l].astype(BF16), "w_gate": w_gate[l].astype(BF16), "w_br": w_br[l],
            "w_out": w_out[l].astype(BF16), "w_ca_q": w_ca_q[l].astype(BF16),
            "w_ca_o": w_ca_o[l].astype(BF16), "w_mlp_up": w_mlp_up[l].astype(BF16),
            "w_mlp_down": w_mlp_down[l].astype(BF16),
        }
        lam_p, dgain = diff_lambda[l], diff_gain[l][None, :]

        mkv = _norm_proj(mem, row(mem_norm), w_ca_kv[l])
        mk_p = mkv[:, :MIX_W].reshape(B, N_MEM, MIX_W)
        mv_p = mkv[:, MIX_W:].reshape(B, N_MEM, MIX_W)

        def attn_p(proj32, proj16, l=l, lam_p=lam_p, dgain=dgain):
            return (_sb_prompt(proj32, proj16, B, L),
                    _diff_prompt(proj32, proj16, B, L, lam_p, dgain, l))

        xp, proj_p, ret_p, hgrn_p = _trunk_layer(
            l, xp, B, L, tab_p, mk_p, mv_p, zero_state, zero_state, w, attn_p,
            {"ret": 256, "hgrn": 64})
        outs_p.append((proj_p, ret_p, hgrn_p, mk_p, mv_p))

        def attn_s(proj32, proj16, l=l, lam_p=lam_p, dgain=dgain):
            del proj16
            return _sample_attn(proj32, page_table, caches, DB, LS, lam_p, dgain, l)

        xs, proj_s, ret_s, hgrn_s = _trunk_layer(
            l, xs, DB, LS, tab_s, cache_mem_k[l].reshape(DB, N_MEM, MIX_W),
            cache_mem_v[l].reshape(DB, N_MEM, MIX_W), _state_to_bd(state_ret[l]),
            _state_to_bd(state_hgrn[l]), w, attn_s, {"ret": LS, "hgrn": LS})
        outs_s.append((proj_s, ret_s, hgrn_s))

    def slab(outs, c, nseq, seqlen):
        return jnp.stack([o[0][:, c * MIX_W:(c + 1) * MIX_W].reshape(nseq, seqlen, HEADS, HEAD_W)
                          for o in outs], axis=0)

    def states(outs, i):
        return jnp.stack([_bd_to_state(o[i]) for o in outs], axis=0)

    def memkv(i):
        return jnp.stack([o[i].reshape(B, N_MEM, HEADS, HEAD_W) for o in outs_p], axis=0)

    return (xp.reshape(B, L, D_MODEL), xs.reshape(DB, LS, D_MODEL),
            slab(outs_p, S_SK, B, L), slab(outs_p, S_SV, B, L),
            slab(outs_p, S_DK, B, L), slab(outs_p, S_DV, B, L),
            memkv(3), memkv(4), states(outs_p, 1), states(outs_p, 2),
            slab(outs_s, S_SK, DB, LS), slab(outs_s, S_SV, DB, LS),
            slab(outs_s, S_DK, DB, LS), slab(outs_s, S_DV, DB, LS),
            states(outs_s, 1), states(outs_s, 2))
```

```python
import functools
import math

import jax
import jax.numpy as jnp
import numpy as np
from jax import lax
from jax.experimental import pallas as pl
from jax.experimental.pallas import tpu as pltpu

F32 = jnp.float32
BF16 = jnp.bfloat16

D_MODEL = 1024
N_MEM = 256
HEADS = 4
HEAD_W = 64
MIX_W = HEADS * HEAD_W
DIFF_DH = 32
D_FF = 4 * D_MODEL
N_BRANCH = 4
PAGE_SIZE = 128
ROPE_THETA = 10000.0
NORM_EPS = 1e-6
MASK_VALUE = -1e30
LOG2_E = math.log2(math.e)
LANES = 128
N_SLABS = 14
D_IN = N_SLABS * MIX_W
(S_RQ, S_RK, S_RV, S_RG, S_SQ, S_SK, S_SV, S_DQ, S_DK, S_DV,
 S_GQ, S_GF, S_GI, S_GG) = range(N_SLABS)

SB_DEAD_LOG = -104.0

V7X_VMEM_BYTES = 64 * 1024 * 1024
VMEM_LIMIT = (V7X_VMEM_BYTES * 7) // 8

_NT = (((1,), (1,)), ((), ()))
_TN = (((0,), (0,)), ((), ()))


def _params(*sem):
    return pltpu.CompilerParams(dimension_semantics=sem, vmem_limit_bytes=VMEM_LIMIT)


def _dot(a, b):
    return jnp.dot(a, b, preferred_element_type=F32)


def _dot_nt(a, b):
    return lax.dot_general(a, b, _NT, preferred_element_type=F32)


def _dot_tn(a, b):
    return lax.dot_general(a, b, _TN, preferred_element_type=F32)


def _split_dot(x, m16, terms=2, left=False):
    out = None
    r = x
    for _ in range(terms):
        p = r.astype(BF16)
        d = _dot(m16, p) if left else _dot(p, m16)
        out = d if out is None else out + d
        r = r - p.astype(F32)
    return out


def _rms(x, g):
    ms = jnp.mean(x * x, axis=-1, keepdims=True)
    return x * lax.rsqrt(ms + NORM_EPS) * g


def _lane_head(width=MIX_W):
    return lax.broadcasted_iota(jnp.int32, (1, width), 1) // HEAD_W


def _block_diag16():
    r = lax.broadcasted_iota(jnp.int32, (MIX_W, MIX_W), 0) // HEAD_W
    c = lax.broadcasted_iota(jnp.int32, (MIX_W, MIX_W), 1) // HEAD_W
    return jnp.where(r == c, 1.0, 0.0).astype(BF16)


def _head_rms(o, gain, bd16):
    ms = _split_dot(o * o, bd16) * (1.0 / HEAD_W)
    return o * lax.rsqrt(ms + NORM_EPS) * gain


def _log_sigmoid(z):
    return jnp.minimum(z, 0.0) - jnp.log(1.0 + jnp.exp(-jnp.abs(z)))


def _rope_slab(s, cos, sin, half):
    lane = lax.broadcasted_iota(jnp.int32, (1, MIX_W), 1)
    first = (lane % (2 * half)) < half
    partner = jnp.where(first, pltpu.roll(s, MIX_W - half, 1), pltpu.roll(s, half, 1))
    return s * cos + partner * sin


def _norm_proj_kernel(x_ref, g_ref, w_ref, o32_ref):
    h = _rms(x_ref[...], g_ref[...]).astype(BF16)
    o32_ref[...] = _dot(h, w_ref[...])


def _mixer_proj_kernel(x_ref, g_ref, w_ref, tab_ref, o32_ref, o16_ref):
    h = _rms(x_ref[...], g_ref[...]).astype(BF16)
    y = _dot(h, w_ref[...])
    cos_r = tab_ref[:, 0 * MIX_W:1 * MIX_W]
    sin_r = tab_ref[:, 1 * MIX_W:2 * MIX_W]
    cos_d = tab_ref[:, 2 * MIX_W:3 * MIX_W]
    sin_d = tab_ref[:, 3 * MIX_W:4 * MIX_W]
    for c in range(N_SLABS):
        s = y[:, c * MIX_W:(c + 1) * MIX_W]
        if c == S_RQ:
            s = _rope_slab(s, cos_r, sin_r, HEAD_W // 2)
        elif c == S_RK:
            s = _rope_slab(s, cos_r, sin_r, HEAD_W // 2) * (HEAD_W ** -0.5)
        elif c in (S_DQ, S_DK):
            s = _rope_slab(s, cos_d, sin_d, DIFF_DH // 2)
        o32_ref[:, c * MIX_W:(c + 1) * MIX_W] = s
        o16_ref[:, c * MIX_W:(c + 1) * MIX_W] = s.astype(BF16)


def _row_tile(n, want):
    t = math.gcd(n, want)
    assert t % 8 == 0
    return t


def _norm_proj(x, gain, w16):
    n, d = x.shape
    wout = w16.shape[1]
    tm = _row_tile(n, 256)
    return pl.pallas_call(
        _norm_proj_kernel,
        grid=(n // tm,),
        in_specs=[pl.BlockSpec((tm, d), lambda i: (i, 0)),
                  pl.BlockSpec((1, d), lambda i: (0, 0)),
                  pl.BlockSpec((d, wout), lambda i: (0, 0))],
        out_specs=pl.BlockSpec((tm, wout), lambda i: (i, 0)),
        out_shape=jax.ShapeDtypeStruct((n, wout), F32),
        compiler_params=_params("parallel"),
        name="norm_proj",
    )(x, gain, w16)


def _mixer_proj(x, gain, w16, tab):
    n, d = x.shape
    tm = _row_tile(n, 256)
    tab_blocks = tab.shape[0] // tm
    return pl.pallas_call(
        _mixer_proj_kernel,
        grid=(n // tm,),
        in_specs=[pl.BlockSpec((tm, d), lambda i: (i, 0)),
                  pl.BlockSpec((1, d), lambda i: (0, 0)),
                  pl.BlockSpec((d, D_IN), lambda i: (0, 0)),
                  pl.BlockSpec((tm, 4 * MIX_W), lambda i: (i % tab_blocks, 0))],
        out_specs=[pl.BlockSpec((tm, D_IN), lambda i: (i, 0)),
                   pl.BlockSpec((tm, D_IN), lambda i: (i, 0))],
        out_shape=[jax.ShapeDtypeStruct((n, D_IN), F32),
                   jax.ShapeDtypeStruct((n, D_IN), BF16)],
        compiler_params=_params("parallel"),
        name="mixer_proj",
    )(x, gain, w16, tab)


def _rope_table(pos):
    def one(group):
        half = group // 2
        inv = ROPE_THETA ** (-jnp.arange(half, dtype=F32) * 2.0 / group)
        ang = pos.astype(F32)[:, None] * inv[None, :]
        cos, sin = jnp.cos(ang), jnp.sin(ang)
        reps = MIX_W // group
        return (jnp.tile(jnp.concatenate([cos, cos], -1), (1, reps)),
                jnp.tile(jnp.concatenate([-sin, sin], -1), (1, reps)))
    cr, sr = one(HEAD_W)
    cd, sd = one(DIFF_DH)
    return jnp.concatenate([cr, sr, cd, sd], axis=-1)


_RET_LOG_GAMMA = [float(np.log(1.0 - 2.0 ** (-5.0 - h))) for h in range(HEADS)]


def _state_step(st, q_dec, k_dec, v, decay_row):
    inter = _dot_nt(q_dec.astype(BF16), st.astype(BF16))
    upd = _dot_tn(v.astype(BF16), k_dec.astype(BF16))
    r = lax.broadcasted_iota(jnp.int32, (MIX_W, MIX_W), 0) // HEAD_W
    c = lax.broadcasted_iota(jnp.int32, (MIX_W, MIX_W), 1) // HEAD_W
    return inter, st * decay_row + jnp.where(r == c, upd, 0.0)


def _retention_kernel(q_ref, k_ref, v_ref, g_ref, gain_ref, st0_ref, o_ref, st_ref, *, chunk):
    @pl.when(pl.program_id(1) == 0)
    def _():
        st_ref[...] = st0_ref[...]

    q, k, v = q_ref[...], k_ref[...], v_ref[...]
    head = _lane_head()
    lg = jnp.zeros((1, MIX_W), F32)
    for h in range(HEADS):
        lg = jnp.where(head == h, _RET_LOG_GAMMA[h], lg)
    steps = lax.broadcasted_iota(jnp.int32, (chunk, 1), 0).astype(F32) + 1.0
    b = steps * lg
    b_last = float(chunk) * lg
    o, st_new = _state_step(st_ref[0], q * jnp.exp(b), k * jnp.exp(b_last - b), v,
                            jnp.exp(b_last))
    st_ref[0] = st_new

    ri = lax.broadcasted_iota(jnp.int32, (chunk, chunk), 0)
    ci = lax.broadcasted_iota(jnp.int32, (chunk, chunk), 1)
    causal = ri >= ci
    dist = (ri - ci).astype(F32)
    k16, v16 = k.astype(BF16), v.astype(BF16)
    for h in range(HEADS):
        sc = _dot_nt(jnp.where(head == h, q, 0.0).astype(BF16), k16)
        dec = jnp.where(causal, jnp.exp(dist * _RET_LOG_GAMMA[h]), 0.0)
        o = o + jnp.where(head == h, _dot((sc * dec).astype(BF16), v16), 0.0)

    g = g_ref[...]
    o_ref[...] = _head_rms(o, gain_ref[...], _block_diag16()) * (g * jax.nn.sigmoid(g))


def _hgrn_kernel(q_ref, f_ref, v_ref, g_ref, gain_ref, lbl_ref, st0_ref, o_ref, st_ref,
                 kk_scr, b_scr, *, chunk, nsub, layer):
    @pl.when(pl.program_id(1) == 0)
    def _():
        st_ref[...] = st0_ref[...]

    lbl = lbl_ref[...]
    e = jnp.exp(lbl - jnp.max(lbl, axis=0, keepdims=True))
    lb_w = e / jnp.sum(e, axis=0, keepdims=True)
    lb = lb_w[0:1, :]
    for i in range(1, layer + 1):
        lb = lb + lb_w[i:i + 1, :]
    lb = lb - lb_w[0:1, :]

    z_all = f_ref[...]
    kk_scr[...] = (1.0 - lb) * jax.nn.sigmoid(-z_all)
    log_f = jnp.log(lb + (1.0 - lb) * jax.nn.sigmoid(z_all))
    ri = lax.broadcasted_iota(jnp.int32, (chunk, chunk), 0)
    ci = lax.broadcasted_iota(jnp.int32, (chunk, chunk), 1)
    tril16 = jnp.where(ri >= ci, 1.0, 0.0).astype(BF16)
    bd16 = _block_diag16()
    rows = lax.broadcasted_iota(jnp.int32, (chunk, 1), 0)
    st = st_ref[0]
    for s in range(nsub):
        lo = s * chunk
        q, v, k = q_ref[lo:lo + chunk, :], v_ref[lo:lo + chunk, :], kk_scr[lo:lo + chunk, :]
        lf = log_f[lo:lo + chunk, :]
        b = _split_dot(lf, tril16, terms=3, left=True) if chunk >= 16 else _small_cumsum(lf, chunk)
        b_scr[lo:lo + chunk, :] = b
        b_last = b[chunk - 1:chunk, :]
        o, st = _state_step(st, q * jnp.exp(b), k * jnp.exp(b_last - b), v, jnp.exp(b_last))

        slabs = []
        for j in range(chunk):
            kj = kk_scr[lo + j:lo + j + 1, :]
            bj = b_scr[lo + j:lo + j + 1, :]
            slabs.append(jnp.where(rows >= j, q * kj * jnp.exp(b - bj), 0.0))
        pair_sum = _dot(jnp.concatenate(slabs, axis=0).astype(BF16), bd16)
        for j in range(chunk):
            o = o + pair_sum[j * chunk:(j + 1) * chunk, :] * v_ref[lo + j:lo + j + 1, :]
        g = g_ref[lo:lo + chunk, :]
        o_ref[lo:lo + chunk, :] = _head_rms(o, gain_ref[...], bd16) * (g * jax.nn.sigmoid(g))
    st_ref[0] = st


def _small_cumsum(x, rows):
    idx = lax.broadcasted_iota(jnp.int32, (rows, 1), 0)
    out = jnp.zeros_like(x)
    for j in range(rows):
        out = out + jnp.where(idx >= j, x[j:j + 1, :], 0.0)
    return out


def _slab_spec(rows, slab, nchunk):
    return pl.BlockSpec((rows, MIX_W), lambda s, c: (s * nchunk + c, slab))


def _recurrence(kind, proj32, nseq, seqlen, chunk, gain, st0, lb_logits=None, layer=0, nsub=1):
    rows = chunk * nsub
    nstep = seqlen // rows
    n = nseq * seqlen
    row = pl.BlockSpec((1, MIX_W), lambda s, c: (0, 0))
    st_spec = pl.BlockSpec((1, MIX_W, MIX_W), lambda s, c: (s, 0, 0))
    out_specs = [pl.BlockSpec((rows, MIX_W), lambda s, c: (s * nstep + c, 0)), st_spec]
    out_shape = [jax.ShapeDtypeStruct((n, MIX_W), F32),
                 jax.ShapeDtypeStruct((nseq, MIX_W, MIX_W), F32)]
    if kind == "ret":
        assert nsub == 1
        slabs = (S_RQ, S_RK, S_RV, S_RG)
        body = functools.partial(_retention_kernel, chunk=chunk)
        extra_specs, extra, scratch = [row, st_spec], (gain, st0), []
    else:
        slabs = (S_GQ, S_GF, S_GI, S_GG)
        body = functools.partial(_hgrn_kernel, chunk=chunk, nsub=nsub, layer=layer)
        extra_specs = [row, pl.BlockSpec(lb_logits.shape, lambda s, c: (0, 0)), st_spec]
        extra = (gain, lb_logits, st0)
        scratch = [pltpu.VMEM((rows, MIX_W), F32), pltpu.VMEM((rows, MIX_W), F32)]
    return pl.pallas_call(
        body,
        grid=(nseq, nstep),
        in_specs=[_slab_spec(rows, s, nstep) for s in slabs] + extra_specs,
        out_specs=out_specs,
        out_shape=out_shape,
        scratch_shapes=scratch,
        compiler_params=_params("parallel", "arbitrary"),
        name="recurrence_" + kind,
    )(proj32, proj32, proj32, proj32, *extra)


def _state_to_bd(s):
    b = s.shape[0]
    st = jnp.swapaxes(s, 2, 3)
    eye = jnp.eye(HEADS, dtype=s.dtype)
    bd = st[:, :, :, None, :] * eye[None, :, None, :, None]
    return bd.reshape(b, MIX_W, MIX_W)


def _bd_to_state(bd):
    b = bd.shape[0]
    r = bd.reshape(b, HEADS, HEAD_W, HEADS, HEAD_W)
    st = jnp.stack([r[:, h, :, h, :] for h in range(HEADS)], axis=1)
    return jnp.swapaxes(st, 2, 3)


def _sb_prompt_kernel(q_ref, k_ref, v_ref, o_ref, acc_ref, *, tq, tk):
    qi = pl.program_id(1)
    head = _lane_head()
    q = q_ref[...].astype(F32) * (HEAD_W ** -0.5)
    qh = [jnp.where(head == h, q, 0.0).astype(BF16) for h in range(HEADS)]
    ri = lax.broadcasted_iota(jnp.int32, (tk, tk), 0)
    ci = lax.broadcasted_iota(jnp.int32, (tk, tk), 1)
    later16 = jnp.where(ri > ci, 1.0, 0.0).astype(BF16)
    qpos = qi * tq + lax.broadcasted_iota(jnp.int32, (tq, 1), 0)
    acc_ref[...] = jnp.zeros_like(acc_ref)

    def cond(state):
        return jnp.logical_and(state[0] >= 0, state[1] > SB_DEAD_LOG)

    def body(state):
        kb = state[0]
        carries = state[2:]
        start = pl.multiple_of(kb * tk, tk)
        kblk = k_ref[pl.ds(start, tk), :]
        vblk = v_ref[pl.ds(start, tk), :]
        mask = (start + lax.broadcasted_iota(jnp.int32, (1, tk), 1)) < qpos
        new_carries = []
        alive = jnp.float32(-jnp.inf)
        for h in range(HEADS):
            z = _dot_nt(qh[h], kblk)
            ls = _log_sigmoid(z)
            u = jnp.where(mask, ls - z, 0.0)
            between = carries[h] + _split_dot(u, later16)
            w = jnp.where(mask, jnp.exp(ls + between), 0.0)
            acc_ref[...] += jnp.where(head == h, _dot(w.astype(BF16), vblk), 0.0)
            c_new = carries[h] + jnp.sum(u, axis=1, keepdims=True)
            new_carries.append(c_new)
            alive = jnp.maximum(alive, jnp.max(c_new))
        return (kb - 1, alive) + tuple(new_carries)

    zero = jnp.zeros((tq, 1), F32)
    lax.while_loop(cond, body, (qi * tq // tk, jnp.float32(0.0), zero, zero, zero, zero))
    o_ref[...] = acc_ref[...]


def _sb_prompt(proj32, proj16, nseq, seqlen):
    tq = tk = 256
    nq = seqlen // tq
    return pl.pallas_call(
        functools.partial(_sb_prompt_kernel, tq=tq, tk=tk),
        grid=(nseq, nq),
        in_specs=[pl.BlockSpec((tq, MIX_W), lambda b, i: (b * nq + i, S_SQ)),
                  pl.BlockSpec((seqlen, MIX_W), lambda b, i: (b, S_SK)),
                  pl.BlockSpec((seqlen, MIX_W), lambda b, i: (b, S_SV))],
        out_specs=pl.BlockSpec((tq, MIX_W), lambda b, i: (b * nq + i, 0)),
        out_shape=jax.ShapeDtypeStruct((nseq * seqlen, MIX_W), F32),
        scratch_shapes=[pltpu.VMEM((tq, MIX_W), F32)],
        compiler_params=_params("parallel", "arbitrary"),
        name="sb_prompt",
    )(proj32, proj16, proj16)


def _diff_lambda(lam_ref, layer):
    lp = lam_ref[...]
    lam_init = 0.8 - 0.6 * math.exp(-0.3 * layer)
    lam = (jnp.exp(jnp.sum(lp[0:1, :] * lp[1:2, :], axis=1, keepdims=True))
           - jnp.exp(jnp.sum(lp[2:3, :] * lp[3:4, :], axis=1, keepdims=True)) + lam_init)
    return lam, lam_init


def _combo_masks():
    lane = lax.broadcasted_iota(jnp.int32, (1, MIX_W), 1)
    return [(lane // DIFF_DH) == (2 * (c // 2) + (c % 2)) for c in range(2 * HEADS)]


def _diff_prompt_kernel(q_ref, k_ref, v_ref, lam_ref, gain_ref, o_ref, qs_ref, m_ref, acc_ref,
                        *, tq, tk, layer):
    qi = pl.program_id(1)
    head = _lane_head()
    q = q_ref[...] * (DIFF_DH ** -0.5 * LOG2_E)
    for c, mk in enumerate(_combo_masks()):
        qs_ref[c * tq:(c + 1) * tq, :] = jnp.where(mk, q, 0.0).astype(BF16)
    m_ref[...] = jnp.full(m_ref.shape, MASK_VALUE, F32)
    acc_ref[...] = jnp.zeros_like(acc_ref)
    qpos = qi * tq + lax.broadcasted_iota(jnp.int32, (2 * tq, 1), 0) % tq
    reps = tk // LANES

    def block(kb, masked):
        start = pl.multiple_of(kb * tk, tk)
        kblk = k_ref[pl.ds(start, tk), :]
        vblk = v_ref[pl.ds(start, tk), :]
        if masked:
            mask = (start + lax.broadcasted_iota(jnp.int32, (1, tk), 1)) <= qpos
        for h in range(HEADS):
            rows = slice(2 * h * tq, (2 * h + 2) * tq)
            s = _dot_nt(qs_ref[rows, :], kblk)
            if masked:
                s = jnp.where(mask, s, MASK_VALUE)
            m_old = m_ref[rows, :]
            m_new = jnp.maximum(m_old, jnp.max(s, axis=1, keepdims=True))
            m_ref[rows, :] = m_new
            p = jnp.exp2(s - jnp.concatenate([m_new] * reps, axis=1)).astype(BF16)
            vaug = jnp.where(head == h, vblk, jnp.ones_like(vblk))
            alpha = jnp.exp2(m_old - m_new)
            acc_ref[rows, :] = (jnp.concatenate([alpha] * (MIX_W // LANES), axis=1) * acc_ref[rows, :]
                                + _dot(p, vaug))

    def unmasked(kb, carry):
        block(kb, False)
        return carry

    diag = qi * tq // tk
    lax.fori_loop(0, diag, unmasked, 0)
    block(diag, True)

    lam, lam_init = _diff_lambda(lam_ref, layer)
    o = jnp.zeros((tq, MIX_W), F32)
    for h in range(HEADS):
        a1 = acc_ref[2 * h * tq:(2 * h + 1) * tq, :]
        a2 = acc_ref[(2 * h + 1) * tq:(2 * h + 2) * tq, :]
        o1 = a1 / pltpu.roll(a1, HEAD_W, 1)
        o2 = a2 / pltpu.roll(a2, HEAD_W, 1)
        o = jnp.where(head == h, o1 - lam * o2, o)
    o_ref[...] = _head_rms(o, gain_ref[...], _block_diag16()) * (1.0 - lam_init)


def _diff_prompt(proj32, proj16, nseq, seqlen, lam_p, gain, layer):
    tq = tk = 256
    nq = seqlen // tq
    return pl.pallas_call(
        functools.partial(_diff_prompt_kernel, tq=tq, tk=tk, layer=layer),
        grid=(nseq, nq),
        in_specs=[pl.BlockSpec((tq, MIX_W), lambda b, i: (b * nq + i, S_DQ)),
                  pl.BlockSpec((seqlen, MIX_W), lambda b, i: (b, S_DK)),
                  pl.BlockSpec((seqlen, MIX_W), lambda b, i: (b, S_DV)),
                  pl.BlockSpec(lam_p.shape, lambda b, i: (0, 0)),
                  pl.BlockSpec((1, MIX_W), lambda b, i: (0, 0))],
        out_specs=pl.BlockSpec((tq, MIX_W), lambda b, i: (b * nq + i, 0)),
        out_shape=jax.ShapeDtypeStruct((nseq * seqlen, MIX_W), F32),
        scratch_shapes=[pltpu.VMEM((2 * HEADS * tq, MIX_W), BF16),
                        pltpu.VMEM((2 * HEADS * tq, LANES), F32),
                        pltpu.VMEM((2 * HEADS * tq, MIX_W), F32)],
        compiler_params=_params("parallel", "arbitrary"),
        name="diff_prompt",
    )(proj32, proj16, proj16, lam_p, gain)


def _stack_rows(x, masks):
    return jnp.concatenate([jnp.where(mk, x, 0.0) for mk in masks], axis=0)


def _sample_attn_kernel(pt_ref, q_ref, sk_ref, sv_ref, dq_ref, dk_ref, dv_ref, *rest,
                        ls, n_pages, layer):
    del pt_ref
    pages, (lam_ref, gain_ref, osb_ref, odf_ref, sb_acc, sb_carry) = rest[:4 * n_pages], rest[4 * n_pages:]
    psk, psv, pdk, pdv = (pages[i * n_pages:(i + 1) * n_pages] for i in range(4))
    head = _lane_head()
    sb_masks = [head == h for h in range(HEADS)]
    qs = _stack_rows(q_ref[...] * (HEAD_W ** -0.5), sb_masks).astype(BF16)
    qd = _stack_rows(dq_ref[...] * (DIFF_DH ** -0.5), _combo_masks()).astype(BF16)

    def pad_page(x):
        return jnp.concatenate([x, jnp.zeros((PAGE_SIZE - ls, MIX_W), F32)], axis=0).astype(BF16)

    def later16(n):
        ri = lax.broadcasted_iota(jnp.int32, (n, n), 0)
        ci = lax.broadcasted_iota(jnp.int32, (n, n), 1)
        return jnp.where(ri > ci, 1.0, 0.0).astype(BF16)

    def sb_block(z, pv_fn, mask):
        lsg = _log_sigmoid(z)
        u = lsg - z
        if mask is not None:
            u = jnp.where(mask, u, 0.0)
        between = sb_carry[...] + _split_dot(u, later16(z.shape[1]))
        w = jnp.exp(lsg + between)
        if mask is not None:
            w = jnp.where(mask, w, 0.0)
        sb_acc[...] += pv_fn(w.astype(BF16))
        sb_carry[...] += jnp.sum(u, axis=1, keepdims=True)

    sb_acc[...] = jnp.zeros_like(sb_acc)
    sb_carry[...] = jnp.zeros_like(sb_carry)
    key = lax.broadcasted_iota(jnp.int32, (1, PAGE_SIZE), 1)
    t_sb = lax.broadcasted_iota(jnp.int32, (HEADS * ls, 1), 0) % ls
    sv_new = pad_page(sv_ref[...])
    sb_block(_dot_nt(qs, pad_page(sk_ref[...])), lambda w: _dot(w, sv_new), key < t_sb)
    for blk in range(n_pages // 2):
        hi = n_pages - 1 - 2 * blk

        @pl.when(jnp.max(sb_carry[...]) > SB_DEAD_LOG)
        def _(hi=hi):
            kt = jnp.concatenate([psk[hi - 1][...], psk[hi][...]], axis=1).astype(BF16)
            vt = jnp.concatenate([psv[hi - 1][...], psv[hi][...]], axis=1).astype(BF16)
            sb_block(_dot(qs, kt), lambda w: _dot_nt(w, vt), None)

    acc = sb_acc[...]
    o = jnp.zeros((ls, MIX_W), F32)
    for h in range(HEADS):
        o = jnp.where(head == h, acc[h * ls:(h + 1) * ls, :], o)
    osb_ref[...] = o

    t_df = lax.broadcasted_iota(jnp.int32, (2 * HEADS * ls, 1), 0) % ls
    s_new = jnp.where(key <= t_df, _dot_nt(qd, pad_page(dk_ref[...])), MASK_VALUE)
    kt_all = jnp.concatenate([r[...].astype(BF16) for r in pdk], axis=1)
    s_past = _dot(qd, kt_all)
    m = jnp.maximum(jnp.max(s_new, axis=1, keepdims=True), jnp.max(s_past, axis=1, keepdims=True))
    p_new = jnp.exp(s_new - m)
    p_past = jnp.exp(s_past - m)
    l = jnp.sum(p_new, axis=1, keepdims=True) + jnp.sum(p_past, axis=1, keepdims=True)
    vt_all = jnp.concatenate([r[...].astype(BF16) for r in pdv], axis=1)
    on = (_dot(p_new.astype(BF16), pad_page(dv_ref[...]))
          + _dot_nt(p_past.astype(BF16), vt_all)) / l
    lam, lam_init = _diff_lambda(lam_ref, layer)
    o = jnp.zeros((ls, MIX_W), F32)
    for h in range(HEADS):
        o1 = on[(2 * h) * ls:(2 * h + 1) * ls, :]
        o2 = on[(2 * h + 1) * ls:(2 * h + 2) * ls, :]
        o = jnp.where(head == h, o1 - lam * o2, o)
    odf_ref[...] = _head_rms(o, gain_ref[...], _block_diag16()) * (1.0 - lam_init)


def _sample_attn(proj32, page_table, caches, nseq, ls, lam_p, gain, layer):
    n_pages = page_table.shape[1]
    assert n_pages % 2 == 0

    def slab(c):
        return pl.BlockSpec((ls, MIX_W), lambda b, pt: (b, c))

    def page(p):
        return pl.BlockSpec((None, None, MIX_W, PAGE_SIZE), lambda b, pt: (layer, pt[b, p], 0, 0))

    out = pl.BlockSpec((ls, MIX_W), lambda b, pt: (b, 0))
    grid_spec = pltpu.PrefetchScalarGridSpec(
        num_scalar_prefetch=1,
        grid=(nseq,),
        in_specs=[slab(S_SQ), slab(S_SK), slab(S_SV), slab(S_DQ), slab(S_DK), slab(S_DV)]
                 + [page(p) for _ in range(4) for p in range(n_pages)]
                 + [pl.BlockSpec(lam_p.shape, lambda b, pt: (0, 0)),
                    pl.BlockSpec((1, MIX_W), lambda b, pt: (0, 0))],
        out_specs=[out, out],
        scratch_shapes=[pltpu.VMEM((HEADS * ls, MIX_W), F32), pltpu.VMEM((HEADS * ls, 1), F32)])
    page_args = [c for c in caches for _ in range(n_pages)]
    return pl.pallas_call(
        functools.partial(_sample_attn_kernel, ls=ls, n_pages=n_pages, layer=layer),
        grid_spec=grid_spec,
        out_shape=[jax.ShapeDtypeStruct((nseq * ls, MIX_W), F32)] * 2,
        compiler_params=_params("arbitrary"),
        name="sample_attn",
    )(page_table, *([proj32] * 6), *page_args, lam_p, gain)


def _merge_kernel(x_ref, b0_ref, b1_ref, b2_ref, b3_ref, pre_ref, wg_ref, wbr_ref, wo_ref,
                  post_ref, o_ref):
    x = x_ref[...]
    h = _rms(x, pre_ref[...]).astype(BF16)
    merged = None
    for i, b_ref in enumerate((b0_ref, b1_ref, b2_ref, b3_ref)):
        gate = jax.nn.sigmoid(_dot(h, wg_ref[:, i * D_MODEL:(i + 1) * D_MODEL]))
        term = gate * _dot(b_ref[...].astype(BF16), wbr_ref[i])
        merged = term if merged is None else merged + term
    o_ref[...] = x + _rms(_dot(merged.astype(BF16), wo_ref[...]), post_ref[...])


def _const_spec(shape):
    nd = len(shape)
    return pl.BlockSpec(shape, lambda i: (0,) * nd, pipeline_mode=pl.Buffered(1))


def _merge(x, branches, pre, wg16, wbr16, wo16, post):
    n = x.shape[0]
    tm = _row_tile(n, 256)
    tok = pl.BlockSpec((tm, D_MODEL), lambda i: (i, 0))
    br = pl.BlockSpec((tm, MIX_W), lambda i: (i, 0))
    return pl.pallas_call(
        _merge_kernel,
        grid=(n // tm,),
        in_specs=[tok, br, br, br, br, _const_spec((1, D_MODEL)), _const_spec(wg16.shape),
                  _const_spec(wbr16.shape), _const_spec(wo16.shape), _const_spec((1, D_MODEL))],
        out_specs=tok,
        out_shape=jax.ShapeDtypeStruct((n, D_MODEL), F32),
        compiler_params=_params("parallel"),
        name="merge",
    )(x, *branches, pre, wg16, wbr16, wo16, post)


def _cross_attn_kernel(q_ref, mk_ref, mv_ref, o_ref):
    head = _lane_head()
    q = q_ref[...] * (HEAD_W ** -0.5)
    mk = mk_ref[...].astype(BF16)
    mv = mv_ref[...].astype(BF16)
    o = jnp.zeros(q.shape, F32)
    for h in range(HEADS):
        s = _dot_nt(jnp.where(head == h, q, 0.0).astype(BF16), mk)
        p = jnp.exp(s - jnp.max(s, axis=1, keepdims=True))
        pv = _dot(p.astype(BF16), mv) / jnp.sum(p, axis=1, keepdims=True)
        o = jnp.where(head == h, pv, o)
    o_ref[...] = o


def _cross_attn(cq, mk, mv, nseq, seqlen):
    tm = _row_tile(seqlen, 512)
    nt = seqlen // tm
    mem = pl.BlockSpec((None, N_MEM, MIX_W), lambda b, i: (b, 0, 0))
    tok = pl.BlockSpec((tm, MIX_W), lambda b, i: (b * nt + i, 0))
    return pl.pallas_call(
        _cross_attn_kernel,
        grid=(nseq, nt),
        in_specs=[tok, mem, mem],
        out_specs=tok,
        out_shape=jax.ShapeDtypeStruct(cq.shape, F32),
        compiler_params=_params("parallel", "parallel"),
        name="cross_attn",
    )(cq, mk, mv)


def _post_kernel(x_ref, co_ref, wco_ref, cpost_ref, mpre_ref, wup_ref, wdn_ref, mpost_ref, o_ref):
    x = x_ref[...] + _rms(_dot(co_ref[...].astype(BF16), wco_ref[...]), cpost_ref[...])
    h = _rms(x, mpre_ref[...]).astype(BF16)
    u = jnp.square(jnp.maximum(_dot(h, wup_ref[...]), 0.0))
    o_ref[...] = x + _rms(_dot(u.astype(BF16), wdn_ref[...]), mpost_ref[...])


def _post(x, co, wco16, cpost, mpre, wup16, wdn16, mpost):
    n = x.shape[0]
    tm = _row_tile(n, 256)
    tok = pl.BlockSpec((tm, D_MODEL), lambda i: (i, 0))
    return pl.pallas_call(
        _post_kernel,
        grid=(n // tm,),
        in_specs=[tok, pl.BlockSpec((tm, MIX_W), lambda i: (i, 0)), _const_spec(wco16.shape),
                  _const_spec((1, D_MODEL)), _const_spec((1, D_MODEL)), _const_spec(wup16.shape),
                  _const_spec(wdn16.shape), _const_spec((1, D_MODEL))],
        out_specs=tok,
        out_shape=jax.ShapeDtypeStruct((n, D_MODEL), F32),
        compiler_params=_params("parallel"),
        name="ca_out_mlp",
    )(x, co, wco16, cpost, mpre, wup16, wdn16, mpost)


def _trunk_layer(layer, x, nseq, seqlen, rope_tab, mem_k, mem_v, st_ret0, st_hgrn0, w,
                 attn_fn, rec_chunk):
    proj32, proj16 = _mixer_proj(x, w["mix_pre_norm"], w["w_in"], rope_tab)
    o_ret, st_ret = _recurrence("ret", proj32, nseq, seqlen, rec_chunk["ret"], w["ret_norm"], st_ret0)
    o_hgrn, st_hgrn = _recurrence("hgrn", proj32, nseq, seqlen, rec_chunk["hgrn"], w["hgrn_norm"],
                                  st_hgrn0, w["hgrn_lb_logits"], layer, rec_chunk["hgrn_nsub"])
    o_sb, o_diff = attn_fn(proj32, proj16)
    x = _merge(x, (o_ret, o_sb, o_diff, o_hgrn), w["mix_pre_norm"], w["w_gate"], w["w_br"],
               w["w_out"], w["mix_post_norm"])
    cq = _norm_proj(x, w["ca_pre_norm"], w["w_ca_q"])
    co = _cross_attn(cq, mem_k, mem_v, nseq, seqlen)
    x = _post(x, co, w["w_ca_o"], w["ca_post_norm"], w["mlp_pre_norm"], w["w_mlp_up"],
              w["w_mlp_down"], w["mlp_post_norm"])
    return x, proj32, st_ret, st_hgrn


def kernel(x_prompt, x_sample, mem_prompt, cache_sb_k, cache_sb_v, cache_diff_k, cache_diff_v, cache_mem_k, cache_mem_v, state_ret, state_hgrn, page_table, mix_pre_norm, mix_post_norm, ca_pre_norm, ca_post_norm, mlp_pre_norm, mlp_post_norm, mem_norm, w_in, w_gate, ret_norm, diff_lambda, diff_norm, hgrn_lb_logits, hgrn_norm, w_br_ret, w_br_sb, w_br_diff, w_br_hgrn, w_out, w_ca_q, w_ca_k, w_ca_v, w_ca_o, w_mlp_up, w_mlp_down):
    depth = w_in.shape[0]
    B, L, _ = x_prompt.shape
    DB, LS, _ = x_sample.shape
    n_pages = page_table.shape[1]
    past_len = n_pages * PAGE_SIZE

    tab_p = _rope_table(jnp.arange(L, dtype=jnp.int32))
    tab_s = jnp.tile(_rope_table(past_len + jnp.arange(LS, dtype=jnp.int32)), (256 // LS, 1))
    caches = [jnp.transpose(c, (0, 1, 3, 4, 2)).reshape(c.shape[0], c.shape[1], MIX_W, PAGE_SIZE)
              for c in (cache_sb_k, cache_sb_v, cache_diff_k, cache_diff_v)]
    w_br = jnp.stack([w_br_ret, w_br_sb, w_br_diff, w_br_hgrn], axis=1).astype(BF16)
    w_ca_kv = jnp.concatenate([w_ca_k, w_ca_v], axis=-1).astype(BF16)
    diff_gain = jnp.tile(diff_norm, (1, HEADS))

    xp = x_prompt.reshape(B * L, D_MODEL)
    xs = x_sample.reshape(DB * LS, D_MODEL)
    mem = mem_prompt.reshape(B * N_MEM, D_MODEL)
    zero_state = jnp.zeros((B, MIX_W, MIX_W), F32)
    outs_p, outs_s = [], []
    for l in range(depth):
        row = lambda a: a[l][None, :]
        w = {
            "mix_pre_norm": row(mix_pre_norm), "mix_post_norm": row(mix_post_norm),
            "ca_pre_norm": row(ca_pre_norm), "ca_post_norm": row(ca_post_norm),
            "mlp_pre_norm": row(mlp_pre_norm), "mlp_post_norm": row(mlp_post_norm),
            "ret_norm": row(ret_norm), "hgrn_norm": row(hgrn_norm),
            "hgrn_lb_logits": hgrn_lb_logits,
            "w_in": w_in[l].astype(BF16), "w_gate": w_gate[l].astype(BF16), "w_br": w_br[l],
            "w_out": w_out[l].astype(BF16), "w_ca_q": w_ca_q[l].astype(BF16),
            "w_ca_o": w_ca_o[l].astype(BF16), "w_mlp_up": w_mlp_up[l].astype(BF16),
            "w_mlp_down": w_mlp_down[l].astype(BF16),
        }
        lam_p, dgain = diff_lambda[l], diff_gain[l][None, :]

        mkv = _norm_proj(mem, row(mem_norm), w_ca_kv[l])
        mk_p = mkv[:, :MIX_W].reshape(B, N_MEM, MIX_W)
        mv_p = mkv[:, MIX_W:].reshape(B, N_MEM, MIX_W)

        def attn_p(proj32, proj16, l=l, lam_p=lam_p, dgain=dgain):
            return (_sb_prompt(proj32, proj16, B, L),
                    _diff_prompt(proj32, proj16, B, L, lam_p, dgain, l))

        xp, proj_p, ret_p, hgrn_p = _trunk_layer(
            l, xp, B, L, tab_p, mk_p, mv_p, zero_state, zero_state, w, attn_p,
            {"ret": 256, "hgrn": 32, "hgrn_nsub": 8})
        outs_p.append((proj_p, ret_p, hgrn_p, mk_p, mv_p))

        def attn_s(proj32, proj16, l=l, lam_p=lam_p, dgain=dgain):
            del proj16
            return _sample_attn(proj32, page_table, caches, DB, LS, lam_p, dgain, l)

        xs, proj_s, ret_s, hgrn_s = _trunk_layer(
            l, xs, DB, LS, tab_s, cache_mem_k[l].reshape(DB, N_MEM, MIX_W),
            cache_mem_v[l].reshape(DB, N_MEM, MIX_W), _state_to_bd(state_ret[l]),
            _state_to_bd(state_hgrn[l]), w, attn_s, {"ret": LS, "hgrn": LS, "hgrn_nsub": 1})
        outs_s.append((proj_s, ret_s, hgrn_s))

    def slab(outs, c, nseq, seqlen):
        return jnp.stack([o[0][:, c * MIX_W:(c + 1) * MIX_W].reshape(nseq, seqlen, HEADS, HEAD_W)
                          for o in outs], axis=0)

    def states(outs, i):
        return jnp.stack([_bd_to_state(o[i]) for o in outs], axis=0)

    def memkv(i):
        return jnp.stack([o[i].reshape(B, N_MEM, HEADS, HEAD_W) for o in outs_p], axis=0)

    return (xp.reshape(B, L, D_MODEL), xs.reshape(DB, LS, D_MODEL),
            slab(outs_p, S_SK, B, L), slab(outs_p, S_SV, B, L),
            slab(outs_p, S_DK, B, L), slab(outs_p, S_DV, B, L),
            memkv(3), memkv(4), states(outs_p, 1), states(outs_p, 2),
            slab(outs_s, S_SK, DB, LS), slab(outs_s, S_SV, DB, LS),
            slab(outs_s, S_DK, DB, LS), slab(outs_s, S_DV, DB, LS),
            states(outs_s, 1), states(outs_s, 2))
```

```python
import functools
import math

import jax
import jax.numpy as jnp
import numpy as np
from jax import lax
from jax.experimental import pallas as pl
from jax.experimental.pallas import tpu as pltpu

F32 = jnp.float32
BF16 = jnp.bfloat16

D_MODEL = 1024
N_MEM = 256
HEADS = 4
HEAD_W = 64
MIX_W = HEADS * HEAD_W
DIFF_DH = 32
D_FF = 4 * D_MODEL
N_BRANCH = 4
PAGE_SIZE = 128
ROPE_THETA = 10000.0
NORM_EPS = 1e-6
MASK_VALUE = -1e30
LOG2_E = math.log2(math.e)
LANES = 128
SUBLANES = 8
assert MIX_W == 2 * LANES and 2 * HEAD_W == LANES
N_SLABS = 14
D_IN = N_SLABS * MIX_W
(S_RQ, S_RK, S_RV, S_RG, S_SQ, S_SK, S_SV, S_DQ, S_DK, S_DV,
 S_GQ, S_GF, S_GI, S_GG) = range(N_SLABS)

SB_DEAD_LOG = -104.0

V7X_VMEM_BYTES = 64 * 1024 * 1024
VMEM_LIMIT = (V7X_VMEM_BYTES * 7) // 8

_NT = (((1,), (1,)), ((), ()))
_TN = (((0,), (0,)), ((), ()))


def _params(*sem):
    return pltpu.CompilerParams(dimension_semantics=sem, vmem_limit_bytes=VMEM_LIMIT)


def _dot(a, b):
    return jnp.dot(a, b, preferred_element_type=F32)


def _dot_nt(a, b):
    return lax.dot_general(a, b, _NT, preferred_element_type=F32)


def _dot_tn(a, b):
    return lax.dot_general(a, b, _TN, preferred_element_type=F32)


def _split_dot(x, m16, terms=2, left=False):
    out = None
    r = x
    for _ in range(terms):
        p = r.astype(BF16)
        d = _dot(m16, p) if left else _dot(p, m16)
        out = d if out is None else out + d
        r = r - p.astype(F32)
    return out


def _rms(x, g):
    ms = jnp.mean(x * x, axis=-1, keepdims=True)
    return x * lax.rsqrt(ms + NORM_EPS) * g


def _lane_head(width=MIX_W):
    return lax.broadcasted_iota(jnp.int32, (1, width), 1) // HEAD_W


def _block_diag16():
    r = lax.broadcasted_iota(jnp.int32, (MIX_W, MIX_W), 0) // HEAD_W
    c = lax.broadcasted_iota(jnp.int32, (MIX_W, MIX_W), 1) // HEAD_W
    return jnp.where(r == c, 1.0, 0.0).astype(BF16)


def _head_rms(o, gain, bd16):
    ms = _split_dot(o * o, bd16) * (1.0 / HEAD_W)
    return o * lax.rsqrt(ms + NORM_EPS) * gain


def _log_sigmoid(z):
    return jnp.minimum(z, 0.0) - jnp.log(1.0 + jnp.exp(-jnp.abs(z)))


def _rope_slab(s, cos, sin, half):
    lane = lax.broadcasted_iota(jnp.int32, (1, MIX_W), 1)
    first = (lane % (2 * half)) < half
    partner = jnp.where(first, pltpu.roll(s, MIX_W - half, 1), pltpu.roll(s, half, 1))
    return s * cos + partner * sin


def _norm_proj_kernel(x_ref, g_ref, w_ref, o32_ref):
    h = _rms(x_ref[...], g_ref[...]).astype(BF16)
    o32_ref[...] = _dot(h, w_ref[...])


def _mixer_proj_kernel(x_ref, g_ref, w_ref, tab_ref, o32_ref, o16_ref):
    h = _rms(x_ref[...], g_ref[...]).astype(BF16)
    y = _dot(h, w_ref[...])
    cos_r = tab_ref[:, 0 * MIX_W:1 * MIX_W]
    sin_r = tab_ref[:, 1 * MIX_W:2 * MIX_W]
    cos_d = tab_ref[:, 2 * MIX_W:3 * MIX_W]
    sin_d = tab_ref[:, 3 * MIX_W:4 * MIX_W]
    for c in range(N_SLABS):
        s = y[:, c * MIX_W:(c + 1) * MIX_W]
        if c == S_RQ:
            s = _rope_slab(s, cos_r, sin_r, HEAD_W // 2)
        elif c == S_RK:
            s = _rope_slab(s, cos_r, sin_r, HEAD_W // 2) * (HEAD_W ** -0.5)
        elif c in (S_DQ, S_DK):
            s = _rope_slab(s, cos_d, sin_d, DIFF_DH // 2)
        o32_ref[:, c * MIX_W:(c + 1) * MIX_W] = s
        o16_ref[:, c * MIX_W:(c + 1) * MIX_W] = s.astype(BF16)


def _row_tile(n, want):
    t = math.gcd(n, want)
    assert t % 8 == 0
    return t


def _norm_proj(x, gain, w16):
    n, d = x.shape
    wout = w16.shape[1]
    tm = _row_tile(n, 256)
    return pl.pallas_call(
        _norm_proj_kernel,
        grid=(n // tm,),
        in_specs=[pl.BlockSpec((tm, d), lambda i: (i, 0)),
                  pl.BlockSpec((1, d), lambda i: (0, 0)),
                  pl.BlockSpec((d, wout), lambda i: (0, 0))],
        out_specs=pl.BlockSpec((tm, wout), lambda i: (i, 0)),
        out_shape=jax.ShapeDtypeStruct((n, wout), F32),
        compiler_params=_params("parallel"),
        name="norm_proj",
    )(x, gain, w16)


def _mixer_proj(x, gain, w16, tab):
    n, d = x.shape
    tm = _row_tile(n, 256)
    tab_blocks = tab.shape[0] // tm
    return pl.pallas_call(
        _mixer_proj_kernel,
        grid=(n // tm,),
        in_specs=[pl.BlockSpec((tm, d), lambda i: (i, 0)),
                  pl.BlockSpec((1, d), lambda i: (0, 0)),
                  pl.BlockSpec((d, D_IN), lambda i: (0, 0)),
                  pl.BlockSpec((tm, 4 * MIX_W), lambda i: (i % tab_blocks, 0))],
        out_specs=[pl.BlockSpec((tm, D_IN), lambda i: (i, 0)),
                   pl.BlockSpec((tm, D_IN), lambda i: (i, 0))],
        out_shape=[jax.ShapeDtypeStruct((n, D_IN), F32),
                   jax.ShapeDtypeStruct((n, D_IN), BF16)],
        compiler_params=_params("parallel"),
        name="mixer_proj",
    )(x, gain, w16, tab)


def _rope_table(pos):
    def one(group):
        half = group // 2
        inv = ROPE_THETA ** (-jnp.arange(half, dtype=F32) * 2.0 / group)
        ang = pos.astype(F32)[:, None] * inv[None, :]
        cos, sin = jnp.cos(ang), jnp.sin(ang)
        reps = MIX_W // group
        return (jnp.tile(jnp.concatenate([cos, cos], -1), (1, reps)),
                jnp.tile(jnp.concatenate([-sin, sin], -1), (1, reps)))
    cr, sr = one(HEAD_W)
    cd, sd = one(DIFF_DH)
    return jnp.concatenate([cr, sr, cd, sd], axis=-1)


_RET_LOG_GAMMA = [float(np.log(1.0 - 2.0 ** (-5.0 - h))) for h in range(HEADS)]


def _state_step(st, q_dec, k_dec, v, decay_row):
    inter = _dot_nt(q_dec.astype(BF16), st.astype(BF16))
    upd = _dot_tn(v.astype(BF16), k_dec.astype(BF16))
    r = lax.broadcasted_iota(jnp.int32, (MIX_W, MIX_W), 0) // HEAD_W
    c = lax.broadcasted_iota(jnp.int32, (MIX_W, MIX_W), 1) // HEAD_W
    return inter, st * decay_row + jnp.where(r == c, upd, 0.0)


def _retention_kernel(q_ref, k_ref, v_ref, g_ref, gain_ref, st0_ref, o_ref, st_ref, *, chunk):
    @pl.when(pl.program_id(1) == 0)
    def _():
        st_ref[...] = st0_ref[...]

    q, k, v = q_ref[...], k_ref[...], v_ref[...]
    head = _lane_head()
    lg = jnp.zeros((1, MIX_W), F32)
    for h in range(HEADS):
        lg = jnp.where(head == h, _RET_LOG_GAMMA[h], lg)
    steps = lax.broadcasted_iota(jnp.int32, (chunk, 1), 0).astype(F32) + 1.0
    b = steps * lg
    b_last = float(chunk) * lg
    o, st_new = _state_step(st_ref[0], q * jnp.exp(b), k * jnp.exp(b_last - b), v,
                            jnp.exp(b_last))
    st_ref[0] = st_new

    ri = lax.broadcasted_iota(jnp.int32, (chunk, chunk), 0)
    ci = lax.broadcasted_iota(jnp.int32, (chunk, chunk), 1)
    causal = ri >= ci
    dist = (ri - ci).astype(F32)
    k16, v16 = k.astype(BF16), v.astype(BF16)
    for h in range(HEADS):
        sc = _dot_nt(jnp.where(head == h, q, 0.0).astype(BF16), k16)
        dec = jnp.where(causal, jnp.exp(dist * _RET_LOG_GAMMA[h]), 0.0)
        o = o + jnp.where(head == h, _dot((sc * dec).astype(BF16), v16), 0.0)

    g = g_ref[...]
    o_ref[...] = _head_rms(o, gain_ref[...], _block_diag16()) * (g * jax.nn.sigmoid(g))


def _hgrn_kernel(q_ref, f_ref, v_ref, g_ref, gain_ref, lbl_ref, st0_ref, o_ref, st_ref,
                 kk_scr, b_scr, *, chunk, nsub, layer):
    @pl.when(pl.program_id(1) == 0)
    def _():
        st_ref[...] = st0_ref[...]

    lbl = lbl_ref[...]
    e = jnp.exp(lbl - jnp.max(lbl, axis=0, keepdims=True))
    lb_w = e / jnp.sum(e, axis=0, keepdims=True)
    lb = lb_w[0:1, :]
    for i in range(1, layer + 1):
        lb = lb + lb_w[i:i + 1, :]
    lb = lb - lb_w[0:1, :]

    z_all = f_ref[...]
    kk_scr[...] = (1.0 - lb) * jax.nn.sigmoid(-z_all)
    log2_f = jnp.log(lb + (1.0 - lb) * jax.nn.sigmoid(z_all)) * LOG2_E
    ri = lax.broadcasted_iota(jnp.int32, (chunk, chunk), 0)
    ci = lax.broadcasted_iota(jnp.int32, (chunk, chunk), 1)
    tril16 = jnp.where(ri >= ci, 1.0, 0.0).astype(BF16)
    bd16 = _block_diag16()
    rows8 = lax.broadcasted_iota(jnp.int32, (SUBLANES, 1), 0)
    groups = chunk // SUBLANES
    r_blk = lax.broadcasted_iota(jnp.int32, (MIX_W, MIX_W), 0) // HEAD_W
    c_blk = lax.broadcasted_iota(jnp.int32, (MIX_W, MIX_W), 1) // HEAD_W

    bs, q_decs, updates, decays = [], [], [], []
    for s in range(nsub):
        lo = s * chunk
        q, v, k = q_ref[lo:lo + chunk, :], v_ref[lo:lo + chunk, :], kk_scr[lo:lo + chunk, :]
        lf = log2_f[lo:lo + chunk, :]
        b = _split_dot(lf, tril16, terms=3, left=True) if chunk >= 16 else _small_cumsum(lf, chunk)
        b_scr[lo:lo + chunk, :] = b
        b_last = b[chunk - 1:chunk, :]
        bs.append(b)
        q_decs.append((q * jnp.exp2(b)).astype(BF16))
        upd = _dot_tn(v.astype(BF16), (k * jnp.exp2(b_last - b)).astype(BF16))
        updates.append(jnp.where(r_blk == c_blk, upd, 0.0))
        decays.append(jnp.exp2(b_last))
    states = [st_ref[0]]
    for s in range(nsub):
        states.append(states[-1] * decays[s] + updates[s])
    st_ref[0] = states[-1]

    for s in range(nsub):
        lo = s * chunk
        q, b = q_ref[lo:lo + chunk, :], bs[s]
        inter = _dot_nt(q_decs[s], states[s].astype(BF16))
        qg = [q[r * SUBLANES:(r + 1) * SUBLANES, :] for r in range(groups)]
        bg = [b[r * SUBLANES:(r + 1) * SUBLANES, :] for r in range(groups)]
        slabs, where = [], []
        for j in range(chunk):
            kj = kk_scr[lo + j:lo + j + 1, :]
            bj = b_scr[lo + j:lo + j + 1, :]
            for r in range(j // SUBLANES, groups):
                first = r * SUBLANES
                t = qg[r] * kj * jnp.exp2(bg[r] - bj)
                slabs.append(jnp.where(rows8 + first >= j, t, 0.0) if first < j else t)
                where.append((j, r))
        pair_sum = _dot(jnp.concatenate(slabs, axis=0).astype(BF16), bd16)
        og = [inter[r * SUBLANES:(r + 1) * SUBLANES, :] for r in range(groups)]
        for i, (j, r) in enumerate(where):
            og[r] = og[r] + pair_sum[i * SUBLANES:(i + 1) * SUBLANES, :] * v_ref[lo + j:lo + j + 1, :]
        o = jnp.concatenate(og, axis=0) if groups > 1 else og[0]
        g = g_ref[lo:lo + chunk, :]
        o_ref[lo:lo + chunk, :] = _head_rms(o, gain_ref[...], bd16) * (g * jax.nn.sigmoid(g))


def _small_cumsum(x, rows):
    idx = lax.broadcasted_iota(jnp.int32, (rows, 1), 0)
    out = jnp.zeros_like(x)
    for j in range(rows):
        out = out + jnp.where(idx >= j, x[j:j + 1, :], 0.0)
    return out


def _slab_spec(rows, slab, nchunk):
    return pl.BlockSpec((rows, MIX_W), lambda s, c: (s * nchunk + c, slab))


def _recurrence(kind, proj32, nseq, seqlen, chunk, gain, st0, lb_logits=None, layer=0, nsub=1):
    rows = chunk * nsub
    nstep = seqlen // rows
    n = nseq * seqlen
    row = pl.BlockSpec((1, MIX_W), lambda s, c: (0, 0))
    st_spec = pl.BlockSpec((1, MIX_W, MIX_W), lambda s, c: (s, 0, 0))
    out_specs = [pl.BlockSpec((rows, MIX_W), lambda s, c: (s * nstep + c, 0)), st_spec]
    out_shape = [jax.ShapeDtypeStruct((n, MIX_W), F32),
                 jax.ShapeDtypeStruct((nseq, MIX_W, MIX_W), F32)]
    if kind == "ret":
        assert nsub == 1
        slabs = (S_RQ, S_RK, S_RV, S_RG)
        body = functools.partial(_retention_kernel, chunk=chunk)
        extra_specs, extra, scratch = [row, st_spec], (gain, st0), []
    else:
        slabs = (S_GQ, S_GF, S_GI, S_GG)
        body = functools.partial(_hgrn_kernel, chunk=chunk, nsub=nsub, layer=layer)
        extra_specs = [row, pl.BlockSpec(lb_logits.shape, lambda s, c: (0, 0)), st_spec]
        extra = (gain, lb_logits, st0)
        scratch = [pltpu.VMEM((rows, MIX_W), F32), pltpu.VMEM((rows, MIX_W), F32)]
    return pl.pallas_call(
        body,
        grid=(nseq, nstep),
        in_specs=[_slab_spec(rows, s, nstep) for s in slabs] + extra_specs,
        out_specs=out_specs,
        out_shape=out_shape,
        scratch_shapes=scratch,
        compiler_params=_params("parallel", "arbitrary"),
        name="recurrence_" + kind,
    )(proj32, proj32, proj32, proj32, *extra)


def _state_to_bd(s):
    b = s.shape[0]
    st = jnp.swapaxes(s, 2, 3)
    eye = jnp.eye(HEADS, dtype=s.dtype)
    bd = st[:, :, :, None, :] * eye[None, :, None, :, None]
    return bd.reshape(b, MIX_W, MIX_W)


def _bd_to_state(bd):
    b = bd.shape[0]
    r = bd.reshape(b, HEADS, HEAD_W, HEADS, HEAD_W)
    st = jnp.stack([r[:, h, :, h, :] for h in range(HEADS)], axis=1)
    return jnp.swapaxes(st, 2, 3)


def _sb_prompt_kernel(q_ref, k_ref, v_ref, o_ref, acc_ref, *, tq, tk):
    qi = pl.program_id(1)
    head = _lane_head()
    q = q_ref[...].astype(F32) * (HEAD_W ** -0.5)
    qh = [jnp.where(head == h, q, 0.0).astype(BF16) for h in range(HEADS)]
    ri = lax.broadcasted_iota(jnp.int32, (tk, tk), 0)
    ci = lax.broadcasted_iota(jnp.int32, (tk, tk), 1)
    later16 = jnp.where(ri > ci, 1.0, 0.0).astype(BF16)
    qpos = qi * tq + lax.broadcasted_iota(jnp.int32, (tq, 1), 0)
    acc_ref[...] = jnp.zeros_like(acc_ref)

    def cond(state):
        return jnp.logical_and(state[0] >= 0, state[1] > SB_DEAD_LOG)

    def body(state):
        kb = state[0]
        carries = state[2:]
        start = pl.multiple_of(kb * tk, tk)
        kblk = k_ref[pl.ds(start, tk), :]
        vblk = v_ref[pl.ds(start, tk), :]
        mask = (start + lax.broadcasted_iota(jnp.int32, (1, tk), 1)) < qpos
        new_carries = []
        alive = jnp.float32(-jnp.inf)
        for h in range(HEADS):
            z = _dot_nt(qh[h], kblk)
            ls = _log_sigmoid(z)
            u = jnp.where(mask, ls - z, 0.0)
            between = carries[h] + _split_dot(u, later16)
            w = jnp.where(mask, jnp.exp(ls + between), 0.0)
            acc_ref[...] += jnp.where(head == h, _dot(w.astype(BF16), vblk), 0.0)
            c_new = carries[h] + jnp.sum(u, axis=1, keepdims=True)
            new_carries.append(c_new)
            alive = jnp.maximum(alive, jnp.max(c_new))
        return (kb - 1, alive) + tuple(new_carries)

    zero = jnp.zeros((tq, 1), F32)
    lax.while_loop(cond, body, (qi * tq // tk, jnp.float32(0.0), zero, zero, zero, zero))
    o_ref[...] = acc_ref[...]


def _sb_prompt(proj32, proj16, nseq, seqlen):
    tq = tk = 256
    nq = seqlen // tq
    return pl.pallas_call(
        functools.partial(_sb_prompt_kernel, tq=tq, tk=tk),
        grid=(nseq, nq),
        in_specs=[pl.BlockSpec((tq, MIX_W), lambda b, i: (b * nq + i, S_SQ)),
                  pl.BlockSpec((seqlen, MIX_W), lambda b, i: (b, S_SK)),
                  pl.BlockSpec((seqlen, MIX_W), lambda b, i: (b, S_SV))],
        out_specs=pl.BlockSpec((tq, MIX_W), lambda b, i: (b * nq + i, 0)),
        out_shape=jax.ShapeDtypeStruct((nseq * seqlen, MIX_W), F32),
        scratch_shapes=[pltpu.VMEM((tq, MIX_W), F32)],
        compiler_params=_params("parallel", "arbitrary"),
        name="sb_prompt",
    )(proj32, proj16, proj16)


def _diff_lambda(lam_ref, layer):
    lp = lam_ref[...]
    lam_init = 0.8 - 0.6 * math.exp(-0.3 * layer)
    lam = (jnp.exp(jnp.sum(lp[0:1, :] * lp[1:2, :], axis=1, keepdims=True))
           - jnp.exp(jnp.sum(lp[2:3, :] * lp[3:4, :], axis=1, keepdims=True)) + lam_init)
    return lam, lam_init


def _combo_masks():
    lane = lax.broadcasted_iota(jnp.int32, (1, MIX_W), 1)
    return [(lane // DIFF_DH) == (2 * (c // 2) + (c % 2)) for c in range(2 * HEADS)]


def _diff_prompt_kernel(q_ref, k_ref, v_ref, lam_ref, gain_ref, o_ref, qs_ref, m_ref, acc_ref,
                        *, tq, tk, layer):
    qi = pl.program_id(1)
    head = _lane_head()
    q = q_ref[...] * (DIFF_DH ** -0.5 * LOG2_E)
    for c, mk in enumerate(_combo_masks()):
        qs_ref[c * tq:(c + 1) * tq, :] = jnp.where(mk, q, 0.0).astype(BF16)
    m_ref[...] = jnp.full(m_ref.shape, MASK_VALUE, F32)
    acc_ref[...] = jnp.zeros_like(acc_ref)
    qpos = qi * tq + lax.broadcasted_iota(jnp.int32, (2 * tq, 1), 0) % tq
    reps = tk // LANES

    def block(kb, masked):
        start = pl.multiple_of(kb * tk, tk)
        kblk = k_ref[pl.ds(start, tk), :]
        vblk = v_ref[pl.ds(start, tk), :]
        if masked:
            mask = (start + lax.broadcasted_iota(jnp.int32, (1, tk), 1)) <= qpos
        for h in range(HEADS):
            rows = slice(2 * h * tq, (2 * h + 2) * tq)
            s = _dot_nt(qs_ref[rows, :], kblk)
            if masked:
                s = jnp.where(mask, s, MASK_VALUE)
            m_old = m_ref[rows, :]
            m_new = jnp.maximum(m_old, jnp.max(s, axis=1, keepdims=True))
            m_ref[rows, :] = m_new
            p = jnp.exp2(s - jnp.concatenate([m_new] * reps, axis=1)).astype(BF16)
            vaug = jnp.where(head == h, vblk, jnp.ones_like(vblk))
            alpha = jnp.exp2(m_old - m_new)
            acc_ref[rows, :] = (jnp.concatenate([alpha] * (MIX_W // LANES), axis=1) * acc_ref[rows, :]
                                + _dot(p, vaug))

    def unmasked(kb, carry):
        block(kb, False)
        return carry

    diag = qi * tq // tk
    lax.fori_loop(0, diag, unmasked, 0)
    block(diag, True)

    lam, lam_init = _diff_lambda(lam_ref, layer)
    halves = []
    for half in range(MIX_W // LANES):
        mine = slice(half * LANES, (half + 1) * LANES)
        other = slice((1 - half) * LANES, (2 - half) * LANES)
        o = jnp.zeros((tq, LANES), F32)
        for h in range(half * 2, half * 2 + 2):
            r1 = slice(2 * h * tq, (2 * h + 1) * tq)
            r2 = slice((2 * h + 1) * tq, (2 * h + 2) * tq)
            o1 = acc_ref[r1, mine] / acc_ref[r1, other]
            o2 = acc_ref[r2, mine] / acc_ref[r2, other]
            o = jnp.where(head[:, mine] == h, o1 - lam * o2, o)
        halves.append(o)
    o = jnp.concatenate(halves, axis=1)
    o_ref[...] = _head_rms(o, gain_ref[...], _block_diag16()) * (1.0 - lam_init)


def _diff_prompt(proj32, proj16, nseq, seqlen, lam_p, gain, layer):
    tq = tk = 512
    nq = seqlen // tq
    return pl.pallas_call(
        functools.partial(_diff_prompt_kernel, tq=tq, tk=tk, layer=layer),
        grid=(nseq, nq),
        in_specs=[pl.BlockSpec((tq, MIX_W), lambda b, i: (b * nq + i, S_DQ)),
                  pl.BlockSpec((seqlen, MIX_W), lambda b, i: (b, S_DK)),
                  pl.BlockSpec((seqlen, MIX_W), lambda b, i: (b, S_DV)),
                  pl.BlockSpec(lam_p.shape, lambda b, i: (0, 0)),
                  pl.BlockSpec((1, MIX_W), lambda b, i: (0, 0))],
        out_specs=pl.BlockSpec((tq, MIX_W), lambda b, i: (b * nq + i, 0)),
        out_shape=jax.ShapeDtypeStruct((nseq * seqlen, MIX_W), F32),
        scratch_shapes=[pltpu.VMEM((2 * HEADS * tq, MIX_W), BF16),
                        pltpu.VMEM((2 * HEADS * tq, LANES), F32),
                        pltpu.VMEM((2 * HEADS * tq, MIX_W), F32)],
        compiler_params=_params("parallel", "arbitrary"),
        name="diff_prompt",
    )(proj32, proj16, proj16, lam_p, gain)


def _stack_rows(x, masks):
    return jnp.concatenate([jnp.where(mk, x, 0.0) for mk in masks], axis=0)


def _sample_attn_kernel(pt_ref, q_ref, sk_ref, sv_ref, dq_ref, dk_ref, dv_ref, *rest,
                        ls, n_pages, layer):
    del pt_ref
    pages, (lam_ref, gain_ref, osb_ref, odf_ref, sb_acc, sb_carry) = rest[:4 * n_pages], rest[4 * n_pages:]
    psk, psv, pdk, pdv = (pages[i * n_pages:(i + 1) * n_pages] for i in range(4))
    head = _lane_head()
    sb_masks = [head == h for h in range(HEADS)]
    qs = _stack_rows(q_ref[...] * (HEAD_W ** -0.5), sb_masks).astype(BF16)
    qd = _stack_rows(dq_ref[...] * (DIFF_DH ** -0.5), _combo_masks()).astype(BF16)

    def pad_page(x):
        return jnp.concatenate([x, jnp.zeros((PAGE_SIZE - ls, MIX_W), F32)], axis=0).astype(BF16)

    def later16(n):
        ri = lax.broadcasted_iota(jnp.int32, (n, n), 0)
        ci = lax.broadcasted_iota(jnp.int32, (n, n), 1)
        return jnp.where(ri > ci, 1.0, 0.0).astype(BF16)

    def sb_block(z, pv_fn, mask):
        lsg = _log_sigmoid(z)
        u = lsg - z
        if mask is not None:
            u = jnp.where(mask, u, 0.0)
        between = sb_carry[...] + _split_dot(u, later16(z.shape[1]))
        w = jnp.exp(lsg + between)
        if mask is not None:
            w = jnp.where(mask, w, 0.0)
        sb_acc[...] += pv_fn(w.astype(BF16))
        sb_carry[...] += jnp.sum(u, axis=1, keepdims=True)

    sb_acc[...] = jnp.zeros_like(sb_acc)
    sb_carry[...] = jnp.zeros_like(sb_carry)
    key = lax.broadcasted_iota(jnp.int32, (1, PAGE_SIZE), 1)
    t_sb = lax.broadcasted_iota(jnp.int32, (HEADS * ls, 1), 0) % ls
    sv_new = pad_page(sv_ref[...])
    sb_block(_dot_nt(qs, pad_page(sk_ref[...])), lambda w: _dot(w, sv_new), key < t_sb)

    def sb_past(blk):
        hi = n_pages - 1 - 2 * blk
        kt = jnp.concatenate([psk[hi - 1][...], psk[hi][...]], axis=1).astype(BF16)
        vt = jnp.concatenate([psv[hi - 1][...], psv[hi][...]], axis=1).astype(BF16)
        sb_block(_dot(qs, kt), lambda w: _dot_nt(w, vt), None)

    def sb_rest(blk):
        if blk < n_pages // 2:
            @pl.when(jnp.max(sb_carry[...]) > SB_DEAD_LOG)
            def _():
                sb_past(blk)
                sb_rest(blk + 1)

    sb_past(0)

    t_df = lax.broadcasted_iota(jnp.int32, (2 * HEADS * ls, 1), 0) % ls
    s_new = jnp.where(key <= t_df, _dot_nt(qd, pad_page(dk_ref[...])), MASK_VALUE)
    kt_all = jnp.concatenate([r[...].astype(BF16) for r in pdk], axis=1)
    s_past = _dot(qd, kt_all)
    m = jnp.maximum(jnp.max(s_new, axis=1, keepdims=True), jnp.max(s_past, axis=1, keepdims=True))
    p_new = jnp.exp(s_new - m)
    p_past = jnp.exp(s_past - m)
    l = jnp.sum(p_new, axis=1, keepdims=True) + jnp.sum(p_past, axis=1, keepdims=True)
    vt_all = jnp.concatenate([r[...].astype(BF16) for r in pdv], axis=1)
    on = (_dot(p_new.astype(BF16), pad_page(dv_ref[...]))
          + _dot_nt(p_past.astype(BF16), vt_all)) / l
    lam, lam_init = _diff_lambda(lam_ref, layer)
    o = jnp.zeros((ls, MIX_W), F32)
    for h in range(HEADS):
        o1 = on[(2 * h) * ls:(2 * h + 1) * ls, :]
        o2 = on[(2 * h + 1) * ls:(2 * h + 2) * ls, :]
        o = jnp.where(head == h, o1 - lam * o2, o)
    odf_ref[...] = _head_rms(o, gain_ref[...], _block_diag16()) * (1.0 - lam_init)

    sb_rest(1)
    acc = sb_acc[...]
    o = jnp.zeros((ls, MIX_W), F32)
    for h in range(HEADS):
        o = jnp.where(head == h, acc[h * ls:(h + 1) * ls, :], o)
    osb_ref[...] = o


def _sample_attn(proj32, page_table, caches, nseq, ls, lam_p, gain, layer):
    n_pages = page_table.shape[1]
    assert n_pages % 2 == 0

    def slab(c):
        return pl.BlockSpec((ls, MIX_W), lambda b, pt: (b, c))

    def page(p):
        return pl.BlockSpec((None, None, MIX_W, PAGE_SIZE), lambda b, pt: (layer, pt[b, p], 0, 0))

    out = pl.BlockSpec((ls, MIX_W), lambda b, pt: (b, 0))
    grid_spec = pltpu.PrefetchScalarGridSpec(
        num_scalar_prefetch=1,
        grid=(nseq,),
        in_specs=[slab(S_SQ), slab(S_SK), slab(S_SV), slab(S_DQ), slab(S_DK), slab(S_DV)]
                 + [page(p) for _ in range(4) for p in range(n_pages)]
                 + [pl.BlockSpec(lam_p.shape, lambda b, pt: (0, 0)),
                    pl.BlockSpec((1, MIX_W), lambda b, pt: (0, 0))],
        out_specs=[out, out],
        scratch_shapes=[pltpu.VMEM((HEADS * ls, MIX_W), F32), pltpu.VMEM((HEADS * ls, 1), F32)])
    page_args = [c for c in caches for _ in range(n_pages)]
    return pl.pallas_call(
        functools.partial(_sample_attn_kernel, ls=ls, n_pages=n_pages, layer=layer),
        grid_spec=grid_spec,
        out_shape=[jax.ShapeDtypeStruct((nseq * ls, MIX_W), F32)] * 2,
        compiler_params=_params("arbitrary"),
        name="sample_attn",
    )(page_table, *([proj32] * 6), *page_args, lam_p, gain)


def _merge_kernel(x_ref, b0_ref, b1_ref, b2_ref, b3_ref, pre_ref, wg_ref, wbr_ref, wo_ref,
                  post_ref, o_ref):
    x = x_ref[...]
    h = _rms(x, pre_ref[...]).astype(BF16)
    merged = None
    for i, b_ref in enumerate((b0_ref, b1_ref, b2_ref, b3_ref)):
        gate = jax.nn.sigmoid(_dot(h, wg_ref[:, i * D_MODEL:(i + 1) * D_MODEL]))
        term = gate * _dot(b_ref[...].astype(BF16), wbr_ref[i])
        merged = term if merged is None else merged + term
    o_ref[...] = x + _rms(_dot(merged.astype(BF16), wo_ref[...]), post_ref[...])


def _const_spec(shape):
    nd = len(shape)
    return pl.BlockSpec(shape, lambda i: (0,) * nd, pipeline_mode=pl.Buffered(1))


def _merge(x, branches, pre, wg16, wbr16, wo16, post):
    n = x.shape[0]
    tm = _row_tile(n, 256)
    tok = pl.BlockSpec((tm, D_MODEL), lambda i: (i, 0))
    br = pl.BlockSpec((tm, MIX_W), lambda i: (i, 0))
    return pl.pallas_call(
        _merge_kernel,
        grid=(n // tm,),
        in_specs=[tok, br, br, br, br, _const_spec((1, D_MODEL)), _const_spec(wg16.shape),
                  _const_spec(wbr16.shape), _const_spec(wo16.shape), _const_spec((1, D_MODEL))],
        out_specs=tok,
        out_shape=jax.ShapeDtypeStruct((n, D_MODEL), F32),
        compiler_params=_params("parallel"),
        name="merge",
    )(x, *branches, pre, wg16, wbr16, wo16, post)


def _cross_attn_kernel(q_ref, mk_ref, mv_ref, o_ref, *, channel_major):
    head = _lane_head()
    q = q_ref[...] * (HEAD_W ** -0.5)
    mk = mk_ref[...].astype(BF16)
    mv = mv_ref[...].astype(BF16)
    o = jnp.zeros(q.shape, F32)
    for h in range(HEADS):
        qh = jnp.where(head == h, q, 0.0).astype(BF16)
        s = _dot(qh, mk) if channel_major else _dot_nt(qh, mk)
        p = jnp.exp(s - jnp.max(s, axis=1, keepdims=True))
        p16 = p.astype(BF16)
        pv = _dot_nt(p16, mv) if channel_major else _dot(p16, mv)
        o = jnp.where(head == h, pv / jnp.sum(p, axis=1, keepdims=True), o)
    o_ref[...] = o


def _cross_attn(cq, mk, mv, nseq, seqlen, channel_major):
    tm = _row_tile(seqlen, 512)
    nt = seqlen // tm
    mem = pl.BlockSpec((None,) + mk.shape[1:], lambda b, i: (b, 0, 0))
    tok = pl.BlockSpec((tm, MIX_W), lambda b, i: (b * nt + i, 0))
    return pl.pallas_call(
        functools.partial(_cross_attn_kernel, channel_major=channel_major),
        grid=(nseq, nt),
        in_specs=[tok, mem, mem],
        out_specs=tok,
        out_shape=jax.ShapeDtypeStruct(cq.shape, F32),
        compiler_params=_params("parallel", "parallel"),
        name="cross_attn",
    )(cq, mk, mv)


def _post_kernel(x_ref, co_ref, wco_ref, cpost_ref, mpre_ref, wup_ref, wdn_ref, mpost_ref, o_ref):
    x = x_ref[...] + _rms(_dot(co_ref[...].astype(BF16), wco_ref[...]), cpost_ref[...])
    h = _rms(x, mpre_ref[...]).astype(BF16)
    u = jnp.square(jnp.maximum(_dot(h, wup_ref[...]), 0.0))
    o_ref[...] = x + _rms(_dot(u.astype(BF16), wdn_ref[...]), mpost_ref[...])


def _post(x, co, wco16, cpost, mpre, wup16, wdn16, mpost):
    n = x.shape[0]
    tm = _row_tile(n, 256)
    tok = pl.BlockSpec((tm, D_MODEL), lambda i: (i, 0))
    return pl.pallas_call(
        _post_kernel,
        grid=(n // tm,),
        in_specs=[tok, pl.BlockSpec((tm, MIX_W), lambda i: (i, 0)), _const_spec(wco16.shape),
                  _const_spec((1, D_MODEL)), _const_spec((1, D_MODEL)), _const_spec(wup16.shape),
                  _const_spec(wdn16.shape), _const_spec((1, D_MODEL))],
        out_specs=tok,
        out_shape=jax.ShapeDtypeStruct((n, D_MODEL), F32),
        compiler_params=_params("parallel"),
        name="ca_out_mlp",
    )(x, co, wco16, cpost, mpre, wup16, wdn16, mpost)


def _trunk_layer(layer, x, nseq, seqlen, rope_tab, mem_k, mem_v, mem_channel_major,
                 st_ret0, st_hgrn0, w, attn_fn, rec_chunk):
    proj32, proj16 = _mixer_proj(x, w["mix_pre_norm"], w["w_in"], rope_tab)
    o_ret, st_ret = _recurrence("ret", proj32, nseq, seqlen, rec_chunk["ret"], w["ret_norm"], st_ret0)
    o_hgrn, st_hgrn = _recurrence("hgrn", proj32, nseq, seqlen, rec_chunk["hgrn"], w["hgrn_norm"],
                                  st_hgrn0, w["hgrn_lb_logits"], layer, rec_chunk["hgrn_nsub"])
    o_sb, o_diff = attn_fn(proj32, proj16)
    x = _merge(x, (o_ret, o_sb, o_diff, o_hgrn), w["mix_pre_norm"], w["w_gate"], w["w_br"],
               w["w_out"], w["mix_post_norm"])
    cq = _norm_proj(x, w["ca_pre_norm"], w["w_ca_q"])
    co = _cross_attn(cq, mem_k, mem_v, nseq, seqlen, mem_channel_major)
    x = _post(x, co, w["w_ca_o"], w["ca_post_norm"], w["mlp_pre_norm"], w["w_mlp_up"],
              w["w_mlp_down"], w["mlp_post_norm"])
    return x, proj32, st_ret, st_hgrn


def kernel(x_prompt, x_sample, mem_prompt, cache_sb_k, cache_sb_v, cache_diff_k, cache_diff_v, cache_mem_k, cache_mem_v, state_ret, state_hgrn, page_table, mix_pre_norm, mix_post_norm, ca_pre_norm, ca_post_norm, mlp_pre_norm, mlp_post_norm, mem_norm, w_in, w_gate, ret_norm, diff_lambda, diff_norm, hgrn_lb_logits, hgrn_norm, w_br_ret, w_br_sb, w_br_diff, w_br_hgrn, w_out, w_ca_q, w_ca_k, w_ca_v, w_ca_o, w_mlp_up, w_mlp_down):
    depth = w_in.shape[0]
    B, L, _ = x_prompt.shape
    DB, LS, _ = x_sample.shape
    n_pages = page_table.shape[1]
    past_len = n_pages * PAGE_SIZE

    tab_p = _rope_table(jnp.arange(L, dtype=jnp.int32))
    tab_s = jnp.tile(_rope_table(past_len + jnp.arange(LS, dtype=jnp.int32)), (256 // LS, 1))
    caches = [jnp.transpose(c, (0, 1, 3, 4, 2)).reshape(c.shape[0], c.shape[1], MIX_W, PAGE_SIZE)
              for c in (cache_sb_k, cache_sb_v, cache_diff_k, cache_diff_v)]
    mem_cache = [jnp.transpose(c, (0, 1, 3, 4, 2)).reshape(c.shape[0], c.shape[1], MIX_W, N_MEM)
                 for c in (cache_mem_k, cache_mem_v)]
    w_br = jnp.stack([w_br_ret, w_br_sb, w_br_diff, w_br_hgrn], axis=1).astype(BF16)
    w_ca_kv = jnp.concatenate([w_ca_k, w_ca_v], axis=-1).astype(BF16)
    diff_gain = jnp.tile(diff_norm, (1, HEADS))

    xp = x_prompt.reshape(B * L, D_MODEL)
    xs = x_sample.reshape(DB * LS, D_MODEL)
    mem = mem_prompt.reshape(B * N_MEM, D_MODEL)
    zero_state = jnp.zeros((B, MIX_W, MIX_W), F32)
    outs_p, outs_s = [], []
    for l in range(depth):
        row = lambda a: a[l][None, :]
        w = {
            "mix_pre_norm": row(mix_pre_norm), "mix_post_norm": row(mix_post_norm),
            "ca_pre_norm": row(ca_pre_norm), "ca_post_norm": row(ca_post_norm),
            "mlp_pre_norm": row(mlp_pre_norm), "mlp_post_norm": row(mlp_post_norm),
            "ret_norm": row(ret_norm), "hgrn_norm": row(hgrn_norm),
            "hgrn_lb_logits": hgrn_lb_logits,
            "w_in": w_in[l].astype(BF16), "w_gate": w_gate[l].astype(BF16), "w_br": w_br[l],
            "w_out": w_out[l].astype(BF16), "w_ca_q": w_ca_q[l].astype(BF16),
            "w_ca_o": w_ca_o[l].astype(BF16), "w_mlp_up": w_mlp_up[l].astype(BF16),
            "w_mlp_down": w_mlp_down[l].astype(BF16),
        }
        lam_p, dgain = diff_lambda[l], diff_gain[l][None, :]

        mkv = _norm_proj(mem, row(mem_norm), w_ca_kv[l])
        mk_p = mkv[:, :MIX_W].reshape(B, N_MEM, MIX_W)
        mv_p = mkv[:, MIX_W:].reshape(B, N_MEM, MIX_W)

        def attn_p(proj32, proj16, l=l, lam_p=lam_p, dgain=dgain):
            return (_sb_prompt(proj32, proj16, B, L),
                    _diff_prompt(proj32, proj16, B, L, lam_p, dgain, l))

        xp, proj_p, ret_p, hgrn_p = _trunk_layer(
            l, xp, B, L, tab_p, mk_p, mv_p, False, zero_state, zero_state, w, attn_p,
            {"ret": 256, "hgrn": 32, "hgrn_nsub": 8})
        outs_p.append((proj_p, ret_p, hgrn_p, mk_p, mv_p))

        def attn_s(proj32, proj16, l=l, lam_p=lam_p, dgain=dgain):
            del proj16
            return _sample_attn(proj32, page_table, caches, DB, LS, lam_p, dgain, l)

        xs, proj_s, ret_s, hgrn_s = _trunk_layer(
            l, xs, DB, LS, tab_s, mem_cache[0][l], mem_cache[1][l], True, _state_to_bd(state_ret[l]),
            _state_to_bd(state_hgrn[l]), w, attn_s, {"ret": LS, "hgrn": LS, "hgrn_nsub": 1})
        outs_s.append((proj_s, ret_s, hgrn_s))

    def slab(outs, c, nseq, seqlen):
        return jnp.stack([o[0][:, c * MIX_W:(c + 1) * MIX_W].reshape(nseq, seqlen, HEADS, HEAD_W)
                          for o in outs], axis=0)

    def states(outs, i):
        return jnp.stack([_bd_to_state(o[i]) for o in outs], axis=0)

    def memkv(i):
        return jnp.stack([o[i].reshape(B, N_MEM, HEADS, HEAD_W) for o in outs_p], axis=0)

    return (xp.reshape(B, L, D_MODEL), xs.reshape(DB, LS, D_MODEL),
            slab(outs_p, S_SK, B, L), slab(outs_p, S_SV, B, L),
            slab(outs_p, S_DK, B, L), slab(outs_p, S_DV, B, L),
            memkv(3), memkv(4), states(outs_p, 1), states(outs_p, 2),
            slab(outs_s, S_SK, DB, LS), slab(outs_s, S_SV, DB, LS),
            slab(outs_s, S_DK, DB, LS), slab(outs_s, S_DV, DB, LS),
            states(outs_s, 1), states(outs_s, 2))
```

```python
import functools
import math

import jax
import jax.numpy as jnp
import numpy as np
from jax import lax
from jax.experimental import pallas as pl
from jax.experimental.pallas import tpu as pltpu

F32 = jnp.float32
BF16 = jnp.bfloat16

D_MODEL = 1024
N_MEM = 256
HEADS = 4
HEAD_W = 64
MIX_W = HEADS * HEAD_W
DIFF_DH = 32
D_FF = 4 * D_MODEL
N_BRANCH = 4
PAGE_SIZE = 128
ROPE_THETA = 10000.0
NORM_EPS = 1e-6
MASK_VALUE = -1e30
LOG2_E = math.log2(math.e)
LANES = 128
SUBLANES = 8
assert MIX_W == 2 * LANES and 2 * HEAD_W == LANES
N_SLABS = 14
D_IN = N_SLABS * MIX_W
(S_RQ, S_RK, S_RV, S_RG, S_SQ, S_SK, S_SV, S_DQ, S_DK, S_DV,
 S_GQ, S_GF, S_GI, S_GG) = range(N_SLABS)

SB_DEAD_LOG = -104.0

V7X_VMEM_BYTES = 64 * 1024 * 1024
VMEM_LIMIT = (V7X_VMEM_BYTES * 7) // 8

_NT = (((1,), (1,)), ((), ()))
_TN = (((0,), (0,)), ((), ()))


def _params(*sem):
    return pltpu.CompilerParams(dimension_semantics=sem, vmem_limit_bytes=VMEM_LIMIT)


def _dot(a, b):
    return jnp.dot(a, b, preferred_element_type=F32)


def _dot_nt(a, b):
    return lax.dot_general(a, b, _NT, preferred_element_type=F32)


def _dot_tn(a, b):
    return lax.dot_general(a, b, _TN, preferred_element_type=F32)


def _split_dot(x, m16, terms=2, left=False):
    out = None
    r = x
    for _ in range(terms):
        p = r.astype(BF16)
        d = _dot(m16, p) if left else _dot(p, m16)
        out = d if out is None else out + d
        r = r - p.astype(F32)
    return out


def _rms(x, g):
    ms = jnp.mean(x * x, axis=-1, keepdims=True)
    return x * lax.rsqrt(ms + NORM_EPS) * g


def _lane_head(width=MIX_W):
    return lax.broadcasted_iota(jnp.int32, (1, width), 1) // HEAD_W


def _block_diag16():
    r = lax.broadcasted_iota(jnp.int32, (MIX_W, MIX_W), 0) // HEAD_W
    c = lax.broadcasted_iota(jnp.int32, (MIX_W, MIX_W), 1) // HEAD_W
    return jnp.where(r == c, 1.0, 0.0).astype(BF16)


def _head_rms(o, gain, bd16):
    ms = _split_dot(o * o, bd16) * (1.0 / HEAD_W)
    return o * lax.rsqrt(ms + NORM_EPS) * gain


def _log_sigmoid(z):
    return jnp.minimum(z, 0.0) - jnp.log(1.0 + jnp.exp(-jnp.abs(z)))


def _rope_slab(s, cos, sin, half):
    lane = lax.broadcasted_iota(jnp.int32, (1, MIX_W), 1)
    first = (lane % (2 * half)) < half
    partner = jnp.where(first, pltpu.roll(s, MIX_W - half, 1), pltpu.roll(s, half, 1))
    return s * cos + partner * sin


def _norm_proj_kernel(x_ref, g_ref, w_ref, o32_ref):
    h = _rms(x_ref[...], g_ref[...]).astype(BF16)
    o32_ref[...] = _dot(h, w_ref[...])


def _mixer_proj_kernel(x_ref, g_ref, w_ref, tab_ref, o32_ref, o16_ref):
    h = _rms(x_ref[...], g_ref[...]).astype(BF16)
    y = _dot(h, w_ref[...])
    cos_r = tab_ref[:, 0 * MIX_W:1 * MIX_W]
    sin_r = tab_ref[:, 1 * MIX_W:2 * MIX_W]
    cos_d = tab_ref[:, 2 * MIX_W:3 * MIX_W]
    sin_d = tab_ref[:, 3 * MIX_W:4 * MIX_W]
    for c in range(N_SLABS):
        s = y[:, c * MIX_W:(c + 1) * MIX_W]
        if c == S_RQ:
            s = _rope_slab(s, cos_r, sin_r, HEAD_W // 2)
        elif c == S_RK:
            s = _rope_slab(s, cos_r, sin_r, HEAD_W // 2) * (HEAD_W ** -0.5)
        elif c in (S_DQ, S_DK):
            s = _rope_slab(s, cos_d, sin_d, DIFF_DH // 2)
        o32_ref[:, c * MIX_W:(c + 1) * MIX_W] = s
        o16_ref[:, c * MIX_W:(c + 1) * MIX_W] = s.astype(BF16)


def _row_tile(n, want):
    t = math.gcd(n, want)
    assert t % 8 == 0
    return t


def _norm_proj(x, gain, w16):
    n, d = x.shape
    wout = w16.shape[1]
    tm = _row_tile(n, 256)
    return pl.pallas_call(
        _norm_proj_kernel,
        grid=(n // tm,),
        in_specs=[pl.BlockSpec((tm, d), lambda i: (i, 0)),
                  pl.BlockSpec((1, d), lambda i: (0, 0)),
                  pl.BlockSpec((d, wout), lambda i: (0, 0))],
        out_specs=pl.BlockSpec((tm, wout), lambda i: (i, 0)),
        out_shape=jax.ShapeDtypeStruct((n, wout), F32),
        compiler_params=_params("parallel"),
        name="norm_proj",
    )(x, gain, w16)


def _mixer_proj(x, gain, w16, tab):
    n, d = x.shape
    tm = _row_tile(n, 256)
    tab_blocks = tab.shape[0] // tm
    return pl.pallas_call(
        _mixer_proj_kernel,
        grid=(n // tm,),
        in_specs=[pl.BlockSpec((tm, d), lambda i: (i, 0)),
                  pl.BlockSpec((1, d), lambda i: (0, 0)),
                  pl.BlockSpec((d, D_IN), lambda i: (0, 0)),
                  pl.BlockSpec((tm, 4 * MIX_W), lambda i: (i % tab_blocks, 0))],
        out_specs=[pl.BlockSpec((tm, D_IN), lambda i: (i, 0)),
                   pl.BlockSpec((tm, D_IN), lambda i: (i, 0))],
        out_shape=[jax.ShapeDtypeStruct((n, D_IN), F32),
                   jax.ShapeDtypeStruct((n, D_IN), BF16)],
        compiler_params=_params("parallel"),
        name="mixer_proj",
    )(x, gain, w16, tab)


def _rope_table(pos):
    def one(group):
        half = group // 2
        inv = ROPE_THETA ** (-jnp.arange(half, dtype=F32) * 2.0 / group)
        ang = pos.astype(F32)[:, None] * inv[None, :]
        cos, sin = jnp.cos(ang), jnp.sin(ang)
        reps = MIX_W // group
        return (jnp.tile(jnp.concatenate([cos, cos], -1), (1, reps)),
                jnp.tile(jnp.concatenate([-sin, sin], -1), (1, reps)))
    cr, sr = one(HEAD_W)
    cd, sd = one(DIFF_DH)
    return jnp.concatenate([cr, sr, cd, sd], axis=-1)


_RET_GAMMA = [1.0 - 2.0 ** (-5.0 - h) for h in range(HEADS)]
_RET_LOG_GAMMA = [float(np.log(g)) for g in _RET_GAMMA]


def _state_step(st, q_dec, k_dec, v, decay_row):
    inter = _dot_nt(q_dec.astype(BF16), st.astype(BF16))
    upd = _dot_tn(v.astype(BF16), k_dec.astype(BF16))
    r = lax.broadcasted_iota(jnp.int32, (MIX_W, MIX_W), 0) // HEAD_W
    c = lax.broadcasted_iota(jnp.int32, (MIX_W, MIX_W), 1) // HEAD_W
    return inter, st * decay_row + jnp.where(r == c, upd, 0.0)


def _retention_kernel(q_ref, k_ref, v_ref, g_ref, gain_ref, st0_ref, o_ref, st_ref, *, chunk):
    @pl.when(pl.program_id(1) == 0)
    def _():
        st_ref[...] = st0_ref[...]

    q, k, v = q_ref[...], k_ref[...], v_ref[...]
    head = _lane_head()
    lg = jnp.zeros((1, MIX_W), F32)
    for h in range(HEADS):
        lg = jnp.where(head == h, _RET_LOG_GAMMA[h], lg)
    steps = lax.broadcasted_iota(jnp.int32, (chunk, 1), 0).astype(F32) + 1.0
    b = steps * lg
    b_last = float(chunk) * lg
    o, st_new = _state_step(st_ref[0], q * jnp.exp(b), k * jnp.exp(b_last - b), v,
                            jnp.exp(b_last))
    st_ref[0] = st_new

    ri = lax.broadcasted_iota(jnp.int32, (chunk, chunk), 0)
    ci = lax.broadcasted_iota(jnp.int32, (chunk, chunk), 1)
    causal = ri >= ci
    dist = (ri - ci).astype(F32)
    k16, v16 = k.astype(BF16), v.astype(BF16)
    for h in range(HEADS):
        sc = _dot_nt(jnp.where(head == h, q, 0.0).astype(BF16), k16)
        dec = jnp.where(causal, jnp.exp(dist * _RET_LOG_GAMMA[h]), 0.0)
        o = o + jnp.where(head == h, _dot((sc * dec).astype(BF16), v16), 0.0)

    g = g_ref[...]
    o_ref[...] = _head_rms(o, gain_ref[...], _block_diag16()) * (g * jax.nn.sigmoid(g))


def _hgrn_kernel(q_ref, f_ref, v_ref, g_ref, gain_ref, lbl_ref, st0_ref, o_ref, st_ref,
                 kk_scr, b_scr, *, chunk, nsub, layer):
    @pl.when(pl.program_id(1) == 0)
    def _():
        st_ref[...] = st0_ref[...]

    lbl = lbl_ref[...]
    e = jnp.exp(lbl - jnp.max(lbl, axis=0, keepdims=True))
    lb_w = e / jnp.sum(e, axis=0, keepdims=True)
    lb = lb_w[0:1, :]
    for i in range(1, layer + 1):
        lb = lb + lb_w[i:i + 1, :]
    lb = lb - lb_w[0:1, :]

    z_all = f_ref[...]
    kk_scr[...] = (1.0 - lb) * jax.nn.sigmoid(-z_all)
    log2_f = jnp.log(lb + (1.0 - lb) * jax.nn.sigmoid(z_all)) * LOG2_E
    ri = lax.broadcasted_iota(jnp.int32, (chunk, chunk), 0)
    ci = lax.broadcasted_iota(jnp.int32, (chunk, chunk), 1)
    tril16 = jnp.where(ri >= ci, 1.0, 0.0).astype(BF16)
    bd16 = _block_diag16()
    rows8 = lax.broadcasted_iota(jnp.int32, (SUBLANES, 1), 0)
    groups = chunk // SUBLANES
    r_blk = lax.broadcasted_iota(jnp.int32, (MIX_W, MIX_W), 0) // HEAD_W
    c_blk = lax.broadcasted_iota(jnp.int32, (MIX_W, MIX_W), 1) // HEAD_W

    bs, q_decs, updates, decays = [], [], [], []
    for s in range(nsub):
        lo = s * chunk
        q, v, k = q_ref[lo:lo + chunk, :], v_ref[lo:lo + chunk, :], kk_scr[lo:lo + chunk, :]
        lf = log2_f[lo:lo + chunk, :]
        b = _split_dot(lf, tril16, terms=3, left=True)
        b_scr[lo:lo + chunk, :] = b
        b_last = b[chunk - 1:chunk, :]
        bs.append(b)
        q_decs.append((q * jnp.exp2(b)).astype(BF16))
        upd = _dot_tn(v.astype(BF16), (k * jnp.exp2(b_last - b)).astype(BF16))
        updates.append(jnp.where(r_blk == c_blk, upd, 0.0))
        decays.append(jnp.exp2(b_last))
    states = [st_ref[0]]
    for s in range(nsub):
        states.append(states[-1] * decays[s] + updates[s])
    st_ref[0] = states[-1]

    for s in range(nsub):
        lo = s * chunk
        q, b = q_ref[lo:lo + chunk, :], bs[s]
        inter = _dot_nt(q_decs[s], states[s].astype(BF16))
        qg = [q[r * SUBLANES:(r + 1) * SUBLANES, :] for r in range(groups)]
        bg = [b[r * SUBLANES:(r + 1) * SUBLANES, :] for r in range(groups)]
        slabs, where = [], []
        for j in range(chunk):
            kj = kk_scr[lo + j:lo + j + 1, :]
            bj = b_scr[lo + j:lo + j + 1, :]
            for r in range(j // SUBLANES, groups):
                first = r * SUBLANES
                t = qg[r] * kj * jnp.exp2(bg[r] - bj)
                slabs.append(jnp.where(rows8 + first >= j, t, 0.0) if first < j else t)
                where.append((j, r))
        pair_sum = _dot(jnp.concatenate(slabs, axis=0).astype(BF16), bd16)
        og = [inter[r * SUBLANES:(r + 1) * SUBLANES, :] for r in range(groups)]
        for i, (j, r) in enumerate(where):
            og[r] = og[r] + pair_sum[i * SUBLANES:(i + 1) * SUBLANES, :] * v_ref[lo + j:lo + j + 1, :]
        o = jnp.concatenate(og, axis=0) if groups > 1 else og[0]
        g = g_ref[lo:lo + chunk, :]
        o_ref[lo:lo + chunk, :] = _head_rms(o, gain_ref[...], bd16) * (g * jax.nn.sigmoid(g))


def _slab_spec(rows, slab, nchunk):
    return pl.BlockSpec((rows, MIX_W), lambda s, c: (s * nchunk + c, slab))


def _recurrence(kind, proj32, nseq, seqlen, chunk, gain, st0, lb_logits=None, layer=0, nsub=1):
    rows = chunk * nsub
    nstep = seqlen // rows
    n = nseq * seqlen
    row = pl.BlockSpec((1, MIX_W), lambda s, c: (0, 0))
    st_spec = pl.BlockSpec((1, MIX_W, MIX_W), lambda s, c: (s, 0, 0))
    out_specs = [pl.BlockSpec((rows, MIX_W), lambda s, c: (s * nstep + c, 0)), st_spec]
    out_shape = [jax.ShapeDtypeStruct((n, MIX_W), F32),
                 jax.ShapeDtypeStruct((nseq, MIX_W, MIX_W), F32)]
    if kind == "ret":
        assert nsub == 1
        slabs = (S_RQ, S_RK, S_RV, S_RG)
        body = functools.partial(_retention_kernel, chunk=chunk)
        extra_specs, extra, scratch = [row, st_spec], (gain, st0), []
    else:
        slabs = (S_GQ, S_GF, S_GI, S_GG)
        body = functools.partial(_hgrn_kernel, chunk=chunk, nsub=nsub, layer=layer)
        extra_specs = [row, pl.BlockSpec(lb_logits.shape, lambda s, c: (0, 0)), st_spec]
        extra = (gain, lb_logits, st0)
        scratch = [pltpu.VMEM((rows, MIX_W), F32), pltpu.VMEM((rows, MIX_W), F32)]
    return pl.pallas_call(
        body,
        grid=(nseq, nstep),
        in_specs=[_slab_spec(rows, s, nstep) for s in slabs] + extra_specs,
        out_specs=out_specs,
        out_shape=out_shape,
        scratch_shapes=scratch,
        compiler_params=_params("parallel", "arbitrary"),
        name="recurrence_" + kind,
    )(proj32, proj32, proj32, proj32, *extra)


def _decode_rec_kernel(q_ref, k_ref, v_ref, g_ref, gain_ref, *rest, per_channel, ls, layer):
    if per_channel:
        lbl_ref, s0_ref, o_ref, s_ref, f_scr, kk_scr, oacc_scr = rest
        lbl = lbl_ref[...]
        e = jnp.exp(lbl - jnp.max(lbl, axis=0, keepdims=True))
        lb_w = e / jnp.sum(e, axis=0, keepdims=True)
        lb = lb_w[0]
        for i in range(1, layer + 1):
            lb = lb + lb_w[i]
        lb = lb - lb_w[0]
        for t in range(ls):
            z = k_ref[t]
            f_scr[t] = lb + (1.0 - lb) * jax.nn.sigmoid(z)
            kk_scr[t] = (1.0 - lb) * jax.nn.sigmoid(-z)
        key_ref = kk_scr
    else:
        s0_ref, o_ref, s_ref, oacc_scr = rest
        h = pl.program_id(0)
        gamma = jnp.float32(_RET_GAMMA[0])
        for i in range(1, HEADS):
            gamma = jnp.where(h == i, jnp.float32(_RET_GAMMA[i]), gamma)
        key_ref = k_ref
    oacc_scr[...] = jnp.zeros_like(oacc_scr)

    def channel(c, carry):
        s = s0_ref[c]
        for t in range(ls):
            f = f_scr[t, pl.ds(c, 1), :] if per_channel else gamma
            s = f * s + key_ref[t, pl.ds(c, 1), :] * v_ref[t]
            oacc_scr[t] += q_ref[t, pl.ds(c, 1), :] * s
        s_ref[c] = s
        return carry

    lax.fori_loop(0, HEAD_W, channel, 0)
    for t in range(ls):
        o = oacc_scr[t]
        ms = jnp.mean(o * o, axis=0, keepdims=True)
        g = g_ref[t]
        o_ref[t] = o * lax.rsqrt(ms + NORM_EPS) * gain_ref[...] * (g * jax.nn.sigmoid(g))


def _decode_recurrence(kind, proj_cm, gain_cm, s0_cm, lbl_cm=None, layer=0):
    _, ls, _, _, nb = proj_cm.shape
    per_channel = kind == "hgrn"
    slabs = (S_GQ, S_GF, S_GI, S_GG) if per_channel else (S_RQ, S_RK, S_RV, S_RG)

    def act(slab):
        return pl.BlockSpec((None, ls, None, HEAD_W, nb), lambda h: (slab, 0, h, 0, 0))

    st_spec = pl.BlockSpec((None, HEAD_W, HEAD_W, nb), lambda h: (h, 0, 0, 0))
    in_specs = [act(s) for s in slabs] + [pl.BlockSpec((None, HEAD_W, nb), lambda h: (h, 0, 0))]
    args = [proj_cm] * 4 + [gain_cm]
    scratch = [pltpu.VMEM((ls, HEAD_W, nb), F32)]
    if per_channel:
        in_specs.append(pl.BlockSpec((lbl_cm.shape[0], None, HEAD_W, nb), lambda h: (0, h, 0, 0)))
        args.append(lbl_cm)
        scratch = [pltpu.VMEM((ls, HEAD_W, nb), F32)] * 3
    return pl.pallas_call(
        functools.partial(_decode_rec_kernel, per_channel=per_channel, ls=ls, layer=layer),
        grid=(HEADS,),
        in_specs=in_specs + [st_spec],
        out_specs=[pl.BlockSpec((ls, None, HEAD_W, nb), lambda h: (0, h, 0, 0)), st_spec],
        out_shape=[jax.ShapeDtypeStruct((ls, HEADS, HEAD_W, nb), F32),
                   jax.ShapeDtypeStruct(s0_cm.shape, F32)],
        scratch_shapes=scratch,
        compiler_params=_params("parallel"),
        name="decode_rec_" + kind,
    )(*args, s0_cm)


def _bd_to_state(bd):
    b = bd.shape[0]
    r = bd.reshape(b, HEADS, HEAD_W, HEADS, HEAD_W)
    st = jnp.stack([r[:, h, :, h, :] for h in range(HEADS)], axis=1)
    return jnp.swapaxes(st, 2, 3)


def _sb_prompt_kernel(q_ref, k_ref, v_ref, o_ref, qs_ref, carry_ref, acc_ref, *, tq, tk):
    qi = pl.program_id(1)
    head = _lane_head()
    q = q_ref[...].astype(F32) * (HEAD_W ** -0.5)
    for h in range(HEADS):
        qs_ref[h * tq:(h + 1) * tq, :] = jnp.where(head == h, q, 0.0).astype(BF16)
    ri = lax.broadcasted_iota(jnp.int32, (tk, tk), 0)
    ci = lax.broadcasted_iota(jnp.int32, (tk, tk), 1)
    later16 = jnp.where(ri > ci, 1.0, 0.0).astype(BF16)
    reps = tk // LANES

    def block(kb, mask):
        start = pl.multiple_of(kb * tk, tk)
        z = _dot_nt(qs_ref[...], k_ref[pl.ds(start, tk), :])
        ls = _log_sigmoid(z)
        u = ls - z
        if mask is not None:
            u = jnp.where(mask, u, 0.0)
        between = jnp.concatenate([carry_ref[...]] * reps, axis=1) + _split_dot(u, later16)
        w = jnp.exp(ls + between)
        if mask is not None:
            w = jnp.where(mask, w, 0.0)
        acc_ref[...] += _dot(w.astype(BF16), v_ref[pl.ds(start, tk), :])
        carry = carry_ref[...] + jnp.sum(u, axis=1, keepdims=True)
        carry_ref[...] = carry
        return jnp.max(carry)

    acc_ref[...] = jnp.zeros_like(acc_ref)
    carry_ref[...] = jnp.zeros_like(carry_ref)
    qpos = qi * tq + lax.broadcasted_iota(jnp.int32, (HEADS * tq, 1), 0) % tq
    diag = qi * tq // tk
    kpos = diag * tk + lax.broadcasted_iota(jnp.int32, (1, tk), 1)
    alive = block(diag, kpos < qpos)

    def cond(state):
        return jnp.logical_and(state[0] >= 0, state[1] > SB_DEAD_LOG)

    def body(state):
        return state[0] - 1, block(state[0], None)

    lax.while_loop(cond, body, (diag - 1, alive))
    acc = acc_ref[...]
    o = jnp.zeros((tq, MIX_W), F32)
    for h in range(HEADS):
        o = jnp.where(head == h, acc[h * tq:(h + 1) * tq, :], o)
    o_ref[...] = o


def _sb_prompt(proj32, proj16, nseq, seqlen):
    tq = tk = 256
    nq = seqlen // tq
    return pl.pallas_call(
        functools.partial(_sb_prompt_kernel, tq=tq, tk=tk),
        grid=(nseq, nq),
        in_specs=[pl.BlockSpec((tq, MIX_W), lambda b, i: (b * nq + i, S_SQ)),
                  pl.BlockSpec((seqlen, MIX_W), lambda b, i: (b, S_SK)),
                  pl.BlockSpec((seqlen, MIX_W), lambda b, i: (b, S_SV))],
        out_specs=pl.BlockSpec((tq, MIX_W), lambda b, i: (b * nq + i, 0)),
        out_shape=jax.ShapeDtypeStruct((nseq * seqlen, MIX_W), F32),
        scratch_shapes=[pltpu.VMEM((HEADS * tq, MIX_W), BF16),
                        pltpu.VMEM((HEADS * tq, LANES), F32),
                        pltpu.VMEM((HEADS * tq, MIX_W), F32)],
        compiler_params=_params("parallel", "arbitrary"),
        name="sb_prompt",
    )(proj32, proj16, proj16)


def _diff_lambda(lam_ref, layer):
    lp = lam_ref[...]
    lam_init = 0.8 - 0.6 * math.exp(-0.3 * layer)
    lam = (jnp.exp(jnp.sum(lp[0:1, :] * lp[1:2, :], axis=1, keepdims=True))
           - jnp.exp(jnp.sum(lp[2:3, :] * lp[3:4, :], axis=1, keepdims=True)) + lam_init)
    return lam, lam_init


def _combo_masks():
    lane = lax.broadcasted_iota(jnp.int32, (1, MIX_W), 1)
    return [(lane // DIFF_DH) == (2 * (c // 2) + (c % 2)) for c in range(2 * HEADS)]


def _diff_prompt_kernel(q_ref, k_ref, v_ref, lam_ref, gain_ref, o_ref, qs_ref, m_ref, acc_ref,
                        *, tq, tk, layer):
    qi = pl.program_id(1)
    head = _lane_head()
    q = q_ref[...] * (DIFF_DH ** -0.5 * LOG2_E)
    for c, mk in enumerate(_combo_masks()):
        qs_ref[c * tq:(c + 1) * tq, :] = jnp.where(mk, q, 0.0).astype(BF16)
    m_ref[...] = jnp.full(m_ref.shape, MASK_VALUE, F32)
    acc_ref[...] = jnp.zeros_like(acc_ref)
    qpos = qi * tq + lax.broadcasted_iota(jnp.int32, (2 * tq, 1), 0) % tq
    reps = tk // LANES

    def block(kb, masked):
        start = pl.multiple_of(kb * tk, tk)
        kblk = k_ref[pl.ds(start, tk), :]
        vblk = v_ref[pl.ds(start, tk), :]
        if masked:
            mask = (start + lax.broadcasted_iota(jnp.int32, (1, tk), 1)) <= qpos
        for h in range(HEADS):
            rows = slice(2 * h * tq, (2 * h + 2) * tq)
            s = _dot_nt(qs_ref[rows, :], kblk)
            if masked:
                s = jnp.where(mask, s, MASK_VALUE)
            m_old = m_ref[rows, :]
            m_new = jnp.maximum(m_old, jnp.max(s, axis=1, keepdims=True))
            m_ref[rows, :] = m_new
            p = jnp.exp2(s - jnp.concatenate([m_new] * reps, axis=1)).astype(BF16)
            vaug = jnp.where(head == h, vblk, jnp.ones_like(vblk))
            alpha = jnp.exp2(m_old - m_new)
            acc_ref[rows, :] = (jnp.concatenate([alpha] * (MIX_W // LANES), axis=1) * acc_ref[rows, :]
                                + _dot(p, vaug))

    def unmasked(kb, carry):
        block(kb, False)
        return carry

    diag = qi * tq // tk
    lax.fori_loop(0, diag, unmasked, 0)
    block(diag, True)

    lam, lam_init = _diff_lambda(lam_ref, layer)
    halves = []
    for half in range(MIX_W // LANES):
        mine = slice(half * LANES, (half + 1) * LANES)
        other = slice((1 - half) * LANES, (2 - half) * LANES)
        o = jnp.zeros((tq, LANES), F32)
        for h in range(half * 2, half * 2 + 2):
            r1 = slice(2 * h * tq, (2 * h + 1) * tq)
            r2 = slice((2 * h + 1) * tq, (2 * h + 2) * tq)
            o1 = acc_ref[r1, mine] / acc_ref[r1, other]
            o2 = acc_ref[r2, mine] / acc_ref[r2, other]
            o = jnp.where(head[:, mine] == h, o1 - lam * o2, o)
        halves.append(o)
    o = jnp.concatenate(halves, axis=1)
    o_ref[...] = _head_rms(o, gain_ref[...], _block_diag16()) * (1.0 - lam_init)


def _diff_prompt(proj32, proj16, nseq, seqlen, lam_p, gain, layer):
    tq = tk = 512
    nq = seqlen // tq
    return pl.pallas_call(
        functools.partial(_diff_prompt_kernel, tq=tq, tk=tk, layer=layer),
        grid=(nseq, nq),
        in_specs=[pl.BlockSpec((tq, MIX_W), lambda b, i: (b * nq + i, S_DQ)),
                  pl.BlockSpec((seqlen, MIX_W), lambda b, i: (b, S_DK)),
                  pl.BlockSpec((seqlen, MIX_W), lambda b, i: (b, S_DV)),
                  pl.BlockSpec(lam_p.shape, lambda b, i: (0, 0)),
                  pl.BlockSpec((1, MIX_W), lambda b, i: (0, 0))],
        out_specs=pl.BlockSpec((tq, MIX_W), lambda b, i: (b * nq + i, 0)),
        out_shape=jax.ShapeDtypeStruct((nseq * seqlen, MIX_W), F32),
        scratch_shapes=[pltpu.VMEM((2 * HEADS * tq, MIX_W), BF16),
                        pltpu.VMEM((2 * HEADS * tq, LANES), F32),
                        pltpu.VMEM((2 * HEADS * tq, MIX_W), F32)],
        compiler_params=_params("parallel", "arbitrary"),
        name="diff_prompt",
    )(proj32, proj16, proj16, lam_p, gain)


def _stack_rows(x, masks):
    return jnp.concatenate([jnp.where(mk, x, 0.0) for mk in masks], axis=0)


def _sample_attn_kernel(pt_ref, q_ref, sk_ref, sv_ref, dq_ref, dk_ref, dv_ref, *rest,
                        ls, n_pages, layer):
    del pt_ref
    pages, (lam_ref, gain_ref, osb_ref, odf_ref, sb_acc, sb_carry) = rest[:4 * n_pages], rest[4 * n_pages:]
    psk, psv, pdk, pdv = (pages[i * n_pages:(i + 1) * n_pages] for i in range(4))
    head = _lane_head()
    sb_masks = [head == h for h in range(HEADS)]
    qs = _stack_rows(q_ref[...] * (HEAD_W ** -0.5), sb_masks).astype(BF16)
    qd = _stack_rows(dq_ref[...] * (DIFF_DH ** -0.5), _combo_masks()).astype(BF16)

    def pad_page(x):
        return jnp.concatenate([x, jnp.zeros((PAGE_SIZE - ls, MIX_W), F32)], axis=0).astype(BF16)

    def later16(n):
        ri = lax.broadcasted_iota(jnp.int32, (n, n), 0)
        ci = lax.broadcasted_iota(jnp.int32, (n, n), 1)
        return jnp.where(ri > ci, 1.0, 0.0).astype(BF16)

    def sb_block(z, pv_fn, mask):
        lsg = _log_sigmoid(z)
        u = lsg - z
        if mask is not None:
            u = jnp.where(mask, u, 0.0)
        between = sb_carry[...] + _split_dot(u, later16(z.shape[1]))
        w = jnp.exp(lsg + between)
        if mask is not None:
            w = jnp.where(mask, w, 0.0)
        sb_acc[...] += pv_fn(w.astype(BF16))
        sb_carry[...] += jnp.sum(u, axis=1, keepdims=True)

    sb_acc[...] = jnp.zeros_like(sb_acc)
    sb_carry[...] = jnp.zeros_like(sb_carry)
    key = lax.broadcasted_iota(jnp.int32, (1, PAGE_SIZE), 1)
    t_sb = lax.broadcasted_iota(jnp.int32, (HEADS * ls, 1), 0) % ls
    sv_new = pad_page(sv_ref[...])
    sb_block(_dot_nt(qs, pad_page(sk_ref[...])), lambda w: _dot(w, sv_new), key < t_sb)

    def sb_past(blk):
        hi = n_pages - 1 - 2 * blk
        kt = jnp.concatenate([psk[hi - 1][...], psk[hi][...]], axis=1).astype(BF16)
        vt = jnp.concatenate([psv[hi - 1][...], psv[hi][...]], axis=1).astype(BF16)
        sb_block(_dot(qs, kt), lambda w: _dot_nt(w, vt), None)

    def sb_rest(blk):
        if blk < n_pages // 2:
            @pl.when(jnp.max(sb_carry[...]) > SB_DEAD_LOG)
            def _():
                sb_past(blk)
                sb_rest(blk + 1)

    sb_past(0)

    t_df = lax.broadcasted_iota(jnp.int32, (2 * HEADS * ls, 1), 0) % ls
    s_new = jnp.where(key <= t_df, _dot_nt(qd, pad_page(dk_ref[...])), MASK_VALUE)
    kt_all = jnp.concatenate([r[...].astype(BF16) for r in pdk], axis=1)
    s_past = _dot(qd, kt_all)
    m = jnp.maximum(jnp.max(s_new, axis=1, keepdims=True), jnp.max(s_past, axis=1, keepdims=True))
    p_new = jnp.exp(s_new - m)
    p_past = jnp.exp(s_past - m)
    l = jnp.sum(p_new, axis=1, keepdims=True) + jnp.sum(p_past, axis=1, keepdims=True)
    vt_all = jnp.concatenate([r[...].astype(BF16) for r in pdv], axis=1)
    on = (_dot(p_new.astype(BF16), pad_page(dv_ref[...]))
          + _dot_nt(p_past.astype(BF16), vt_all)) / l
    lam, lam_init = _diff_lambda(lam_ref, layer)
    o = jnp.zeros((ls, MIX_W), F32)
    for h in range(HEADS):
        o1 = on[(2 * h) * ls:(2 * h + 1) * ls, :]
        o2 = on[(2 * h + 1) * ls:(2 * h + 2) * ls, :]
        o = jnp.where(head == h, o1 - lam * o2, o)
    odf_ref[...] = _head_rms(o, gain_ref[...], _block_diag16()) * (1.0 - lam_init)

    sb_rest(1)
    acc = sb_acc[...]
    o = jnp.zeros((ls, MIX_W), F32)
    for h in range(HEADS):
        o = jnp.where(head == h, acc[h * ls:(h + 1) * ls, :], o)
    osb_ref[...] = o


def _sample_attn(proj32, page_table, caches, nseq, ls, lam_p, gain, layer):
    n_pages = page_table.shape[1]
    assert n_pages % 2 == 0

    def slab(c):
        return pl.BlockSpec((ls, MIX_W), lambda b, pt: (b, c))

    def page(p):
        return pl.BlockSpec((None, None, MIX_W, PAGE_SIZE), lambda b, pt: (layer, pt[b, p], 0, 0))

    out = pl.BlockSpec((ls, MIX_W), lambda b, pt: (b, 0))
    grid_spec = pltpu.PrefetchScalarGridSpec(
        num_scalar_prefetch=1,
        grid=(nseq,),
        in_specs=[slab(S_SQ), slab(S_SK), slab(S_SV), slab(S_DQ), slab(S_DK), slab(S_DV)]
                 + [page(p) for _ in range(4) for p in range(n_pages)]
                 + [pl.BlockSpec(lam_p.shape, lambda b, pt: (0, 0)),
                    pl.BlockSpec((1, MIX_W), lambda b, pt: (0, 0))],
        out_specs=[out, out],
        scratch_shapes=[pltpu.VMEM((HEADS * ls, MIX_W), F32), pltpu.VMEM((HEADS * ls, 1), F32)])
    page_args = [c for c in caches for _ in range(n_pages)]
    return pl.pallas_call(
        functools.partial(_sample_attn_kernel, ls=ls, n_pages=n_pages, layer=layer),
        grid_spec=grid_spec,
        out_shape=[jax.ShapeDtypeStruct((nseq * ls, MIX_W), F32)] * 2,
        compiler_params=_params("arbitrary"),
        name="sample_attn",
    )(page_table, *([proj32] * 6), *page_args, lam_p, gain)


def _merge_kernel(x_ref, b0_ref, b1_ref, b2_ref, b3_ref, pre_ref, wg_ref, wbr_ref, wo_ref,
                  post_ref, o_ref):
    x = x_ref[...]
    h = _rms(x, pre_ref[...]).astype(BF16)
    merged = None
    for i, b_ref in enumerate((b0_ref, b1_ref, b2_ref, b3_ref)):
        gate = jax.nn.sigmoid(_dot(h, wg_ref[:, i * D_MODEL:(i + 1) * D_MODEL]))
        term = gate * _dot(b_ref[...].astype(BF16), wbr_ref[i])
        merged = term if merged is None else merged + term
    o_ref[...] = x + _rms(_dot(merged.astype(BF16), wo_ref[...]), post_ref[...])


def _const_spec(shape):
    nd = len(shape)
    return pl.BlockSpec(shape, lambda i: (0,) * nd, pipeline_mode=pl.Buffered(1))


def _merge(x, branches, pre, wg16, wbr16, wo16, post):
    n = x.shape[0]
    tm = _row_tile(n, 256)
    tok = pl.BlockSpec((tm, D_MODEL), lambda i: (i, 0))
    br = pl.BlockSpec((tm, MIX_W), lambda i: (i, 0))
    return pl.pallas_call(
        _merge_kernel,
        grid=(n // tm,),
        in_specs=[tok, br, br, br, br, _const_spec((1, D_MODEL)), _const_spec(wg16.shape),
                  _const_spec(wbr16.shape), _const_spec(wo16.shape), _const_spec((1, D_MODEL))],
        out_specs=tok,
        out_shape=jax.ShapeDtypeStruct((n, D_MODEL), F32),
        compiler_params=_params("parallel"),
        name="merge",
    )(x, *branches, pre, wg16, wbr16, wo16, post)


def _cross_attn_kernel(q_ref, mk_ref, mv_ref, o_ref, *, channel_major):
    head = _lane_head()
    q = q_ref[...] * (HEAD_W ** -0.5)
    mk = mk_ref[...].astype(BF16)
    mv = mv_ref[...].astype(BF16)
    o = jnp.zeros(q.shape, F32)
    for h in range(HEADS):
        qh = jnp.where(head == h, q, 0.0).astype(BF16)
        s = _dot(qh, mk) if channel_major else _dot_nt(qh, mk)
        p = jnp.exp(s - jnp.max(s, axis=1, keepdims=True))
        p16 = p.astype(BF16)
        pv = _dot_nt(p16, mv) if channel_major else _dot(p16, mv)
        o = jnp.where(head == h, pv / jnp.sum(p, axis=1, keepdims=True), o)
    o_ref[...] = o


def _cross_attn(cq, mk, mv, nseq, seqlen, channel_major):
    tm = _row_tile(seqlen, 512)
    nt = seqlen // tm
    mem = pl.BlockSpec((None,) + mk.shape[1:], lambda b, i: (b, 0, 0))
    tok = pl.BlockSpec((tm, MIX_W), lambda b, i: (b * nt + i, 0))
    return pl.pallas_call(
        functools.partial(_cross_attn_kernel, channel_major=channel_major),
        grid=(nseq, nt),
        in_specs=[tok, mem, mem],
        out_specs=tok,
        out_shape=jax.ShapeDtypeStruct(cq.shape, F32),
        compiler_params=_params("parallel", "parallel"),
        name="cross_attn",
    )(cq, mk, mv)


def _post_kernel(x_ref, co_ref, wco_ref, cpost_ref, mpre_ref, wup_ref, wdn_ref, mpost_ref, o_ref):
    x = x_ref[...] + _rms(_dot(co_ref[...].astype(BF16), wco_ref[...]), cpost_ref[...])
    h = _rms(x, mpre_ref[...]).astype(BF16)
    u = jnp.square(jnp.maximum(_dot(h, wup_ref[...]), 0.0))
    o_ref[...] = x + _rms(_dot(u.astype(BF16), wdn_ref[...]), mpost_ref[...])


def _post(x, co, wco16, cpost, mpre, wup16, wdn16, mpost):
    n = x.shape[0]
    tm = _row_tile(n, 256)
    tok = pl.BlockSpec((tm, D_MODEL), lambda i: (i, 0))
    return pl.pallas_call(
        _post_kernel,
        grid=(n // tm,),
        in_specs=[tok, pl.BlockSpec((tm, MIX_W), lambda i: (i, 0)), _const_spec(wco16.shape),
                  _const_spec((1, D_MODEL)), _const_spec((1, D_MODEL)), _const_spec(wup16.shape),
                  _const_spec(wdn16.shape), _const_spec((1, D_MODEL))],
        out_specs=tok,
        out_shape=jax.ShapeDtypeStruct((n, D_MODEL), F32),
        compiler_params=_params("parallel"),
        name="ca_out_mlp",
    )(x, co, wco16, cpost, mpre, wup16, wdn16, mpost)


def _trunk_layer(x, nseq, seqlen, rope_tab, mem_k, mem_v, mem_channel_major, w, rec_fn, attn_fn):
    proj32, proj16 = _mixer_proj(x, w["mix_pre_norm"], w["w_in"], rope_tab)
    o_ret, st_ret, o_hgrn, st_hgrn = rec_fn(proj32)
    o_sb, o_diff = attn_fn(proj32, proj16)
    x = _merge(x, (o_ret, o_sb, o_diff, o_hgrn), w["mix_pre_norm"], w["w_gate"], w["w_br"],
               w["w_out"], w["mix_post_norm"])
    cq = _norm_proj(x, w["ca_pre_norm"], w["w_ca_q"])
    co = _cross_attn(cq, mem_k, mem_v, nseq, seqlen, mem_channel_major)
    x = _post(x, co, w["w_ca_o"], w["ca_post_norm"], w["mlp_pre_norm"], w["w_mlp_up"],
              w["w_mlp_down"], w["mlp_post_norm"])
    return x, proj32, st_ret, st_hgrn


def kernel(x_prompt, x_sample, mem_prompt, cache_sb_k, cache_sb_v, cache_diff_k, cache_diff_v, cache_mem_k, cache_mem_v, state_ret, state_hgrn, page_table, mix_pre_norm, mix_post_norm, ca_pre_norm, ca_post_norm, mlp_pre_norm, mlp_post_norm, mem_norm, w_in, w_gate, ret_norm, diff_lambda, diff_norm, hgrn_lb_logits, hgrn_norm, w_br_ret, w_br_sb, w_br_diff, w_br_hgrn, w_out, w_ca_q, w_ca_k, w_ca_v, w_ca_o, w_mlp_up, w_mlp_down):
    depth = w_in.shape[0]
    B, L, _ = x_prompt.shape
    DB, LS, _ = x_sample.shape
    n_pages = page_table.shape[1]
    past_len = n_pages * PAGE_SIZE

    tab_p = _rope_table(jnp.arange(L, dtype=jnp.int32))
    tab_s = jnp.tile(_rope_table(past_len + jnp.arange(LS, dtype=jnp.int32)), (256 // LS, 1))
    caches = [jnp.transpose(c, (0, 1, 3, 4, 2)).reshape(c.shape[0], c.shape[1], MIX_W, PAGE_SIZE)
              for c in (cache_sb_k, cache_sb_v, cache_diff_k, cache_diff_v)]
    mem_cache = [jnp.transpose(c, (0, 1, 3, 4, 2)).reshape(c.shape[0], c.shape[1], MIX_W, N_MEM)
                 for c in (cache_mem_k, cache_mem_v)]
    w_br = jnp.stack([w_br_ret, w_br_sb, w_br_diff, w_br_hgrn], axis=1).astype(BF16)
    w_ca_kv = jnp.concatenate([w_ca_k, w_ca_v], axis=-1).astype(BF16)
    diff_gain = jnp.tile(diff_norm, (1, HEADS))
    def lanes_cm(a):
        return jnp.broadcast_to(a.reshape(depth, HEADS, HEAD_W, 1), (depth, HEADS, HEAD_W, DB))
    ret_gain_cm, hgrn_gain_cm, lbl_cm = lanes_cm(ret_norm), lanes_cm(hgrn_norm), lanes_cm(hgrn_lb_logits)
    st_ret_cm = state_ret.transpose(0, 2, 3, 4, 1)
    st_hgrn_cm = state_hgrn.transpose(0, 2, 3, 4, 1)

    xp = x_prompt.reshape(B * L, D_MODEL)
    xs = x_sample.reshape(DB * LS, D_MODEL)
    mem = mem_prompt.reshape(B * N_MEM, D_MODEL)
    zero_state = jnp.zeros((B, MIX_W, MIX_W), F32)
    outs_p, outs_s = [], []
    for l in range(depth):
        row = lambda a: a[l][None, :]
        w = {
            "mix_pre_norm": row(mix_pre_norm), "mix_post_norm": row(mix_post_norm),
            "ca_pre_norm": row(ca_pre_norm), "ca_post_norm": row(ca_post_norm),
            "mlp_pre_norm": row(mlp_pre_norm), "mlp_post_norm": row(mlp_post_norm),
            "ret_norm": row(ret_norm), "hgrn_norm": row(hgrn_norm),
            "hgrn_lb_logits": hgrn_lb_logits,
            "w_in": w_in[l].astype(BF16), "w_gate": w_gate[l].astype(BF16), "w_br": w_br[l],
            "w_out": w_out[l].astype(BF16), "w_ca_q": w_ca_q[l].astype(BF16),
            "w_ca_o": w_ca_o[l].astype(BF16), "w_mlp_up": w_mlp_up[l].astype(BF16),
            "w_mlp_down": w_mlp_down[l].astype(BF16),
        }
        lam_p, dgain = diff_lambda[l], diff_gain[l][None, :]

        mkv = _norm_proj(mem, row(mem_norm), w_ca_kv[l])
        mk_p = mkv[:, :MIX_W].reshape(B, N_MEM, MIX_W)
        mv_p = mkv[:, MIX_W:].reshape(B, N_MEM, MIX_W)

        def rec_p(proj32, l=l, w=w):
            o_ret, st_ret = _recurrence("ret", proj32, B, L, 256, w["ret_norm"], zero_state)
            o_hgrn, st_hgrn = _recurrence("hgrn", proj32, B, L, 32, w["hgrn_norm"], zero_state,
                                          hgrn_lb_logits, l, nsub=8)
            return o_ret, _bd_to_state(st_ret), o_hgrn, _bd_to_state(st_hgrn)

        def attn_p(proj32, proj16, l=l, lam_p=lam_p, dgain=dgain):
            return (_sb_prompt(proj32, proj16, B, L),
                    _diff_prompt(proj32, proj16, B, L, lam_p, dgain, l))

        xp, proj_p, ret_p, hgrn_p = _trunk_layer(xp, B, L, tab_p, mk_p, mv_p, False, w, rec_p, attn_p)
        outs_p.append((proj_p, ret_p, hgrn_p, mk_p, mv_p))

        def rec_s(proj32, l=l):
            pcm = proj32.reshape(DB, LS, N_SLABS, HEADS, HEAD_W).transpose(2, 1, 3, 4, 0)
            o_ret, st_ret = _decode_recurrence("ret", pcm, ret_gain_cm[l], st_ret_cm[l])
            o_hgrn, st_hgrn = _decode_recurrence("hgrn", pcm, hgrn_gain_cm[l], st_hgrn_cm[l],
                                                 lbl_cm, l)
            rows = lambda o: o.transpose(3, 0, 1, 2).reshape(DB * LS, MIX_W)
            return (rows(o_ret), st_ret.transpose(3, 0, 1, 2),
                    rows(o_hgrn), st_hgrn.transpose(3, 0, 1, 2))

        def attn_s(proj32, proj16, l=l, lam_p=lam_p, dgain=dgain):
            del proj16
            return _sample_attn(proj32, page_table, caches, DB, LS, lam_p, dgain, l)

        xs, proj_s, ret_s, hgrn_s = _trunk_layer(xs, DB, LS, tab_s, mem_cache[0][l], mem_cache[1][l],
                                                 True, w, rec_s, attn_s)
        outs_s.append((proj_s, ret_s, hgrn_s))

    def slab(outs, c, nseq, seqlen):
        return jnp.stack([o[0][:, c * MIX_W:(c + 1) * MIX_W].reshape(nseq, seqlen, HEADS, HEAD_W)
                          for o in outs], axis=0)

    def states(outs, i):
        return jnp.stack([o[i] for o in outs], axis=0)

    def memkv(i):
        return jnp.stack([o[i].reshape(B, N_MEM, HEADS, HEAD_W) for o in outs_p], axis=0)

    return (xp.reshape(B, L, D_MODEL), xs.reshape(DB, LS, D_MODEL),
            slab(outs_p, S_SK, B, L), slab(outs_p, S_SV, B, L),
            slab(outs_p, S_DK, B, L), slab(outs_p, S_DV, B, L),
            memkv(3), memkv(4), states(outs_p, 1), states(outs_p, 2),
            slab(outs_s, S_SK, DB, LS), slab(outs_s, S_SV, DB, LS),
            slab(outs_s, S_DK, DB, LS), slab(outs_s, S_DV, DB, LS),
            states(outs_s, 1), states(outs_s, 2))
```

```python
import functools
import math

import jax
import jax.numpy as jnp
import numpy as np
from jax import lax
from jax.experimental import pallas as pl
from jax.experimental.pallas import tpu as pltpu

F32 = jnp.float32
BF16 = jnp.bfloat16

D_MODEL = 1024
N_MEM = 256
HEADS = 4
HEAD_W = 64
MIX_W = HEADS * HEAD_W
DIFF_DH = 32
D_FF = 4 * D_MODEL
N_BRANCH = 4
PAGE_SIZE = 128
ROPE_THETA = 10000.0
NORM_EPS = 1e-6
MASK_VALUE = -1e30
LOG2_E = math.log2(math.e)
LANES = 128
SUBLANES = 8
assert MIX_W == 2 * LANES and 2 * HEAD_W == LANES
N_SLABS = 14
D_IN = N_SLABS * MIX_W
(S_RQ, S_RK, S_RV, S_RG, S_SQ, S_SK, S_SV, S_DQ, S_DK, S_DV,
 S_GQ, S_GF, S_GI, S_GG) = range(N_SLABS)
KV_SLABS = (S_SK, S_SV, S_DK, S_DV)

SB_DEAD_LOG = -104.0

V7X_VMEM_BYTES = 64 * 1024 * 1024
VMEM_LIMIT = (V7X_VMEM_BYTES * 7) // 8

_NT = (((1,), (1,)), ((), ()))
_TN = (((0,), (0,)), ((), ()))


def _params(*sem):
    return pltpu.CompilerParams(dimension_semantics=sem, vmem_limit_bytes=VMEM_LIMIT)


def _dot(a, b):
    return jnp.dot(a, b, preferred_element_type=F32)


def _dot_nt(a, b):
    return lax.dot_general(a, b, _NT, preferred_element_type=F32)


def _dot_tn(a, b):
    return lax.dot_general(a, b, _TN, preferred_element_type=F32)


def _split_dot(x, m16, terms=2, left=False):
    out = None
    r = x
    for _ in range(terms):
        p = r.astype(BF16)
        d = _dot(m16, p) if left else _dot(p, m16)
        out = d if out is None else out + d
        r = r - p.astype(F32)
    return out


def _rms(x, g):
    ms = jnp.mean(x * x, axis=-1, keepdims=True)
    return x * lax.rsqrt(ms + NORM_EPS) * g


def _lane_head(width=MIX_W):
    return lax.broadcasted_iota(jnp.int32, (1, width), 1) // HEAD_W


def _block_diag16():
    r = lax.broadcasted_iota(jnp.int32, (MIX_W, MIX_W), 0) // HEAD_W
    c = lax.broadcasted_iota(jnp.int32, (MIX_W, MIX_W), 1) // HEAD_W
    return jnp.where(r == c, 1.0, 0.0).astype(BF16)


def _head_rms(o, gain, bd16):
    ms = _split_dot(o * o, bd16) * (1.0 / HEAD_W)
    return o * lax.rsqrt(ms + NORM_EPS) * gain


def _log_sigmoid(z):
    return jnp.minimum(z, 0.0) - jnp.log(1.0 + jnp.exp(-jnp.abs(z)))


def _rope_slab(s, cos, sin, half):
    lane = lax.broadcasted_iota(jnp.int32, (1, MIX_W), 1)
    first = (lane % (2 * half)) < half
    partner = jnp.where(first, pltpu.roll(s, MIX_W - half, 1), pltpu.roll(s, half, 1))
    return s * cos + partner * sin


def _norm_proj_kernel(x_ref, g_ref, w_ref, o32_ref):
    h = _rms(x_ref[...], g_ref[...]).astype(BF16)
    o32_ref[...] = _dot(h, w_ref[...])


def _mixer_proj_kernel(x_ref, g_ref, w_ref, tab_ref, o32_ref, *kv_refs):
    h = _rms(x_ref[...], g_ref[...]).astype(BF16)
    y = _dot(h, w_ref[...])
    cos_r = tab_ref[:, 0 * MIX_W:1 * MIX_W]
    sin_r = tab_ref[:, 1 * MIX_W:2 * MIX_W]
    cos_d = tab_ref[:, 2 * MIX_W:3 * MIX_W]
    sin_d = tab_ref[:, 3 * MIX_W:4 * MIX_W]
    for c in range(N_SLABS):
        s = y[:, c * MIX_W:(c + 1) * MIX_W]
        if c == S_RQ:
            s = _rope_slab(s, cos_r, sin_r, HEAD_W // 2)
        elif c == S_RK:
            s = _rope_slab(s, cos_r, sin_r, HEAD_W // 2) * (HEAD_W ** -0.5)
        elif c in (S_DQ, S_DK):
            s = _rope_slab(s, cos_d, sin_d, DIFF_DH // 2)
        o32_ref[:, c * MIX_W:(c + 1) * MIX_W] = s
        if kv_refs and c in KV_SLABS:
            kv16_ref, kvcm_ref = kv_refs
            i = KV_SLABS.index(c)
            kv16_ref[:, i * MIX_W:(i + 1) * MIX_W] = s.astype(BF16)
            kvcm_ref[i] = s.T


def _row_tile(n, want):
    t = math.gcd(n, want)
    assert t % 8 == 0
    return t


def _norm_proj(x, gain, w16):
    n, d = x.shape
    wout = w16.shape[1]
    tm = _row_tile(n, 256)
    return pl.pallas_call(
        _norm_proj_kernel,
        grid=(n // tm,),
        in_specs=[pl.BlockSpec((tm, d), lambda i: (i, 0)),
                  pl.BlockSpec((1, d), lambda i: (0, 0)),
                  pl.BlockSpec((d, wout), lambda i: (0, 0))],
        out_specs=pl.BlockSpec((tm, wout), lambda i: (i, 0)),
        out_shape=jax.ShapeDtypeStruct((n, wout), F32),
        compiler_params=_params("parallel"),
        name="norm_proj",
    )(x, gain, w16)


def _mixer_proj(x, gain, w16, tab, kv_seqs=None):
    n, d = x.shape
    tm = _row_tile(n, 256)
    tab_blocks = tab.shape[0] // tm
    out_specs = [pl.BlockSpec((tm, D_IN), lambda i: (i, 0))]
    out_shape = [jax.ShapeDtypeStruct((n, D_IN), F32)]
    if kv_seqs is not None:
        nseq, seqlen = kv_seqs
        tiles = seqlen // tm
        nkv = len(KV_SLABS)
        out_specs += [pl.BlockSpec((tm, nkv * MIX_W), lambda i: (i, 0)),
                      pl.BlockSpec((nkv, None, MIX_W, tm), lambda i: (0, i // tiles, 0, i % tiles))]
        out_shape += [jax.ShapeDtypeStruct((n, nkv * MIX_W), BF16),
                      jax.ShapeDtypeStruct((nkv, nseq, MIX_W, seqlen), F32)]
    return pl.pallas_call(
        _mixer_proj_kernel,
        grid=(n // tm,),
        in_specs=[pl.BlockSpec((tm, d), lambda i: (i, 0)),
                  pl.BlockSpec((1, d), lambda i: (0, 0)),
                  pl.BlockSpec((d, D_IN), lambda i: (0, 0)),
                  pl.BlockSpec((tm, 4 * MIX_W), lambda i: (i % tab_blocks, 0))],
        out_specs=out_specs,
        out_shape=out_shape,
        compiler_params=_params("parallel"),
        name="mixer_proj",
    )(x, gain, w16, tab)


def _rope_table(pos):
    def one(group):
        half = group // 2
        inv = ROPE_THETA ** (-jnp.arange(half, dtype=F32) * 2.0 / group)
        ang = pos.astype(F32)[:, None] * inv[None, :]
        cos, sin = jnp.cos(ang), jnp.sin(ang)
        reps = MIX_W // group
        return (jnp.tile(jnp.concatenate([cos, cos], -1), (1, reps)),
                jnp.tile(jnp.concatenate([-sin, sin], -1), (1, reps)))
    cr, sr = one(HEAD_W)
    cd, sd = one(DIFF_DH)
    return jnp.concatenate([cr, sr, cd, sd], axis=-1)


_RET_GAMMA = [1.0 - 2.0 ** (-5.0 - h) for h in range(HEADS)]
_RET_LOG_GAMMA = [float(np.log(g)) for g in _RET_GAMMA]


def _state_step(st, q_dec, k_dec, v, decay_row):
    inter = _dot_nt(q_dec.astype(BF16), st.astype(BF16))
    upd = _dot_tn(v.astype(BF16), k_dec.astype(BF16))
    r = lax.broadcasted_iota(jnp.int32, (MIX_W, MIX_W), 0) // HEAD_W
    c = lax.broadcasted_iota(jnp.int32, (MIX_W, MIX_W), 1) // HEAD_W
    return inter, st * decay_row + jnp.where(r == c, upd, 0.0)


def _retention_kernel(q_ref, k_ref, v_ref, g_ref, gain_ref, st0_ref, o_ref, st_ref, *, chunk):
    @pl.when(pl.program_id(1) == 0)
    def _():
        st_ref[...] = st0_ref[...]

    q, k, v = q_ref[...], k_ref[...], v_ref[...]
    head = _lane_head()
    lg = jnp.zeros((1, MIX_W), F32)
    for h in range(HEADS):
        lg = jnp.where(head == h, _RET_LOG_GAMMA[h], lg)
    steps = lax.broadcasted_iota(jnp.int32, (chunk, 1), 0).astype(F32) + 1.0
    b = steps * lg
    b_last = float(chunk) * lg
    o, st_new = _state_step(st_ref[0], q * jnp.exp(b), k * jnp.exp(b_last - b), v,
                            jnp.exp(b_last))
    st_ref[0] = st_new

    ri = lax.broadcasted_iota(jnp.int32, (chunk, chunk), 0)
    ci = lax.broadcasted_iota(jnp.int32, (chunk, chunk), 1)
    causal = ri >= ci
    dist = (ri - ci).astype(F32)
    k16, v16 = k.astype(BF16), v.astype(BF16)
    for h in range(HEADS):
        sc = _dot_nt(jnp.where(head == h, q, 0.0).astype(BF16), k16)
        dec = jnp.where(causal, jnp.exp(dist * _RET_LOG_GAMMA[h]), 0.0)
        o = o + jnp.where(head == h, _dot((sc * dec).astype(BF16), v16), 0.0)

    g = g_ref[...]
    o_ref[...] = _head_rms(o, gain_ref[...], _block_diag16()) * (g * jax.nn.sigmoid(g))


def _hgrn_kernel(q_ref, f_ref, v_ref, g_ref, gain_ref, lbl_ref, st0_ref, o_ref, st_ref,
                 kk_scr, b_scr, *, chunk, nsub, layer):
    @pl.when(pl.program_id(1) == 0)
    def _():
        st_ref[...] = st0_ref[...]

    lbl = lbl_ref[...]
    e = jnp.exp(lbl - jnp.max(lbl, axis=0, keepdims=True))
    lb_w = e / jnp.sum(e, axis=0, keepdims=True)
    lb = lb_w[0:1, :]
    for i in range(1, layer + 1):
        lb = lb + lb_w[i:i + 1, :]
    lb = lb - lb_w[0:1, :]

    z_all = f_ref[...]
    kk_scr[...] = (1.0 - lb) * jax.nn.sigmoid(-z_all)
    log2_f = jnp.log(lb + (1.0 - lb) * jax.nn.sigmoid(z_all)) * LOG2_E
    ri = lax.broadcasted_iota(jnp.int32, (chunk, chunk), 0)
    ci = lax.broadcasted_iota(jnp.int32, (chunk, chunk), 1)
    tril16 = jnp.where(ri >= ci, 1.0, 0.0).astype(BF16)
    bd16 = _block_diag16()
    rows8 = lax.broadcasted_iota(jnp.int32, (SUBLANES, 1), 0)
    groups = chunk // SUBLANES
    r_blk = lax.broadcasted_iota(jnp.int32, (MIX_W, MIX_W), 0) // HEAD_W
    c_blk = lax.broadcasted_iota(jnp.int32, (MIX_W, MIX_W), 1) // HEAD_W

    bs, q_decs, updates, decays = [], [], [], []
    for s in range(nsub):
        lo = s * chunk
        q, v, k = q_ref[lo:lo + chunk, :], v_ref[lo:lo + chunk, :], kk_scr[lo:lo + chunk, :]
        lf = log2_f[lo:lo + chunk, :]
        b = _split_dot(lf, tril16, terms=3, left=True)
        b_scr[lo:lo + chunk, :] = b
        b_last = b[chunk - 1:chunk, :]
        bs.append(b)
        q_decs.append((q * jnp.exp2(b)).astype(BF16))
        upd = _dot_tn(v.astype(BF16), (k * jnp.exp2(b_last - b)).astype(BF16))
        updates.append(jnp.where(r_blk == c_blk, upd, 0.0))
        decays.append(jnp.exp2(b_last))
    states = [st_ref[0]]
    for s in range(nsub):
        states.append(states[-1] * decays[s] + updates[s])
    st_ref[0] = states[-1]

    for s in range(nsub):
        lo = s * chunk
        q, b = q_ref[lo:lo + chunk, :], bs[s]
        inter = _dot_nt(q_decs[s], states[s].astype(BF16))
        qg = [q[r * SUBLANES:(r + 1) * SUBLANES, :] for r in range(groups)]
        bg = [b[r * SUBLANES:(r + 1) * SUBLANES, :] for r in range(groups)]
        slabs, where = [], []
        for j in range(chunk):
            kj = kk_scr[lo + j:lo + j + 1, :]
            bj = b_scr[lo + j:lo + j + 1, :]
            for r in range(j // SUBLANES, groups):
                first = r * SUBLANES
                t = qg[r] * kj * jnp.exp2(bg[r] - bj)
                slabs.append(jnp.where(rows8 + first >= j, t, 0.0) if first < j else t)
                where.append((j, r))
        pair_sum = _dot(jnp.concatenate(slabs, axis=0).astype(BF16), bd16)
        og = [inter[r * SUBLANES:(r + 1) * SUBLANES, :] for r in range(groups)]
        for i, (j, r) in enumerate(where):
            og[r] = og[r] + pair_sum[i * SUBLANES:(i + 1) * SUBLANES, :] * v_ref[lo + j:lo + j + 1, :]
        o = jnp.concatenate(og, axis=0) if groups > 1 else og[0]
        g = g_ref[lo:lo + chunk, :]
        o_ref[lo:lo + chunk, :] = _head_rms(o, gain_ref[...], bd16) * (g * jax.nn.sigmoid(g))


def _slab_spec(rows, slab, nchunk):
    return pl.BlockSpec((rows, MIX_W), lambda s, c: (s * nchunk + c, slab))


def _recurrence(kind, proj32, nseq, seqlen, chunk, gain, st0, lb_logits=None, layer=0, nsub=1):
    rows = chunk * nsub
    nstep = seqlen // rows
    n = nseq * seqlen
    row = pl.BlockSpec((1, MIX_W), lambda s, c: (0, 0))
    st_spec = pl.BlockSpec((1, MIX_W, MIX_W), lambda s, c: (s, 0, 0))
    out_specs = [pl.BlockSpec((rows, MIX_W), lambda s, c: (s * nstep + c, 0)), st_spec]
    out_shape = [jax.ShapeDtypeStruct((n, MIX_W), F32),
                 jax.ShapeDtypeStruct((nseq, MIX_W, MIX_W), F32)]
    if kind == "ret":
        assert nsub == 1
        slabs = (S_RQ, S_RK, S_RV, S_RG)
        body = functools.partial(_retention_kernel, chunk=chunk)
        extra_specs, extra, scratch = [row, st_spec], (gain, st0), []
    else:
        slabs = (S_GQ, S_GF, S_GI, S_GG)
        body = functools.partial(_hgrn_kernel, chunk=chunk, nsub=nsub, layer=layer)
        extra_specs = [row, pl.BlockSpec(lb_logits.shape, lambda s, c: (0, 0)), st_spec]
        extra = (gain, lb_logits, st0)
        scratch = [pltpu.VMEM((rows, MIX_W), F32), pltpu.VMEM((rows, MIX_W), F32)]
    return pl.pallas_call(
        body,
        grid=(nseq, nstep),
        in_specs=[_slab_spec(rows, s, nstep) for s in slabs] + extra_specs,
        out_specs=out_specs,
        out_shape=out_shape,
        scratch_shapes=scratch,
        compiler_params=_params("parallel", "arbitrary"),
        name="recurrence_" + kind,
    )(proj32, proj32, proj32, proj32, *extra)


def _decode_rec_kernel(q_ref, k_ref, v_ref, g_ref, gain_ref, *rest, per_channel, ls, layer):
    if per_channel:
        lbl_ref, s0_ref, o_ref, s_ref, f_scr, kk_scr, oacc_scr = rest
        lbl = lbl_ref[...]
        e = jnp.exp(lbl - jnp.max(lbl, axis=0, keepdims=True))
        lb_w = e / jnp.sum(e, axis=0, keepdims=True)
        lb = lb_w[0]
        for i in range(1, layer + 1):
            lb = lb + lb_w[i]
        lb = lb - lb_w[0]
        for t in range(ls):
            z = k_ref[t]
            f_scr[t] = lb + (1.0 - lb) * jax.nn.sigmoid(z)
            kk_scr[t] = (1.0 - lb) * jax.nn.sigmoid(-z)
        key_ref = kk_scr
    else:
        s0_ref, o_ref, s_ref, oacc_scr = rest
        h = pl.program_id(0)
        gamma = jnp.float32(_RET_GAMMA[0])
        for i in range(1, HEADS):
            gamma = jnp.where(h == i, jnp.float32(_RET_GAMMA[i]), gamma)
        key_ref = k_ref
    oacc_scr[...] = jnp.zeros_like(oacc_scr)

    def channel(c, carry):
        s = s0_ref[c]
        for t in range(ls):
            f = f_scr[t, pl.ds(c, 1), :] if per_channel else gamma
            s = f * s + key_ref[t, pl.ds(c, 1), :] * v_ref[t]
            oacc_scr[t] += q_ref[t, pl.ds(c, 1), :] * s
        s_ref[c] = s
        return carry

    lax.fori_loop(0, HEAD_W, channel, 0)
    for t in range(ls):
        o = oacc_scr[t]
        ms = jnp.mean(o * o, axis=0, keepdims=True)
        g = g_ref[t]
        o_ref[t] = o * lax.rsqrt(ms + NORM_EPS) * gain_ref[...] * (g * jax.nn.sigmoid(g))


def _decode_recurrence(kind, proj_cm, gain_cm, s0_cm, lbl_cm=None, layer=0):
    _, ls, _, _, nb = proj_cm.shape
    per_channel = kind == "hgrn"
    slabs = (S_GQ, S_GF, S_GI, S_GG) if per_channel else (S_RQ, S_RK, S_RV, S_RG)

    def act(slab):
        return pl.BlockSpec((None, ls, None, HEAD_W, nb), lambda h: (slab, 0, h, 0, 0))

    st_spec = pl.BlockSpec((None, HEAD_W, HEAD_W, nb), lambda h: (h, 0, 0, 0))
    in_specs = [act(s) for s in slabs] + [pl.BlockSpec((None, HEAD_W, nb), lambda h: (h, 0, 0))]
    args = [proj_cm] * 4 + [gain_cm]
    scratch = [pltpu.VMEM((ls, HEAD_W, nb), F32)]
    if per_channel:
        in_specs.append(pl.BlockSpec((lbl_cm.shape[0], None, HEAD_W, nb), lambda h: (0, h, 0, 0)))
        args.append(lbl_cm)
        scratch = [pltpu.VMEM((ls, HEAD_W, nb), F32)] * 3
    return pl.pallas_call(
        functools.partial(_decode_rec_kernel, per_channel=per_channel, ls=ls, layer=layer),
        grid=(HEADS,),
        in_specs=in_specs + [st_spec],
        out_specs=[pl.BlockSpec((ls, None, HEAD_W, nb), lambda h: (0, h, 0, 0)), st_spec],
        out_shape=[jax.ShapeDtypeStruct((ls, HEADS, HEAD_W, nb), F32),
                   jax.ShapeDtypeStruct(s0_cm.shape, F32)],
        scratch_shapes=scratch,
        compiler_params=_params("parallel"),
        name="decode_rec_" + kind,
    )(*args, s0_cm)


def _bd_to_state(bd):
    b = bd.shape[0]
    r = bd.reshape(b, HEADS, HEAD_W, HEADS, HEAD_W)
    st = jnp.stack([r[:, h, :, h, :] for h in range(HEADS)], axis=1)
    return jnp.swapaxes(st, 2, 3)


def _sb_prompt_kernel(q_ref, k_ref, v_ref, o_ref, qs_ref, carry_ref, acc_ref, *, tq, tk):
    qi = pl.program_id(1)
    head = _lane_head()
    q = q_ref[...].astype(F32) * (HEAD_W ** -0.5)
    for h in range(HEADS):
        qs_ref[h * tq:(h + 1) * tq, :] = jnp.where(head == h, q, 0.0).astype(BF16)
    ri = lax.broadcasted_iota(jnp.int32, (tk, tk), 0)
    ci = lax.broadcasted_iota(jnp.int32, (tk, tk), 1)
    later16 = jnp.where(ri > ci, 1.0, 0.0).astype(BF16)
    reps = tk // LANES

    def block(kb, mask):
        start = pl.multiple_of(kb * tk, tk)
        z = _dot_nt(qs_ref[...], k_ref[pl.ds(start, tk), :])
        ls = _log_sigmoid(z)
        u = ls - z
        if mask is not None:
            u = jnp.where(mask, u, 0.0)
        between = jnp.concatenate([carry_ref[...]] * reps, axis=1) + _split_dot(u, later16)
        w = jnp.exp(ls + between)
        if mask is not None:
            w = jnp.where(mask, w, 0.0)
        acc_ref[...] += _dot(w.astype(BF16), v_ref[pl.ds(start, tk), :])
        carry = carry_ref[...] + jnp.sum(u, axis=1, keepdims=True)
        carry_ref[...] = carry
        return jnp.max(carry)

    acc_ref[...] = jnp.zeros_like(acc_ref)
    carry_ref[...] = jnp.zeros_like(carry_ref)
    qpos = qi * tq + lax.broadcasted_iota(jnp.int32, (HEADS * tq, 1), 0) % tq
    diag = qi * tq // tk
    kpos = diag * tk + lax.broadcasted_iota(jnp.int32, (1, tk), 1)
    alive = block(diag, kpos < qpos)

    def cond(state):
        return jnp.logical_and(state[0] >= 0, state[1] > SB_DEAD_LOG)

    def body(state):
        return state[0] - 1, block(state[0], None)

    lax.while_loop(cond, body, (diag - 1, alive))
    acc = acc_ref[...]
    o = jnp.zeros((tq, MIX_W), F32)
    for h in range(HEADS):
        o = jnp.where(head == h, acc[h * tq:(h + 1) * tq, :], o)
    o_ref[...] = o


def _sb_prompt(proj32, kv16, nseq, seqlen):
    tq = tk = 256
    nq = seqlen // tq
    return pl.pallas_call(
        functools.partial(_sb_prompt_kernel, tq=tq, tk=tk),
        grid=(nseq, nq),
        in_specs=[pl.BlockSpec((tq, MIX_W), lambda b, i: (b * nq + i, S_SQ)),
                  pl.BlockSpec((seqlen, MIX_W), lambda b, i: (b, KV_SLABS.index(S_SK))),
                  pl.BlockSpec((seqlen, MIX_W), lambda b, i: (b, KV_SLABS.index(S_SV)))],
        out_specs=pl.BlockSpec((tq, MIX_W), lambda b, i: (b * nq + i, 0)),
        out_shape=jax.ShapeDtypeStruct((nseq * seqlen, MIX_W), F32),
        scratch_shapes=[pltpu.VMEM((HEADS * tq, MIX_W), BF16),
                        pltpu.VMEM((HEADS * tq, LANES), F32),
                        pltpu.VMEM((HEADS * tq, MIX_W), F32)],
        compiler_params=_params("parallel", "arbitrary"),
        name="sb_prompt",
    )(proj32, kv16, kv16)


def _diff_lambda(lam_ref, layer):
    lp = lam_ref[...]
    lam_init = 0.8 - 0.6 * math.exp(-0.3 * layer)
    lam = (jnp.exp(jnp.sum(lp[0:1, :] * lp[1:2, :], axis=1, keepdims=True))
           - jnp.exp(jnp.sum(lp[2:3, :] * lp[3:4, :], axis=1, keepdims=True)) + lam_init)
    return lam, lam_init


def _combo_masks():
    lane = lax.broadcasted_iota(jnp.int32, (1, MIX_W), 1)
    return [(lane // DIFF_DH) == (2 * (c // 2) + (c % 2)) for c in range(2 * HEADS)]


def _diff_prompt_kernel(q_ref, k_ref, v_ref, lam_ref, gain_ref, o_ref, qs_ref, m_ref, acc_ref,
                        *, tq, tk, layer):
    qi = pl.program_id(1)
    head = _lane_head()
    q = q_ref[...] * (DIFF_DH ** -0.5 * LOG2_E)
    for c, mk in enumerate(_combo_masks()):
        qs_ref[c * tq:(c + 1) * tq, :] = jnp.where(mk, q, 0.0).astype(BF16)
    m_ref[...] = jnp.full(m_ref.shape, MASK_VALUE, F32)
    acc_ref[...] = jnp.zeros_like(acc_ref)
    qpos = qi * tq + lax.broadcasted_iota(jnp.int32, (2 * tq, 1), 0) % tq
    reps = tk // LANES

    def block(kb, masked):
        start = pl.multiple_of(kb * tk, tk)
        kblk = k_ref[pl.ds(start, tk), :]
        vblk = v_ref[pl.ds(start, tk), :]
        if masked:
            mask = (start + lax.broadcasted_iota(jnp.int32, (1, tk), 1)) <= qpos
        for h in range(HEADS):
            rows = slice(2 * h * tq, (2 * h + 2) * tq)
            s = _dot_nt(qs_ref[rows, :], kblk)
            if masked:
                s = jnp.where(mask, s, MASK_VALUE)
            m_old = m_ref[rows, :]
            m_new = jnp.maximum(m_old, jnp.max(s, axis=1, keepdims=True))
            m_ref[rows, :] = m_new
            p = jnp.exp2(s - jnp.concatenate([m_new] * reps, axis=1)).astype(BF16)
            vaug = jnp.where(head == h, vblk, jnp.ones_like(vblk))
            alpha = jnp.exp2(m_old - m_new)
            acc_ref[rows, :] = (jnp.concatenate([alpha] * (MIX_W // LANES), axis=1) * acc_ref[rows, :]
                                + _dot(p, vaug))

    def unmasked(kb, carry):
        block(kb, False)
        return carry

    diag = qi * tq // tk
    lax.fori_loop(0, diag, unmasked, 0)
    block(diag, True)

    lam, lam_init = _diff_lambda(lam_ref, layer)
    halves = []
    for half in range(MIX_W // LANES):
        mine = slice(half * LANES, (half + 1) * LANES)
        other = slice((1 - half) * LANES, (2 - half) * LANES)
        o = jnp.zeros((tq, LANES), F32)
        for h in range(half * 2, half * 2 + 2):
            r1 = slice(2 * h * tq, (2 * h + 1) * tq)
            r2 = slice((2 * h + 1) * tq, (2 * h + 2) * tq)
            o1 = acc_ref[r1, mine] / acc_ref[r1, other]
            o2 = acc_ref[r2, mine] / acc_ref[r2, other]
            o = jnp.where(head[:, mine] == h, o1 - lam * o2, o)
        halves.append(o)
    o = jnp.concatenate(halves, axis=1)
    o_ref[...] = _head_rms(o, gain_ref[...], _block_diag16()) * (1.0 - lam_init)


def _diff_prompt(proj32, kv16, nseq, seqlen, lam_p, gain, layer):
    tq = tk = 512
    nq = seqlen // tq
    return pl.pallas_call(
        functools.partial(_diff_prompt_kernel, tq=tq, tk=tk, layer=layer),
        grid=(nseq, nq),
        in_specs=[pl.BlockSpec((tq, MIX_W), lambda b, i: (b * nq + i, S_DQ)),
                  pl.BlockSpec((seqlen, MIX_W), lambda b, i: (b, KV_SLABS.index(S_DK))),
                  pl.BlockSpec((seqlen, MIX_W), lambda b, i: (b, KV_SLABS.index(S_DV))),
                  pl.BlockSpec(lam_p.shape, lambda b, i: (0, 0)),
                  pl.BlockSpec((1, MIX_W), lambda b, i: (0, 0))],
        out_specs=pl.BlockSpec((tq, MIX_W), lambda b, i: (b * nq + i, 0)),
        out_shape=jax.ShapeDtypeStruct((nseq * seqlen, MIX_W), F32),
        scratch_shapes=[pltpu.VMEM((2 * HEADS * tq, MIX_W), BF16),
                        pltpu.VMEM((2 * HEADS * tq, LANES), F32),
                        pltpu.VMEM((2 * HEADS * tq, MIX_W), F32)],
        compiler_params=_params("parallel", "arbitrary"),
        name="diff_prompt",
    )(proj32, kv16, kv16, lam_p, gain)


def _stack_rows(x, masks):
    return jnp.concatenate([jnp.where(mk, x, 0.0) for mk in masks], axis=0)


def _sample_attn_kernel(pt_ref, q_ref, sk_ref, sv_ref, dq_ref, dk_ref, dv_ref, *rest,
                        ls, n_pages, layer):
    del pt_ref
    pages, (lam_ref, gain_ref, osb_ref, odf_ref, sb_acc, sb_carry) = rest[:4 * n_pages], rest[4 * n_pages:]
    psk, psv, pdk, pdv = (pages[i * n_pages:(i + 1) * n_pages] for i in range(4))
    head = _lane_head()
    sb_masks = [head == h for h in range(HEADS)]
    qs = _stack_rows(q_ref[...] * (HEAD_W ** -0.5), sb_masks).astype(BF16)
    qd = _stack_rows(dq_ref[...] * (DIFF_DH ** -0.5), _combo_masks()).astype(BF16)

    def pad_page(x):
        return jnp.concatenate([x, jnp.zeros((PAGE_SIZE - ls, MIX_W), F32)], axis=0).astype(BF16)

    def later16(n):
        ri = lax.broadcasted_iota(jnp.int32, (n, n), 0)
        ci = lax.broadcasted_iota(jnp.int32, (n, n), 1)
        return jnp.where(ri > ci, 1.0, 0.0).astype(BF16)

    def sb_block(z, pv_fn, mask):
        lsg = _log_sigmoid(z)
        u = lsg - z
        if mask is not None:
            u = jnp.where(mask, u, 0.0)
        between = sb_carry[...] + _split_dot(u, later16(z.shape[1]))
        w = jnp.exp(lsg + between)
        if mask is not None:
            w = jnp.where(mask, w, 0.0)
        sb_acc[...] += pv_fn(w.astype(BF16))
        sb_carry[...] += jnp.sum(u, axis=1, keepdims=True)

    sb_acc[...] = jnp.zeros_like(sb_acc)
    sb_carry[...] = jnp.zeros_like(sb_carry)
    key = lax.broadcasted_iota(jnp.int32, (1, PAGE_SIZE), 1)
    t_sb = lax.broadcasted_iota(jnp.int32, (HEADS * ls, 1), 0) % ls
    sv_new = pad_page(sv_ref[...])
    sb_block(_dot_nt(qs, pad_page(sk_ref[...])), lambda w: _dot(w, sv_new), key < t_sb)

    def sb_past(blk):
        hi = n_pages - 1 - 2 * blk
        kt = jnp.concatenate([psk[hi - 1][...], psk[hi][...]], axis=1).astype(BF16)
        vt = jnp.concatenate([psv[hi - 1][...], psv[hi][...]], axis=1).astype(BF16)
        sb_block(_dot(qs, kt), lambda w: _dot_nt(w, vt), None)

    def sb_rest(blk):
        if blk < n_pages // 2:
            @pl.when(jnp.max(sb_carry[...]) > SB_DEAD_LOG)
            def _():
                sb_past(blk)
                sb_rest(blk + 1)

    sb_past(0)

    t_df = lax.broadcasted_iota(jnp.int32, (2 * HEADS * ls, 1), 0) % ls
    s_new = jnp.where(key <= t_df, _dot_nt(qd, pad_page(dk_ref[...])), MASK_VALUE)
    kt_all = jnp.concatenate([r[...].astype(BF16) for r in pdk], axis=1)
    s_past = _dot(qd, kt_all)
    m = jnp.maximum(jnp.max(s_new, axis=1, keepdims=True), jnp.max(s_past, axis=1, keepdims=True))
    p_new = jnp.exp(s_new - m)
    p_past = jnp.exp(s_past - m)
    l = jnp.sum(p_new, axis=1, keepdims=True) + jnp.sum(p_past, axis=1, keepdims=True)
    vt_all = jnp.concatenate([r[...].astype(BF16) for r in pdv], axis=1)
    on = (_dot(p_new.astype(BF16), pad_page(dv_ref[...]))
          + _dot_nt(p_past.astype(BF16), vt_all)) / l
    lam, lam_init = _diff_lambda(lam_ref, layer)
    o = jnp.zeros((ls, MIX_W), F32)
    for h in range(HEADS):
        o1 = on[(2 * h) * ls:(2 * h + 1) * ls, :]
        o2 = on[(2 * h + 1) * ls:(2 * h + 2) * ls, :]
        o = jnp.where(head == h, o1 - lam * o2, o)
    odf_ref[...] = _head_rms(o, gain_ref[...], _block_diag16()) * (1.0 - lam_init)

    sb_rest(1)
    acc = sb_acc[...]
    o = jnp.zeros((ls, MIX_W), F32)
    for h in range(HEADS):
        o = jnp.where(head == h, acc[h * ls:(h + 1) * ls, :], o)
    osb_ref[...] = o


def _sample_attn(proj32, page_table, caches, nseq, ls, lam_p, gain, layer):
    n_pages = page_table.shape[1]
    assert n_pages % 2 == 0

    def slab(c):
        return pl.BlockSpec((ls, MIX_W), lambda b, pt: (b, c))

    def page(p):
        return pl.BlockSpec((None, None, MIX_W, PAGE_SIZE), lambda b, pt: (layer, pt[b, p], 0, 0))

    out = pl.BlockSpec((ls, MIX_W), lambda b, pt: (b, 0))
    grid_spec = pltpu.PrefetchScalarGridSpec(
        num_scalar_prefetch=1,
        grid=(nseq,),
        in_specs=[slab(S_SQ), slab(S_SK), slab(S_SV), slab(S_DQ), slab(S_DK), slab(S_DV)]
                 + [page(p) for _ in range(4) for p in range(n_pages)]
                 + [pl.BlockSpec(lam_p.shape, lambda b, pt: (0, 0)),
                    pl.BlockSpec((1, MIX_W), lambda b, pt: (0, 0))],
        out_specs=[out, out],
        scratch_shapes=[pltpu.VMEM((HEADS * ls, MIX_W), F32), pltpu.VMEM((HEADS * ls, 1), F32)])
    page_args = [c for c in caches for _ in range(n_pages)]
    return pl.pallas_call(
        functools.partial(_sample_attn_kernel, ls=ls, n_pages=n_pages, layer=layer),
        grid_spec=grid_spec,
        out_shape=[jax.ShapeDtypeStruct((nseq * ls, MIX_W), F32)] * 2,
        compiler_params=_params("arbitrary"),
        name="sample_attn",
    )(page_table, *([proj32] * 6), *page_args, lam_p, gain)


def _merge_kernel(x_ref, b0_ref, b1_ref, b2_ref, b3_ref, pre_ref, wg_ref, wbr_ref, wo_ref,
                  post_ref, o_ref):
    x = x_ref[...]
    h = _rms(x, pre_ref[...]).astype(BF16)
    merged = None
    for i, b_ref in enumerate((b0_ref, b1_ref, b2_ref, b3_ref)):
        gate = jax.nn.sigmoid(_dot(h, wg_ref[:, i * D_MODEL:(i + 1) * D_MODEL]))
        term = gate * _dot(b_ref[...].astype(BF16), wbr_ref[i])
        merged = term if merged is None else merged + term
    o_ref[...] = x + _rms(_dot(merged.astype(BF16), wo_ref[...]), post_ref[...])


def _const_spec(shape):
    nd = len(shape)
    return pl.BlockSpec(shape, lambda i: (0,) * nd, pipeline_mode=pl.Buffered(1))


def _merge(x, branches, pre, wg16, wbr16, wo16, post):
    n = x.shape[0]
    tm = _row_tile(n, 256)
    tok = pl.BlockSpec((tm, D_MODEL), lambda i: (i, 0))
    br = pl.BlockSpec((tm, MIX_W), lambda i: (i, 0))
    return pl.pallas_call(
        _merge_kernel,
        grid=(n // tm,),
        in_specs=[tok, br, br, br, br, _const_spec((1, D_MODEL)), _const_spec(wg16.shape),
                  _const_spec(wbr16.shape), _const_spec(wo16.shape), _const_spec((1, D_MODEL))],
        out_specs=tok,
        out_shape=jax.ShapeDtypeStruct((n, D_MODEL), F32),
        compiler_params=_params("parallel"),
        name="merge",
    )(x, *branches, pre, wg16, wbr16, wo16, post)


def _cross_attn_kernel(q_ref, mk_ref, mv_ref, o_ref, *, channel_major):
    head = _lane_head()
    q = q_ref[...] * (HEAD_W ** -0.5)
    mk = mk_ref[...].astype(BF16)
    mv = mv_ref[...].astype(BF16)
    o = jnp.zeros(q.shape, F32)
    for h in range(HEADS):
        qh = jnp.where(head == h, q, 0.0).astype(BF16)
        s = _dot(qh, mk) if channel_major else _dot_nt(qh, mk)
        p = jnp.exp(s - jnp.max(s, axis=1, keepdims=True))
        p16 = p.astype(BF16)
        pv = _dot_nt(p16, mv) if channel_major else _dot(p16, mv)
        o = jnp.where(head == h, pv / jnp.sum(p, axis=1, keepdims=True), o)
    o_ref[...] = o


def _cross_attn(cq, mk, mv, nseq, seqlen, channel_major):
    tm = _row_tile(seqlen, 512)
    nt = seqlen // tm
    mem = pl.BlockSpec((None,) + mk.shape[1:], lambda b, i: (b, 0, 0))
    tok = pl.BlockSpec((tm, MIX_W), lambda b, i: (b * nt + i, 0))
    return pl.pallas_call(
        functools.partial(_cross_attn_kernel, channel_major=channel_major),
        grid=(nseq, nt),
        in_specs=[tok, mem, mem],
        out_specs=tok,
        out_shape=jax.ShapeDtypeStruct(cq.shape, F32),
        compiler_params=_params("parallel", "parallel"),
        name="cross_attn",
    )(cq, mk, mv)


def _post_kernel(x_ref, co_ref, wco_ref, cpost_ref, mpre_ref, wup_ref, wdn_ref, mpost_ref, o_ref):
    x = x_ref[...] + _rms(_dot(co_ref[...].astype(BF16), wco_ref[...]), cpost_ref[...])
    h = _rms(x, mpre_ref[...]).astype(BF16)
    u = jnp.square(jnp.maximum(_dot(h, wup_ref[...]), 0.0))
    o_ref[...] = x + _rms(_dot(u.astype(BF16), wdn_ref[...]), mpost_ref[...])


def _post(x, co, wco16, cpost, mpre, wup16, wdn16, mpost):
    n = x.shape[0]
    tm = _row_tile(n, 256)
    tok = pl.BlockSpec((tm, D_MODEL), lambda i: (i, 0))
    return pl.pallas_call(
        _post_kernel,
        grid=(n // tm,),
        in_specs=[tok, pl.BlockSpec((tm, MIX_W), lambda i: (i, 0)), _const_spec(wco16.shape),
                  _const_spec((1, D_MODEL)), _const_spec((1, D_MODEL)), _const_spec(wup16.shape),
                  _const_spec(wdn16.shape), _const_spec((1, D_MODEL))],
        out_specs=tok,
        out_shape=jax.ShapeDtypeStruct((n, D_MODEL), F32),
        compiler_params=_params("parallel"),
        name="ca_out_mlp",
    )(x, co, wco16, cpost, mpre, wup16, wdn16, mpost)


def _trunk_layer(x, nseq, seqlen, rope_tab, mem_k, mem_v, mem_channel_major, w, rec_fn, attn_fn,
                 emit_kv):
    proj32, *kv = _mixer_proj(x, w["mix_pre_norm"], w["w_in"], rope_tab,
                              (nseq, seqlen) if emit_kv else None)
    o_ret, st_ret, o_hgrn, st_hgrn, new_kv = rec_fn(proj32, kv)
    o_sb, o_diff = attn_fn(proj32, kv)
    x = _merge(x, (o_ret, o_sb, o_diff, o_hgrn), w["mix_pre_norm"], w["w_gate"], w["w_br"],
               w["w_out"], w["mix_post_norm"])
    cq = _norm_proj(x, w["ca_pre_norm"], w["w_ca_q"])
    co = _cross_attn(cq, mem_k, mem_v, nseq, seqlen, mem_channel_major)
    x = _post(x, co, w["w_ca_o"], w["ca_post_norm"], w["mlp_pre_norm"], w["w_mlp_up"],
              w["w_mlp_down"], w["mlp_post_norm"])
    return x, new_kv, st_ret, st_hgrn


def kernel(x_prompt, x_sample, mem_prompt, cache_sb_k, cache_sb_v, cache_diff_k, cache_diff_v, cache_mem_k, cache_mem_v, state_ret, state_hgrn, page_table, mix_pre_norm, mix_post_norm, ca_pre_norm, ca_post_norm, mlp_pre_norm, mlp_post_norm, mem_norm, w_in, w_gate, ret_norm, diff_lambda, diff_norm, hgrn_lb_logits, hgrn_norm, w_br_ret, w_br_sb, w_br_diff, w_br_hgrn, w_out, w_ca_q, w_ca_k, w_ca_v, w_ca_o, w_mlp_up, w_mlp_down):
    depth = w_in.shape[0]
    B, L, _ = x_prompt.shape
    DB, LS, _ = x_sample.shape
    n_pages = page_table.shape[1]
    past_len = n_pages * PAGE_SIZE

    tab_p = _rope_table(jnp.arange(L, dtype=jnp.int32))
    tab_s = jnp.tile(_rope_table(past_len + jnp.arange(LS, dtype=jnp.int32)), (256 // LS, 1))
    caches = [jnp.transpose(c, (0, 1, 3, 4, 2)).reshape(c.shape[0], c.shape[1], MIX_W, PAGE_SIZE)
              for c in (cache_sb_k, cache_sb_v, cache_diff_k, cache_diff_v)]
    mem_cache = [jnp.transpose(c, (0, 1, 3, 4, 2)).reshape(c.shape[0], c.shape[1], MIX_W, N_MEM)
                 for c in (cache_mem_k, cache_mem_v)]
    w_br = jnp.stack([w_br_ret, w_br_sb, w_br_diff, w_br_hgrn], axis=1).astype(BF16)
    w_ca_kv = jnp.concatenate([w_ca_k, w_ca_v], axis=-1).astype(BF16)
    diff_gain = jnp.tile(diff_norm, (1, HEADS))
    def lanes_cm(a):
        return jnp.broadcast_to(a.reshape(depth, HEADS, HEAD_W, 1), (depth, HEADS, HEAD_W, DB))
    ret_gain_cm, hgrn_gain_cm, lbl_cm = lanes_cm(ret_norm), lanes_cm(hgrn_norm), lanes_cm(hgrn_lb_logits)
    st_ret_cm = state_ret.transpose(0, 2, 3, 4, 1)
    st_hgrn_cm = state_hgrn.transpose(0, 2, 3, 4, 1)

    xp = x_prompt.reshape(B * L, D_MODEL)
    xs = x_sample.reshape(DB * LS, D_MODEL)
    mem = mem_prompt.reshape(B * N_MEM, D_MODEL)
    zero_state = jnp.zeros((B, MIX_W, MIX_W), F32)
    outs_p, outs_s = [], []
    for l in range(depth):
        row = lambda a: a[l][None, :]
        w = {
            "mix_pre_norm": row(mix_pre_norm), "mix_post_norm": row(mix_post_norm),
            "ca_pre_norm": row(ca_pre_norm), "ca_post_norm": row(ca_post_norm),
            "mlp_pre_norm": row(mlp_pre_norm), "mlp_post_norm": row(mlp_post_norm),
            "ret_norm": row(ret_norm), "hgrn_norm": row(hgrn_norm),
            "hgrn_lb_logits": hgrn_lb_logits,
            "w_in": w_in[l].astype(BF16), "w_gate": w_gate[l].astype(BF16), "w_br": w_br[l],
            "w_out": w_out[l].astype(BF16), "w_ca_q": w_ca_q[l].astype(BF16),
            "w_ca_o": w_ca_o[l].astype(BF16), "w_mlp_up": w_mlp_up[l].astype(BF16),
            "w_mlp_down": w_mlp_down[l].astype(BF16),
        }
        lam_p, dgain = diff_lambda[l], diff_gain[l][None, :]

        mkv = _norm_proj(mem, row(mem_norm), w_ca_kv[l])
        mk_p = mkv[:, :MIX_W].reshape(B, N_MEM, MIX_W)
        mv_p = mkv[:, MIX_W:].reshape(B, N_MEM, MIX_W)

        def rec_p(proj32, kv, l=l, w=w):
            o_ret, st_ret = _recurrence("ret", proj32, B, L, 256, w["ret_norm"], zero_state)
            o_hgrn, st_hgrn = _recurrence("hgrn", proj32, B, L, 32, w["hgrn_norm"], zero_state,
                                          hgrn_lb_logits, l, nsub=8)
            kv_cm = kv[1]
            return (o_ret, _bd_to_state(st_ret), o_hgrn, _bd_to_state(st_hgrn),
                    [kv_cm[i] for i in range(len(KV_SLABS))])

        def attn_p(proj32, kv, l=l, lam_p=lam_p, dgain=dgain):
            return (_sb_prompt(proj32, kv[0], B, L),
                    _diff_prompt(proj32, kv[0], B, L, lam_p, dgain, l))

        xp, kv_p, ret_p, hgrn_p = _trunk_layer(xp, B, L, tab_p, mk_p, mv_p, False, w, rec_p, attn_p,
                                               True)
        outs_p.append((kv_p, ret_p, hgrn_p, mk_p, mv_p))

        def rec_s(proj32, kv, l=l):
            del kv
            pcm = proj32.reshape(DB, LS, N_SLABS, HEADS, HEAD_W).transpose(2, 1, 3, 4, 0)
            o_ret, st_ret = _decode_recurrence("ret", pcm, ret_gain_cm[l], st_ret_cm[l])
            o_hgrn, st_hgrn = _decode_recurrence("hgrn", pcm, hgrn_gain_cm[l], st_hgrn_cm[l],
                                                 lbl_cm, l)
            rows = lambda o: o.transpose(3, 0, 1, 2).reshape(DB * LS, MIX_W)
            return (rows(o_ret), st_ret.transpose(3, 0, 1, 2),
                    rows(o_hgrn), st_hgrn.transpose(3, 0, 1, 2), [pcm[c] for c in KV_SLABS])

        def attn_s(proj32, kv, l=l, lam_p=lam_p, dgain=dgain):
            del kv
            return _sample_attn(proj32, page_table, caches, DB, LS, lam_p, dgain, l)

        xs, kv_s, ret_s, hgrn_s = _trunk_layer(xs, DB, LS, tab_s, mem_cache[0][l], mem_cache[1][l],
                                               True, w, rec_s, attn_s, False)
        outs_s.append((kv_s, ret_s, hgrn_s))

    def kv_prompt(i):
        a = jnp.stack([o[0][i] for o in outs_p], axis=0)
        return a.reshape(depth, B, HEADS, HEAD_W, L).transpose(0, 1, 4, 2, 3)

    def kv_sample(i):
        return jnp.stack([o[0][i] for o in outs_s], axis=0).transpose(0, 4, 1, 2, 3)

    def states(outs, i):
        return jnp.stack([o[i] for o in outs], axis=0)

    def memkv(i):
        return jnp.stack([o[i].reshape(B, N_MEM, HEADS, HEAD_W) for o in outs_p], axis=0)

    return (xp.reshape(B, L, D_MODEL), xs.reshape(DB, LS, D_MODEL),
            kv_prompt(0), kv_prompt(1), kv_prompt(2), kv_prompt(3),
            memkv(3), memkv(4), states(outs_p, 1), states(outs_p, 2),
            kv_sample(0), kv_sample(1), kv_sample(2), kv_sample(3),
            states(outs_s, 1), states(outs_s, 2))
```

```python
import functools
import math

import jax
import jax.numpy as jnp
import numpy as np
from jax import lax
from jax.experimental import pallas as pl
from jax.experimental.pallas import tpu as pltpu

F32 = jnp.float32
BF16 = jnp.bfloat16

D_MODEL = 1024
N_MEM = 256
HEADS = 4
HEAD_W = 64
MIX_W = HEADS * HEAD_W
DIFF_DH = 32
D_FF = 4 * D_MODEL
N_BRANCH = 4
PAGE_SIZE = 128
ROPE_THETA = 10000.0
NORM_EPS = 1e-6
MASK_VALUE = -1e30
LOG2_E = math.log2(math.e)
LANES = 128
SUBLANES = 8
assert MIX_W == 2 * LANES and 2 * HEAD_W == LANES
N_SLABS = 14
D_IN = N_SLABS * MIX_W
(S_RQ, S_RK, S_RV, S_RG, S_SQ, S_SK, S_SV, S_DQ, S_DK, S_DV,
 S_GQ, S_GF, S_GI, S_GG) = range(N_SLABS)
KV_SLABS = (S_SK, S_SV, S_DK, S_DV)

SB_DEAD_LOG = -104.0

V7X_VMEM_BYTES = 64 * 1024 * 1024
VMEM_LIMIT = (V7X_VMEM_BYTES * 7) // 8

_NT = (((1,), (1,)), ((), ()))
_TN = (((0,), (0,)), ((), ()))


def _params(*sem):
    return pltpu.CompilerParams(dimension_semantics=sem, vmem_limit_bytes=VMEM_LIMIT)


def _dot(a, b):
    return jnp.dot(a, b, preferred_element_type=F32)


def _dot_nt(a, b):
    return lax.dot_general(a, b, _NT, preferred_element_type=F32)


def _dot_tn(a, b):
    return lax.dot_general(a, b, _TN, preferred_element_type=F32)


def _split_dot(x, m16, terms=2, left=False):
    out = None
    r = x
    for _ in range(terms):
        p = r.astype(BF16)
        d = _dot(m16, p) if left else _dot(p, m16)
        out = d if out is None else out + d
        r = r - p.astype(F32)
    return out


def _rms(x, g):
    ms = jnp.mean(x * x, axis=-1, keepdims=True)
    return x * lax.rsqrt(ms + NORM_EPS) * g


def _lane_head(width=MIX_W):
    return lax.broadcasted_iota(jnp.int32, (1, width), 1) // HEAD_W


def _block_diag16():
    r = lax.broadcasted_iota(jnp.int32, (MIX_W, MIX_W), 0) // HEAD_W
    c = lax.broadcasted_iota(jnp.int32, (MIX_W, MIX_W), 1) // HEAD_W
    return jnp.where(r == c, 1.0, 0.0).astype(BF16)


def _head_rms(o, gain, bd16):
    ms = _split_dot(o * o, bd16) * (1.0 / HEAD_W)
    return o * lax.rsqrt(ms + NORM_EPS) * gain


def _log_sigmoid(z):
    return jnp.minimum(z, 0.0) - jnp.log(1.0 + jnp.exp(-jnp.abs(z)))


def _rope_slab(s, cos, sin, half):
    lane = lax.broadcasted_iota(jnp.int32, (1, MIX_W), 1)
    first = (lane % (2 * half)) < half
    partner = jnp.where(first, pltpu.roll(s, MIX_W - half, 1), pltpu.roll(s, half, 1))
    return s * cos + partner * sin


def _norm_proj_kernel(x_ref, g_ref, w_ref, o32_ref):
    h = _rms(x_ref[...], g_ref[...]).astype(BF16)
    o32_ref[...] = _dot(h, w_ref[...])


def _mixer_proj_kernel(x_ref, g_ref, w_ref, tab_ref, o32_ref, *kv_refs):
    h = _rms(x_ref[...], g_ref[...]).astype(BF16)
    y = _dot(h, w_ref[...])
    cos_r = tab_ref[:, 0 * MIX_W:1 * MIX_W]
    sin_r = tab_ref[:, 1 * MIX_W:2 * MIX_W]
    cos_d = tab_ref[:, 2 * MIX_W:3 * MIX_W]
    sin_d = tab_ref[:, 3 * MIX_W:4 * MIX_W]
    for c in range(N_SLABS):
        s = y[:, c * MIX_W:(c + 1) * MIX_W]
        if c == S_RQ:
            s = _rope_slab(s, cos_r, sin_r, HEAD_W // 2)
        elif c == S_RK:
            s = _rope_slab(s, cos_r, sin_r, HEAD_W // 2) * (HEAD_W ** -0.5)
        elif c in (S_DQ, S_DK):
            s = _rope_slab(s, cos_d, sin_d, DIFF_DH // 2)
        o32_ref[:, c * MIX_W:(c + 1) * MIX_W] = s
        if kv_refs and c in KV_SLABS:
            i = KV_SLABS.index(c)
            kv_refs[0][:, i * MIX_W:(i + 1) * MIX_W] = s.astype(BF16)
            kv_refs[1 + i][...] = s.T


def _row_tile(n, want):
    t = math.gcd(n, want)
    assert t % 8 == 0
    return t


def _norm_proj(x, gain, w16):
    n, d = x.shape
    wout = w16.shape[1]
    tm = _row_tile(n, 256)
    return pl.pallas_call(
        _norm_proj_kernel,
        grid=(n // tm,),
        in_specs=[pl.BlockSpec((tm, d), lambda i: (i, 0)),
                  pl.BlockSpec((1, d), lambda i: (0, 0)),
                  pl.BlockSpec((d, wout), lambda i: (0, 0))],
        out_specs=pl.BlockSpec((tm, wout), lambda i: (i, 0)),
        out_shape=jax.ShapeDtypeStruct((n, wout), F32),
        compiler_params=_params("parallel"),
        name="norm_proj",
    )(x, gain, w16)


def _mixer_proj(x, gain, w16, tab, kv_seqs=None):
    n, d = x.shape
    tm = _row_tile(n, 256)
    tab_blocks = tab.shape[0] // tm
    out_specs = [pl.BlockSpec((tm, D_IN), lambda i: (i, 0))]
    out_shape = [jax.ShapeDtypeStruct((n, D_IN), F32)]
    if kv_seqs is not None:
        nseq, seqlen = kv_seqs
        tiles = seqlen // tm
        nkv = len(KV_SLABS)
        out_specs += [pl.BlockSpec((tm, nkv * MIX_W), lambda i: (i, 0))]
        out_specs += [pl.BlockSpec((None, MIX_W, tm), lambda i: (i // tiles, 0, i % tiles))] * nkv
        out_shape += [jax.ShapeDtypeStruct((n, nkv * MIX_W), BF16)]
        out_shape += [jax.ShapeDtypeStruct((nseq, MIX_W, seqlen), F32)] * nkv
    return pl.pallas_call(
        _mixer_proj_kernel,
        grid=(n // tm,),
        in_specs=[pl.BlockSpec((tm, d), lambda i: (i, 0)),
                  pl.BlockSpec((1, d), lambda i: (0, 0)),
                  pl.BlockSpec((d, D_IN), lambda i: (0, 0)),
                  pl.BlockSpec((tm, 4 * MIX_W), lambda i: (i % tab_blocks, 0))],
        out_specs=out_specs,
        out_shape=out_shape,
        compiler_params=_params("parallel"),
        name="mixer_proj",
    )(x, gain, w16, tab)


def _rope_table(pos):
    def one(group):
        half = group // 2
        inv = ROPE_THETA ** (-jnp.arange(half, dtype=F32) * 2.0 / group)
        ang = pos.astype(F32)[:, None] * inv[None, :]
        cos, sin = jnp.cos(ang), jnp.sin(ang)
        reps = MIX_W // group
        return (jnp.tile(jnp.concatenate([cos, cos], -1), (1, reps)),
                jnp.tile(jnp.concatenate([-sin, sin], -1), (1, reps)))
    cr, sr = one(HEAD_W)
    cd, sd = one(DIFF_DH)
    return jnp.concatenate([cr, sr, cd, sd], axis=-1)


_RET_GAMMA = [1.0 - 2.0 ** (-5.0 - h) for h in range(HEADS)]
_RET_LOG_GAMMA = [float(np.log(g)) for g in _RET_GAMMA]


def _state_step(st, q_dec, k_dec, v, decay_row):
    inter = _dot_nt(q_dec.astype(BF16), st.astype(BF16))
    upd = _dot_tn(v.astype(BF16), k_dec.astype(BF16))
    r = lax.broadcasted_iota(jnp.int32, (MIX_W, MIX_W), 0) // HEAD_W
    c = lax.broadcasted_iota(jnp.int32, (MIX_W, MIX_W), 1) // HEAD_W
    return inter, st * decay_row + jnp.where(r == c, upd, 0.0)


def _retention_kernel(q_ref, k_ref, v_ref, g_ref, gain_ref, st0_ref, o_ref, st_ref, *, chunk):
    @pl.when(pl.program_id(1) == 0)
    def _():
        st_ref[...] = st0_ref[...]

    q, k, v = q_ref[...], k_ref[...], v_ref[...]
    head = _lane_head()
    lg = jnp.zeros((1, MIX_W), F32)
    for h in range(HEADS):
        lg = jnp.where(head == h, _RET_LOG_GAMMA[h], lg)
    steps = lax.broadcasted_iota(jnp.int32, (chunk, 1), 0).astype(F32) + 1.0
    b = steps * lg
    b_last = float(chunk) * lg
    o, st_new = _state_step(st_ref[0], q * jnp.exp(b), k * jnp.exp(b_last - b), v,
                            jnp.exp(b_last))
    st_ref[0] = st_new

    ri = lax.broadcasted_iota(jnp.int32, (chunk, chunk), 0)
    ci = lax.broadcasted_iota(jnp.int32, (chunk, chunk), 1)
    causal = ri >= ci
    dist = (ri - ci).astype(F32)
    k16, v16 = k.astype(BF16), v.astype(BF16)
    for h in range(HEADS):
        sc = _dot_nt(jnp.where(head == h, q, 0.0).astype(BF16), k16)
        dec = jnp.where(causal, jnp.exp(dist * _RET_LOG_GAMMA[h]), 0.0)
        o = o + jnp.where(head == h, _dot((sc * dec).astype(BF16), v16), 0.0)

    g = g_ref[...]
    o_ref[...] = _head_rms(o, gain_ref[...], _block_diag16()) * (g * jax.nn.sigmoid(g))


def _hgrn_kernel(q_ref, f_ref, v_ref, g_ref, gain_ref, lbl_ref, st0_ref, o_ref, st_ref,
                 kk_scr, b_scr, *, chunk, nsub, layer):
    @pl.when(pl.program_id(1) == 0)
    def _():
        st_ref[...] = st0_ref[...]

    lbl = lbl_ref[...]
    e = jnp.exp(lbl - jnp.max(lbl, axis=0, keepdims=True))
    lb_w = e / jnp.sum(e, axis=0, keepdims=True)
    lb = lb_w[0:1, :]
    for i in range(1, layer + 1):
        lb = lb + lb_w[i:i + 1, :]
    lb = lb - lb_w[0:1, :]

    z_all = f_ref[...]
    kk_scr[...] = (1.0 - lb) * jax.nn.sigmoid(-z_all)
    log2_f = jnp.log(lb + (1.0 - lb) * jax.nn.sigmoid(z_all)) * LOG2_E
    ri = lax.broadcasted_iota(jnp.int32, (chunk, chunk), 0)
    ci = lax.broadcasted_iota(jnp.int32, (chunk, chunk), 1)
    tril16 = jnp.where(ri >= ci, 1.0, 0.0).astype(BF16)
    bd16 = _block_diag16()
    rows8 = lax.broadcasted_iota(jnp.int32, (SUBLANES, 1), 0)
    groups = chunk // SUBLANES
    r_blk = lax.broadcasted_iota(jnp.int32, (MIX_W, MIX_W), 0) // HEAD_W
    c_blk = lax.broadcasted_iota(jnp.int32, (MIX_W, MIX_W), 1) // HEAD_W

    bs, q_decs, updates, decays = [], [], [], []
    for s in range(nsub):
        lo = s * chunk
        q, v, k = q_ref[lo:lo + chunk, :], v_ref[lo:lo + chunk, :], kk_scr[lo:lo + chunk, :]
        lf = log2_f[lo:lo + chunk, :]
        b = _split_dot(lf, tril16, terms=3, left=True)
        b_scr[lo:lo + chunk, :] = b
        b_last = b[chunk - 1:chunk, :]
        bs.append(b)
        q_decs.append((q * jnp.exp2(b)).astype(BF16))
        upd = _dot_tn(v.astype(BF16), (k * jnp.exp2(b_last - b)).astype(BF16))
        updates.append(jnp.where(r_blk == c_blk, upd, 0.0))
        decays.append(jnp.exp2(b_last))
    states = [st_ref[0]]
    for s in range(nsub):
        states.append(states[-1] * decays[s] + updates[s])
    st_ref[0] = states[-1]

    for s in range(nsub):
        lo = s * chunk
        q, b = q_ref[lo:lo + chunk, :], bs[s]
        inter = _dot_nt(q_decs[s], states[s].astype(BF16))
        qg = [q[r * SUBLANES:(r + 1) * SUBLANES, :] for r in range(groups)]
        bg = [b[r * SUBLANES:(r + 1) * SUBLANES, :] for r in range(groups)]
        slabs, where = [], []
        for j in range(chunk):
            kj = kk_scr[lo + j:lo + j + 1, :]
            bj = b_scr[lo + j:lo + j + 1, :]
            for r in range(j // SUBLANES, groups):
                first = r * SUBLANES
                t = qg[r] * kj * jnp.exp2(bg[r] - bj)
                slabs.append(jnp.where(rows8 + first >= j, t, 0.0) if first < j else t)
                where.append((j, r))
        pair_sum = _dot(jnp.concatenate(slabs, axis=0).astype(BF16), bd16)
        og = [inter[r * SUBLANES:(r + 1) * SUBLANES, :] for r in range(groups)]
        for i, (j, r) in enumerate(where):
            og[r] = og[r] + pair_sum[i * SUBLANES:(i + 1) * SUBLANES, :] * v_ref[lo + j:lo + j + 1, :]
        o = jnp.concatenate(og, axis=0) if groups > 1 else og[0]
        g = g_ref[lo:lo + chunk, :]
        o_ref[lo:lo + chunk, :] = _head_rms(o, gain_ref[...], bd16) * (g * jax.nn.sigmoid(g))


def _slab_spec(rows, slab, nchunk):
    return pl.BlockSpec((rows, MIX_W), lambda s, c: (s * nchunk + c, slab))


def _recurrence(kind, proj32, nseq, seqlen, chunk, gain, st0, lb_logits=None, layer=0, nsub=1):
    rows = chunk * nsub
    nstep = seqlen // rows
    n = nseq * seqlen
    row = pl.BlockSpec((1, MIX_W), lambda s, c: (0, 0))
    st_spec = pl.BlockSpec((1, MIX_W, MIX_W), lambda s, c: (s, 0, 0))
    out_specs = [pl.BlockSpec((rows, MIX_W), lambda s, c: (s * nstep + c, 0)), st_spec]
    out_shape = [jax.ShapeDtypeStruct((n, MIX_W), F32),
                 jax.ShapeDtypeStruct((nseq, MIX_W, MIX_W), F32)]
    if kind == "ret":
        assert nsub == 1
        slabs = (S_RQ, S_RK, S_RV, S_RG)
        body = functools.partial(_retention_kernel, chunk=chunk)
        extra_specs, extra, scratch = [row, st_spec], (gain, st0), []
    else:
        slabs = (S_GQ, S_GF, S_GI, S_GG)
        body = functools.partial(_hgrn_kernel, chunk=chunk, nsub=nsub, layer=layer)
        extra_specs = [row, pl.BlockSpec(lb_logits.shape, lambda s, c: (0, 0)), st_spec]
        extra = (gain, lb_logits, st0)
        scratch = [pltpu.VMEM((rows, MIX_W), F32), pltpu.VMEM((rows, MIX_W), F32)]
    return pl.pallas_call(
        body,
        grid=(nseq, nstep),
        in_specs=[_slab_spec(rows, s, nstep) for s in slabs] + extra_specs,
        out_specs=out_specs,
        out_shape=out_shape,
        scratch_shapes=scratch,
        compiler_params=_params("parallel", "arbitrary"),
        name="recurrence_" + kind,
    )(proj32, proj32, proj32, proj32, *extra)


def _decode_rec_kernel(q_ref, k_ref, v_ref, g_ref, gain_ref, *rest, per_channel, ls, layer):
    if per_channel:
        lbl_ref, s0_ref, o_ref, s_ref, f_scr, kk_scr, oacc_scr = rest
        lbl = lbl_ref[...]
        e = jnp.exp(lbl - jnp.max(lbl, axis=0, keepdims=True))
        lb_w = e / jnp.sum(e, axis=0, keepdims=True)
        lb = lb_w[0]
        for i in range(1, layer + 1):
            lb = lb + lb_w[i]
        lb = lb - lb_w[0]
        for t in range(ls):
            z = k_ref[t]
            f_scr[t] = lb + (1.0 - lb) * jax.nn.sigmoid(z)
            kk_scr[t] = (1.0 - lb) * jax.nn.sigmoid(-z)
        key_ref = kk_scr
    else:
        s0_ref, o_ref, s_ref, oacc_scr = rest
        h = pl.program_id(0)
        gamma = jnp.float32(_RET_GAMMA[0])
        for i in range(1, HEADS):
            gamma = jnp.where(h == i, jnp.float32(_RET_GAMMA[i]), gamma)
        key_ref = k_ref
    oacc_scr[...] = jnp.zeros_like(oacc_scr)

    def channel(c, carry):
        s = s0_ref[c]
        for t in range(ls):
            f = f_scr[t, pl.ds(c, 1), :] if per_channel else gamma
            s = f * s + key_ref[t, pl.ds(c, 1), :] * v_ref[t]
            oacc_scr[t] += q_ref[t, pl.ds(c, 1), :] * s
        s_ref[c] = s
        return carry

    lax.fori_loop(0, HEAD_W, channel, 0)
    for t in range(ls):
        o = oacc_scr[t]
        ms = jnp.mean(o * o, axis=0, keepdims=True)
        g = g_ref[t]
        o_ref[t] = o * lax.rsqrt(ms + NORM_EPS) * gain_ref[...] * (g * jax.nn.sigmoid(g))


def _decode_recurrence(kind, proj_cm, gain_cm, s0_cm, lbl_cm=None, layer=0):
    _, ls, _, _, nb = proj_cm.shape
    per_channel = kind == "hgrn"
    slabs = (S_GQ, S_GF, S_GI, S_GG) if per_channel else (S_RQ, S_RK, S_RV, S_RG)

    def act(slab):
        return pl.BlockSpec((None, ls, None, HEAD_W, nb), lambda h: (slab, 0, h, 0, 0))

    st_spec = pl.BlockSpec((None, HEAD_W, HEAD_W, nb), lambda h: (h, 0, 0, 0))
    in_specs = [act(s) for s in slabs] + [pl.BlockSpec((None, HEAD_W, nb), lambda h: (h, 0, 0))]
    args = [proj_cm] * 4 + [gain_cm]
    scratch = [pltpu.VMEM((ls, HEAD_W, nb), F32)]
    if per_channel:
        in_specs.append(pl.BlockSpec((lbl_cm.shape[0], None, HEAD_W, nb), lambda h: (0, h, 0, 0)))
        args.append(lbl_cm)
        scratch = [pltpu.VMEM((ls, HEAD_W, nb), F32)] * 3
    return pl.pallas_call(
        functools.partial(_decode_rec_kernel, per_channel=per_channel, ls=ls, layer=layer),
        grid=(HEADS,),
        in_specs=in_specs + [st_spec],
        out_specs=[pl.BlockSpec((ls, None, HEAD_W, nb), lambda h: (0, h, 0, 0)), st_spec],
        out_shape=[jax.ShapeDtypeStruct((ls, HEADS, HEAD_W, nb), F32),
                   jax.ShapeDtypeStruct(s0_cm.shape, F32)],
        scratch_shapes=scratch,
        compiler_params=_params("parallel"),
        name="decode_rec_" + kind,
    )(*args, s0_cm)


def _bd_to_state(bd):
    b = bd.shape[0]
    r = bd.reshape(b, HEADS, HEAD_W, HEADS, HEAD_W)
    st = jnp.stack([r[:, h, :, h, :] for h in range(HEADS)], axis=1)
    return jnp.swapaxes(st, 2, 3)


def _sb_prompt_kernel(q_ref, k_ref, v_ref, o_ref, qs_ref, carry_ref, acc_ref, *, tq, tk):
    qi = pl.program_id(1)
    head = _lane_head()
    q = q_ref[...].astype(F32) * (HEAD_W ** -0.5)
    for h in range(HEADS):
        qs_ref[h * tq:(h + 1) * tq, :] = jnp.where(head == h, q, 0.0).astype(BF16)
    ri = lax.broadcasted_iota(jnp.int32, (tk, tk), 0)
    ci = lax.broadcasted_iota(jnp.int32, (tk, tk), 1)
    later16 = jnp.where(ri > ci, 1.0, 0.0).astype(BF16)
    reps = tk // LANES

    def block(kb, mask):
        start = pl.multiple_of(kb * tk, tk)
        z = _dot_nt(qs_ref[...], k_ref[pl.ds(start, tk), :])
        ls = _log_sigmoid(z)
        u = ls - z
        if mask is not None:
            u = jnp.where(mask, u, 0.0)
        between = jnp.concatenate([carry_ref[...]] * reps, axis=1) + _split_dot(u, later16)
        w = jnp.exp(ls + between)
        if mask is not None:
            w = jnp.where(mask, w, 0.0)
        acc_ref[...] += _dot(w.astype(BF16), v_ref[pl.ds(start, tk), :])
        carry = carry_ref[...] + jnp.sum(u, axis=1, keepdims=True)
        carry_ref[...] = carry
        return jnp.max(carry)

    acc_ref[...] = jnp.zeros_like(acc_ref)
    carry_ref[...] = jnp.zeros_like(carry_ref)
    qpos = qi * tq + lax.broadcasted_iota(jnp.int32, (HEADS * tq, 1), 0) % tq
    diag = qi * tq // tk
    kpos = diag * tk + lax.broadcasted_iota(jnp.int32, (1, tk), 1)
    alive = block(diag, kpos < qpos)

    def cond(state):
        return jnp.logical_and(state[0] >= 0, state[1] > SB_DEAD_LOG)

    def body(state):
        return state[0] - 1, block(state[0], None)

    lax.while_loop(cond, body, (diag - 1, alive))
    acc = acc_ref[...]
    o = jnp.zeros((tq, MIX_W), F32)
    for h in range(HEADS):
        o = jnp.where(head == h, acc[h * tq:(h + 1) * tq, :], o)
    o_ref[...] = o


def _sb_prompt(proj32, kv16, nseq, seqlen):
    tq = tk = 256
    nq = seqlen // tq
    return pl.pallas_call(
        functools.partial(_sb_prompt_kernel, tq=tq, tk=tk),
        grid=(nseq, nq),
        in_specs=[pl.BlockSpec((tq, MIX_W), lambda b, i: (b * nq + i, S_SQ)),
                  pl.BlockSpec((seqlen, MIX_W), lambda b, i: (b, KV_SLABS.index(S_SK))),
                  pl.BlockSpec((seqlen, MIX_W), lambda b, i: (b, KV_SLABS.index(S_SV)))],
        out_specs=pl.BlockSpec((tq, MIX_W), lambda b, i: (b * nq + i, 0)),
        out_shape=jax.ShapeDtypeStruct((nseq * seqlen, MIX_W), F32),
        scratch_shapes=[pltpu.VMEM((HEADS * tq, MIX_W), BF16),
                        pltpu.VMEM((HEADS * tq, LANES), F32),
                        pltpu.VMEM((HEADS * tq, MIX_W), F32)],
        compiler_params=_params("parallel", "arbitrary"),
        name="sb_prompt",
    )(proj32, kv16, kv16)


def _diff_lambda(lam_ref, layer):
    lp = lam_ref[...]
    lam_init = 0.8 - 0.6 * math.exp(-0.3 * layer)
    lam = (jnp.exp(jnp.sum(lp[0:1, :] * lp[1:2, :], axis=1, keepdims=True))
           - jnp.exp(jnp.sum(lp[2:3, :] * lp[3:4, :], axis=1, keepdims=True)) + lam_init)
    return lam, lam_init


def _combo_masks():
    lane = lax.broadcasted_iota(jnp.int32, (1, MIX_W), 1)
    return [(lane // DIFF_DH) == (2 * (c // 2) + (c % 2)) for c in range(2 * HEADS)]


def _diff_prompt_kernel(q_ref, k_ref, v_ref, lam_ref, gain_ref, o_ref, qs_ref, m_ref, acc_ref,
                        *, tq, tk, layer):
    qi = pl.program_id(1)
    head = _lane_head()
    q = q_ref[...] * (DIFF_DH ** -0.5 * LOG2_E)
    for c, mk in enumerate(_combo_masks()):
        qs_ref[c * tq:(c + 1) * tq, :] = jnp.where(mk, q, 0.0).astype(BF16)
    m_ref[...] = jnp.full(m_ref.shape, MASK_VALUE, F32)
    acc_ref[...] = jnp.zeros_like(acc_ref)
    qpos = qi * tq + lax.broadcasted_iota(jnp.int32, (2 * tq, 1), 0) % tq
    reps = tk // LANES

    def block(kb, masked):
        start = pl.multiple_of(kb * tk, tk)
        kblk = k_ref[pl.ds(start, tk), :]
        vblk = v_ref[pl.ds(start, tk), :]
        if masked:
            mask = (start + lax.broadcasted_iota(jnp.int32, (1, tk), 1)) <= qpos
        for h in range(HEADS):
            rows = slice(2 * h * tq, (2 * h + 2) * tq)
            s = _dot_nt(qs_ref[rows, :], kblk)
            if masked:
                s = jnp.where(mask, s, MASK_VALUE)
            m_old = m_ref[rows, :]
            m_new = jnp.maximum(m_old, jnp.max(s, axis=1, keepdims=True))
            m_ref[rows, :] = m_new
            p = jnp.exp2(s - jnp.concatenate([m_new] * reps, axis=1)).astype(BF16)
            vaug = jnp.where(head == h, vblk, jnp.ones_like(vblk))
            alpha = jnp.exp2(m_old - m_new)
            acc_ref[rows, :] = (jnp.concatenate([alpha] * (MIX_W // LANES), axis=1) * acc_ref[rows, :]
                                + _dot(p, vaug))

    def unmasked(kb, carry):
        block(kb, False)
        return carry

    diag = qi * tq // tk
    lax.fori_loop(0, diag, unmasked, 0)
    block(diag, True)

    lam, lam_init = _diff_lambda(lam_ref, layer)
    halves = []
    for half in range(MIX_W // LANES):
        mine = slice(half * LANES, (half + 1) * LANES)
        other = slice((1 - half) * LANES, (2 - half) * LANES)
        o = jnp.zeros((tq, LANES), F32)
        for h in range(half * 2, half * 2 + 2):
            r1 = slice(2 * h * tq, (2 * h + 1) * tq)
            r2 = slice((2 * h + 1) * tq, (2 * h + 2) * tq)
            o1 = acc_ref[r1, mine] / acc_ref[r1, other]
            o2 = acc_ref[r2, mine] / acc_ref[r2, other]
            o = jnp.where(head[:, mine] == h, o1 - lam * o2, o)
        halves.append(o)
    o = jnp.concatenate(halves, axis=1)
    o_ref[...] = _head_rms(o, gain_ref[...], _block_diag16()) * (1.0 - lam_init)


def _diff_prompt(proj32, kv16, nseq, seqlen, lam_p, gain, layer):
    tq = tk = 512
    nq = seqlen // tq
    return pl.pallas_call(
        functools.partial(_diff_prompt_kernel, tq=tq, tk=tk, layer=layer),
        grid=(nseq, nq),
        in_specs=[pl.BlockSpec((tq, MIX_W), lambda b, i: (b * nq + i, S_DQ)),
                  pl.BlockSpec((seqlen, MIX_W), lambda b, i: (b, KV_SLABS.index(S_DK))),
                  pl.BlockSpec((seqlen, MIX_W), lambda b, i: (b, KV_SLABS.index(S_DV))),
                  pl.BlockSpec(lam_p.shape, lambda b, i: (0, 0)),
                  pl.BlockSpec((1, MIX_W), lambda b, i: (0, 0))],
        out_specs=pl.BlockSpec((tq, MIX_W), lambda b, i: (b * nq + i, 0)),
        out_shape=jax.ShapeDtypeStruct((nseq * seqlen, MIX_W), F32),
        scratch_shapes=[pltpu.VMEM((2 * HEADS * tq, MIX_W), BF16),
                        pltpu.VMEM((2 * HEADS * tq, LANES), F32),
                        pltpu.VMEM((2 * HEADS * tq, MIX_W), F32)],
        compiler_params=_params("parallel", "arbitrary"),
        name="diff_prompt",
    )(proj32, kv16, kv16, lam_p, gain)


def _stack_rows(x, masks):
    return jnp.concatenate([jnp.where(mk, x, 0.0) for mk in masks], axis=0)


def _sample_attn_kernel(pt_ref, q_ref, sk_ref, sv_ref, dq_ref, dk_ref, dv_ref, *rest,
                        ls, n_pages, layer):
    del pt_ref
    pages, (lam_ref, gain_ref, osb_ref, odf_ref, sb_acc, sb_carry) = rest[:4 * n_pages], rest[4 * n_pages:]
    psk, psv, pdk, pdv = (pages[i * n_pages:(i + 1) * n_pages] for i in range(4))
    head = _lane_head()
    sb_masks = [head == h for h in range(HEADS)]
    qs = _stack_rows(q_ref[...] * (HEAD_W ** -0.5), sb_masks).astype(BF16)
    qd = _stack_rows(dq_ref[...] * (DIFF_DH ** -0.5), _combo_masks()).astype(BF16)

    def pad_page(x):
        return jnp.concatenate([x, jnp.zeros((PAGE_SIZE - ls, MIX_W), F32)], axis=0).astype(BF16)

    def later16(n):
        ri = lax.broadcasted_iota(jnp.int32, (n, n), 0)
        ci = lax.broadcasted_iota(jnp.int32, (n, n), 1)
        return jnp.where(ri > ci, 1.0, 0.0).astype(BF16)

    def sb_block(z, pv_fn, mask):
        lsg = _log_sigmoid(z)
        u = lsg - z
        if mask is not None:
            u = jnp.where(mask, u, 0.0)
        between = sb_carry[...] + _split_dot(u, later16(z.shape[1]))
        w = jnp.exp(lsg + between)
        if mask is not None:
            w = jnp.where(mask, w, 0.0)
        sb_acc[...] += pv_fn(w.astype(BF16))
        sb_carry[...] += jnp.sum(u, axis=1, keepdims=True)

    sb_acc[...] = jnp.zeros_like(sb_acc)
    sb_carry[...] = jnp.zeros_like(sb_carry)
    key = lax.broadcasted_iota(jnp.int32, (1, PAGE_SIZE), 1)
    t_sb = lax.broadcasted_iota(jnp.int32, (HEADS * ls, 1), 0) % ls
    sv_new = pad_page(sv_ref[...])
    sb_block(_dot_nt(qs, pad_page(sk_ref[...])), lambda w: _dot(w, sv_new), key < t_sb)

    def sb_past(blk):
        hi = n_pages - 1 - 2 * blk
        kt = jnp.concatenate([psk[hi - 1][...], psk[hi][...]], axis=1).astype(BF16)
        vt = jnp.concatenate([psv[hi - 1][...], psv[hi][...]], axis=1).astype(BF16)
        sb_block(_dot(qs, kt), lambda w: _dot_nt(w, vt), None)

    def sb_rest(blk):
        if blk < n_pages // 2:
            @pl.when(jnp.max(sb_carry[...]) > SB_DEAD_LOG)
            def _():
                sb_past(blk)
                sb_rest(blk + 1)

    sb_past(0)

    t_df = lax.broadcasted_iota(jnp.int32, (2 * HEADS * ls, 1), 0) % ls
    s_new = jnp.where(key <= t_df, _dot_nt(qd, pad_page(dk_ref[...])), MASK_VALUE)
    kt_all = jnp.concatenate([r[...].astype(BF16) for r in pdk], axis=1)
    s_past = _dot(qd, kt_all)
    m = jnp.maximum(jnp.max(s_new, axis=1, keepdims=True), jnp.max(s_past, axis=1, keepdims=True))
    p_new = jnp.exp(s_new - m)
    p_past = jnp.exp(s_past - m)
    l = jnp.sum(p_new, axis=1, keepdims=True) + jnp.sum(p_past, axis=1, keepdims=True)
    vt_all = jnp.concatenate([r[...].astype(BF16) for r in pdv], axis=1)
    on = (_dot(p_new.astype(BF16), pad_page(dv_ref[...]))
          + _dot_nt(p_past.astype(BF16), vt_all)) / l
    lam, lam_init = _diff_lambda(lam_ref, layer)
    o = jnp.zeros((ls, MIX_W), F32)
    for h in range(HEADS):
        o1 = on[(2 * h) * ls:(2 * h + 1) * ls, :]
        o2 = on[(2 * h + 1) * ls:(2 * h + 2) * ls, :]
        o = jnp.where(head == h, o1 - lam * o2, o)
    odf_ref[...] = _head_rms(o, gain_ref[...], _block_diag16()) * (1.0 - lam_init)

    sb_rest(1)
    acc = sb_acc[...]
    o = jnp.zeros((ls, MIX_W), F32)
    for h in range(HEADS):
        o = jnp.where(head == h, acc[h * ls:(h + 1) * ls, :], o)
    osb_ref[...] = o


def _sample_attn(proj32, page_table, caches, nseq, ls, lam_p, gain, layer):
    n_pages = page_table.shape[1]
    assert n_pages % 2 == 0

    def slab(c):
        return pl.BlockSpec((ls, MIX_W), lambda b, pt: (b, c))

    def page(p):
        return pl.BlockSpec((None, None, MIX_W, PAGE_SIZE), lambda b, pt: (layer, pt[b, p], 0, 0))

    out = pl.BlockSpec((ls, MIX_W), lambda b, pt: (b, 0))
    grid_spec = pltpu.PrefetchScalarGridSpec(
        num_scalar_prefetch=1,
        grid=(nseq,),
        in_specs=[slab(S_SQ), slab(S_SK), slab(S_SV), slab(S_DQ), slab(S_DK), slab(S_DV)]
                 + [page(p) for _ in range(4) for p in range(n_pages)]
                 + [pl.BlockSpec(lam_p.shape, lambda b, pt: (0, 0)),
                    pl.BlockSpec((1, MIX_W), lambda b, pt: (0, 0))],
        out_specs=[out, out],
        scratch_shapes=[pltpu.VMEM((HEADS * ls, MIX_W), F32), pltpu.VMEM((HEADS * ls, 1), F32)])
    page_args = [c for c in caches for _ in range(n_pages)]
    return pl.pallas_call(
        functools.partial(_sample_attn_kernel, ls=ls, n_pages=n_pages, layer=layer),
        grid_spec=grid_spec,
        out_shape=[jax.ShapeDtypeStruct((nseq * ls, MIX_W), F32)] * 2,
        compiler_params=_params("arbitrary"),
        name="sample_attn",
    )(page_table, *([proj32] * 6), *page_args, lam_p, gain)


def _merge_kernel(x_ref, b0_ref, b1_ref, b2_ref, b3_ref, pre_ref, wg_ref, wbr_ref, wo_ref,
                  post_ref, o_ref):
    x = x_ref[...]
    h = _rms(x, pre_ref[...]).astype(BF16)
    merged = None
    for i, b_ref in enumerate((b0_ref, b1_ref, b2_ref, b3_ref)):
        gate = jax.nn.sigmoid(_dot(h, wg_ref[:, i * D_MODEL:(i + 1) * D_MODEL]))
        term = gate * _dot(b_ref[...].astype(BF16), wbr_ref[i])
        merged = term if merged is None else merged + term
    o_ref[...] = x + _rms(_dot(merged.astype(BF16), wo_ref[...]), post_ref[...])


def _const_spec(shape):
    nd = len(shape)
    return pl.BlockSpec(shape, lambda i: (0,) * nd, pipeline_mode=pl.Buffered(1))


def _merge(x, branches, pre, wg16, wbr16, wo16, post):
    n = x.shape[0]
    tm = _row_tile(n, 256)
    tok = pl.BlockSpec((tm, D_MODEL), lambda i: (i, 0))
    br = pl.BlockSpec((tm, MIX_W), lambda i: (i, 0))
    return pl.pallas_call(
        _merge_kernel,
        grid=(n // tm,),
        in_specs=[tok, br, br, br, br, _const_spec((1, D_MODEL)), _const_spec(wg16.shape),
                  _const_spec(wbr16.shape), _const_spec(wo16.shape), _const_spec((1, D_MODEL))],
        out_specs=tok,
        out_shape=jax.ShapeDtypeStruct((n, D_MODEL), F32),
        compiler_params=_params("parallel"),
        name="merge",
    )(x, *branches, pre, wg16, wbr16, wo16, post)


def _cross_attend(q, mk, mv, channel_major):
    head = _lane_head()
    rows = q.shape[0]
    mk = mk.astype(BF16)
    mv = mv.astype(BF16)
    qs = _stack_rows(q * (HEAD_W ** -0.5), [head == h for h in range(HEADS)]).astype(BF16)
    s = _dot(qs, mk) if channel_major else _dot_nt(qs, mk)
    p = jnp.exp(s - jnp.max(s, axis=1, keepdims=True))
    p16 = p.astype(BF16)
    pv = (_dot_nt(p16, mv) if channel_major else _dot(p16, mv)) / jnp.sum(p, axis=1, keepdims=True)
    o = jnp.zeros(q.shape, F32)
    for h in range(HEADS):
        o = jnp.where(head == h, pv[h * rows:(h + 1) * rows, :], o)
    return o


def _cross_attn_kernel(q_ref, mk_ref, mv_ref, o_ref, *, channel_major):
    o_ref[...] = _cross_attend(q_ref[...], mk_ref[...], mv_ref[...], channel_major)


def _cross_attn(cq, mk, mv, nseq, seqlen, channel_major):
    tm = _row_tile(seqlen, 512)
    nt = seqlen // tm
    mem = pl.BlockSpec((None,) + mk.shape[1:], lambda b, i: (b, 0, 0))
    tok = pl.BlockSpec((tm, MIX_W), lambda b, i: (b * nt + i, 0))
    return pl.pallas_call(
        functools.partial(_cross_attn_kernel, channel_major=channel_major),
        grid=(nseq, nt),
        in_specs=[tok, mem, mem],
        out_specs=tok,
        out_shape=jax.ShapeDtypeStruct(cq.shape, F32),
        compiler_params=_params("parallel", "parallel"),
        name="cross_attn",
    )(cq, mk, mv)


def _post_kernel(x_ref, *refs, attend):
    if attend:
        cpre_ref, wq_ref, mk_ref, mv_ref, *refs = refs
        cq = _dot(_rms(x_ref[...], cpre_ref[...]).astype(BF16), wq_ref[...])
        co = _cross_attend(cq, mk_ref[...], mv_ref[...], channel_major=False)
    else:
        co_ref, *refs = refs
        co = co_ref[...]
    wco_ref, cpost_ref, mpre_ref, wup_ref, wdn_ref, mpost_ref, o_ref = refs
    x = x_ref[...] + _rms(_dot(co.astype(BF16), wco_ref[...]), cpost_ref[...])
    h = _rms(x, mpre_ref[...]).astype(BF16)
    u = jnp.square(jnp.maximum(_dot(h, wup_ref[...]), 0.0))
    o_ref[...] = x + _rms(_dot(u.astype(BF16), wdn_ref[...]), mpost_ref[...])


def _post(x, attn_args, wco16, cpost, mpre, wup16, wdn16, mpost):
    n = x.shape[0]
    tm = _row_tile(n, 256)
    tok = pl.BlockSpec((tm, D_MODEL), lambda i: (i, 0))
    attend = len(attn_args) > 1
    if attend:
        cpre, wq16, mk, mv, seqlen = attn_args
        tiles = seqlen // tm
        mem = pl.BlockSpec((None, N_MEM, MIX_W), lambda i: (i // tiles, 0, 0))
        attn_specs = [_const_spec((1, D_MODEL)), _const_spec(wq16.shape), mem, mem]
        attn_in = (cpre, wq16, mk, mv)
    else:
        attn_specs = [pl.BlockSpec((tm, MIX_W), lambda i: (i, 0))]
        attn_in = attn_args
    return pl.pallas_call(
        functools.partial(_post_kernel, attend=attend),
        grid=(n // tm,),
        in_specs=[tok] + attn_specs + [_const_spec(wco16.shape),
                  _const_spec((1, D_MODEL)), _const_spec((1, D_MODEL)), _const_spec(wup16.shape),
                  _const_spec(wdn16.shape), _const_spec((1, D_MODEL))],
        out_specs=tok,
        out_shape=jax.ShapeDtypeStruct((n, D_MODEL), F32),
        compiler_params=_params("parallel"),
        name="ca_out_mlp",
    )(x, *attn_in, wco16, cpost, mpre, wup16, wdn16, mpost)


def _trunk_layer(x, nseq, seqlen, rope_tab, mem_k, mem_v, mem_channel_major, w, rec_fn, attn_fn,
                 emit_kv):
    proj32, *kv = _mixer_proj(x, w["mix_pre_norm"], w["w_in"], rope_tab,
                              (nseq, seqlen) if emit_kv else None)
    o_ret, st_ret, o_hgrn, st_hgrn, new_kv = rec_fn(proj32, kv)
    o_sb, o_diff = attn_fn(proj32, kv)
    x = _merge(x, (o_ret, o_sb, o_diff, o_hgrn), w["mix_pre_norm"], w["w_gate"], w["w_br"],
               w["w_out"], w["mix_post_norm"])
    if seqlen % _row_tile(nseq * seqlen, 256) == 0:
        assert not mem_channel_major
        attn_args = (w["ca_pre_norm"], w["w_ca_q"], mem_k, mem_v, seqlen)
    else:
        cq = _norm_proj(x, w["ca_pre_norm"], w["w_ca_q"])
        attn_args = (_cross_attn(cq, mem_k, mem_v, nseq, seqlen, mem_channel_major),)
    x = _post(x, attn_args, w["w_ca_o"], w["ca_post_norm"], w["mlp_pre_norm"], w["w_mlp_up"],
              w["w_mlp_down"], w["mlp_post_norm"])
    return x, new_kv, st_ret, st_hgrn


def kernel(x_prompt, x_sample, mem_prompt, cache_sb_k, cache_sb_v, cache_diff_k, cache_diff_v, cache_mem_k, cache_mem_v, state_ret, state_hgrn, page_table, mix_pre_norm, mix_post_norm, ca_pre_norm, ca_post_norm, mlp_pre_norm, mlp_post_norm, mem_norm, w_in, w_gate, ret_norm, diff_lambda, diff_norm, hgrn_lb_logits, hgrn_norm, w_br_ret, w_br_sb, w_br_diff, w_br_hgrn, w_out, w_ca_q, w_ca_k, w_ca_v, w_ca_o, w_mlp_up, w_mlp_down):
    depth = w_in.shape[0]
    B, L, _ = x_prompt.shape
    DB, LS, _ = x_sample.shape
    n_pages = page_table.shape[1]
    past_len = n_pages * PAGE_SIZE

    tab_p = _rope_table(jnp.arange(L, dtype=jnp.int32))
    tab_s = jnp.tile(_rope_table(past_len + jnp.arange(LS, dtype=jnp.int32)), (256 // LS, 1))
    caches = [jnp.transpose(c, (0, 1, 3, 4, 2)).reshape(c.shape[0], c.shape[1], MIX_W, PAGE_SIZE)
              for c in (cache_sb_k, cache_sb_v, cache_diff_k, cache_diff_v)]
    mem_cache = [jnp.transpose(c, (0, 1, 3, 4, 2)).reshape(c.shape[0], c.shape[1], MIX_W, N_MEM)
                 for c in (cache_mem_k, cache_mem_v)]
    w_br = jnp.stack([w_br_ret, w_br_sb, w_br_diff, w_br_hgrn], axis=1).astype(BF16)
    w_ca_kv = jnp.concatenate([w_ca_k, w_ca_v], axis=-1).astype(BF16)
    diff_gain = jnp.tile(diff_norm, (1, HEADS))
    def lanes_cm(a):
        return jnp.broadcast_to(a.reshape(depth, HEADS, HEAD_W, 1), (depth, HEADS, HEAD_W, DB))
    ret_gain_cm, hgrn_gain_cm, lbl_cm = lanes_cm(ret_norm), lanes_cm(hgrn_norm), lanes_cm(hgrn_lb_logits)
    st_ret_cm = state_ret.transpose(0, 2, 3, 4, 1)
    st_hgrn_cm = state_hgrn.transpose(0, 2, 3, 4, 1)

    xp = x_prompt.reshape(B * L, D_MODEL)
    xs = x_sample.reshape(DB * LS, D_MODEL)
    mem = mem_prompt.reshape(B * N_MEM, D_MODEL)
    zero_state = jnp.zeros((B, MIX_W, MIX_W), F32)
    outs_p, outs_s = [], []
    for l in range(depth):
        row = lambda a: a[l][None, :]
        w = {
            "mix_pre_norm": row(mix_pre_norm), "mix_post_norm": row(mix_post_norm),
            "ca_pre_norm": row(ca_pre_norm), "ca_post_norm": row(ca_post_norm),
            "mlp_pre_norm": row(mlp_pre_norm), "mlp_post_norm": row(mlp_post_norm),
            "ret_norm": row(ret_norm), "hgrn_norm": row(hgrn_norm),
            "hgrn_lb_logits": hgrn_lb_logits,
            "w_in": w_in[l].astype(BF16), "w_gate": w_gate[l].astype(BF16), "w_br": w_br[l],
            "w_out": w_out[l].astype(BF16), "w_ca_q": w_ca_q[l].astype(BF16),
            "w_ca_o": w_ca_o[l].astype(BF16), "w_mlp_up": w_mlp_up[l].astype(BF16),
            "w_mlp_down": w_mlp_down[l].astype(BF16),
        }
        lam_p, dgain = diff_lambda[l], diff_gain[l][None, :]

        mkv = _norm_proj(mem, row(mem_norm), w_ca_kv[l])
        mk_p = mkv[:, :MIX_W].reshape(B, N_MEM, MIX_W)
        mv_p = mkv[:, MIX_W:].reshape(B, N_MEM, MIX_W)

        def rec_p(proj32, kv, l=l, w=w):
            o_ret, st_ret = _recurrence("ret", proj32, B, L, 256, w["ret_norm"], zero_state)
            o_hgrn, st_hgrn = _recurrence("hgrn", proj32, B, L, 32, w["hgrn_norm"], zero_state,
                                          hgrn_lb_logits, l, nsub=8)
            return o_ret, _bd_to_state(st_ret), o_hgrn, _bd_to_state(st_hgrn), list(kv[1:])

        def attn_p(proj32, kv, l=l, lam_p=lam_p, dgain=dgain):
            return (_sb_prompt(proj32, kv[0], B, L),
                    _diff_prompt(proj32, kv[0], B, L, lam_p, dgain, l))

        xp, kv_p, ret_p, hgrn_p = _trunk_layer(xp, B, L, tab_p, mk_p, mv_p, False, w, rec_p, attn_p,
                                               True)
        outs_p.append((kv_p, ret_p, hgrn_p, mk_p, mv_p))

        def rec_s(proj32, kv, l=l):
            del kv
            pcm = proj32.reshape(DB, LS, N_SLABS, HEADS, HEAD_W).transpose(2, 1, 3, 4, 0)
            o_ret, st_ret = _decode_recurrence("ret", pcm, ret_gain_cm[l], st_ret_cm[l])
            o_hgrn, st_hgrn = _decode_recurrence("hgrn", pcm, hgrn_gain_cm[l], st_hgrn_cm[l],
                                                 lbl_cm, l)
            rows = lambda o: o.transpose(3, 0, 1, 2).reshape(DB * LS, MIX_W)
            return (rows(o_ret), st_ret.transpose(3, 0, 1, 2),
                    rows(o_hgrn), st_hgrn.transpose(3, 0, 1, 2), [pcm[c] for c in KV_SLABS])

        def attn_s(proj32, kv, l=l, lam_p=lam_p, dgain=dgain):
            del kv
            return _sample_attn(proj32, page_table, caches, DB, LS, lam_p, dgain, l)

        xs, kv_s, ret_s, hgrn_s = _trunk_layer(xs, DB, LS, tab_s, mem_cache[0][l], mem_cache[1][l],
                                               True, w, rec_s, attn_s, False)
        outs_s.append((kv_s, ret_s, hgrn_s))

    def kv_prompt(i):
        a = jnp.stack([o[0][i] for o in outs_p], axis=0)
        return a.reshape(depth, B, HEADS, HEAD_W, L).transpose(0, 1, 4, 2, 3)

    def kv_sample(i):
        return jnp.stack([o[0][i] for o in outs_s], axis=0).transpose(0, 4, 1, 2, 3)

    def states(outs, i):
        return jnp.stack([o[i] for o in outs], axis=0)

    def memkv(i):
        return jnp.stack([o[i].reshape(B, N_MEM, HEADS, HEAD_W) for o in outs_p], axis=0)

    return (xp.reshape(B, L, D_MODEL), xs.reshape(DB, LS, D_MODEL),
            kv_prompt(0), kv_prompt(1), kv_prompt(2), kv_prompt(3),
            memkv(3), memkv(4), states(outs_p, 1), states(outs_p, 2),
            kv_sample(0), kv_sample(1), kv_sample(2), kv_sample(3),
            states(outs_s, 1), states(outs_s, 2))
```

```python
import functools
import math

import jax
import jax.numpy as jnp
import numpy as np
from jax import lax
from jax.experimental import pallas as pl
from jax.experimental.pallas import tpu as pltpu

F32 = jnp.float32
BF16 = jnp.bfloat16

D_MODEL = 1024
N_MEM = 256
HEADS = 4
HEAD_W = 64
MIX_W = HEADS * HEAD_W
DIFF_DH = 32
D_FF = 4 * D_MODEL
N_BRANCH = 4
PAGE_SIZE = 128
ROPE_THETA = 10000.0
NORM_EPS = 1e-6
MASK_VALUE = -1e30
LOG2_E = math.log2(math.e)
LANES = 128
SUBLANES = 8
assert MIX_W == 2 * LANES and 2 * HEAD_W == LANES
N_SLABS = 14
D_IN = N_SLABS * MIX_W
(S_RQ, S_RK, S_RV, S_RG, S_SQ, S_SK, S_SV, S_DQ, S_DK, S_DV,
 S_GQ, S_GF, S_GI, S_GG) = range(N_SLABS)
KV_SLABS = (S_SK, S_SV, S_DK, S_DV)

SB_DEAD_LOG = -104.0

V7X_VMEM_BYTES = 64 * 1024 * 1024
VMEM_LIMIT = (V7X_VMEM_BYTES * 7) // 8

_NT = (((1,), (1,)), ((), ()))
_TN = (((0,), (0,)), ((), ()))


def _params(*sem):
    return pltpu.CompilerParams(dimension_semantics=sem, vmem_limit_bytes=VMEM_LIMIT)


def _dot(a, b):
    return jnp.dot(a, b, preferred_element_type=F32)


def _dot_nt(a, b):
    return lax.dot_general(a, b, _NT, preferred_element_type=F32)


def _dot_tn(a, b):
    return lax.dot_general(a, b, _TN, preferred_element_type=F32)


def _split_dot(x, m16, terms=2, left=False):
    out = None
    r = x
    for _ in range(terms):
        p = r.astype(BF16)
        d = _dot(m16, p) if left else _dot(p, m16)
        out = d if out is None else out + d
        r = r - p.astype(F32)
    return out


def _rms(x, g):
    ms = jnp.mean(x * x, axis=-1, keepdims=True)
    return x * lax.rsqrt(ms + NORM_EPS) * g


def _lane_head(width=MIX_W):
    return lax.broadcasted_iota(jnp.int32, (1, width), 1) // HEAD_W


def _block_diag16():
    r = lax.broadcasted_iota(jnp.int32, (MIX_W, MIX_W), 0) // HEAD_W
    c = lax.broadcasted_iota(jnp.int32, (MIX_W, MIX_W), 1) // HEAD_W
    return jnp.where(r == c, 1.0, 0.0).astype(BF16)


def _head_rms(o, gain, bd16):
    ms = _split_dot(o * o, bd16) * (1.0 / HEAD_W)
    return o * lax.rsqrt(ms + NORM_EPS) * gain


def _log_sigmoid(z):
    return jnp.minimum(z, 0.0) - jnp.log(1.0 + jnp.exp(-jnp.abs(z)))


def _rope_slab(s, cos, sin, half):
    lane = lax.broadcasted_iota(jnp.int32, (1, MIX_W), 1)
    first = (lane % (2 * half)) < half
    partner = jnp.where(first, pltpu.roll(s, MIX_W - half, 1), pltpu.roll(s, half, 1))
    return s * cos + partner * sin


def _norm_proj_kernel(x_ref, g_ref, w_ref, o32_ref):
    h = _rms(x_ref[...], g_ref[...]).astype(BF16)
    o32_ref[...] = _dot(h, w_ref[...])


def _mixer_proj_kernel(x_ref, g_ref, w_ref, tab_ref, o32_ref, *kv_refs):
    h = _rms(x_ref[...], g_ref[...]).astype(BF16)
    y = _dot(h, w_ref[...])
    cos_r = tab_ref[:, 0 * MIX_W:1 * MIX_W]
    sin_r = tab_ref[:, 1 * MIX_W:2 * MIX_W]
    cos_d = tab_ref[:, 2 * MIX_W:3 * MIX_W]
    sin_d = tab_ref[:, 3 * MIX_W:4 * MIX_W]
    for c in range(N_SLABS):
        s = y[:, c * MIX_W:(c + 1) * MIX_W]
        if c == S_RQ:
            s = _rope_slab(s, cos_r, sin_r, HEAD_W // 2)
        elif c == S_RK:
            s = _rope_slab(s, cos_r, sin_r, HEAD_W // 2) * (HEAD_W ** -0.5)
        elif c in (S_DQ, S_DK):
            s = _rope_slab(s, cos_d, sin_d, DIFF_DH // 2)
        o32_ref[:, c * MIX_W:(c + 1) * MIX_W] = s
        if kv_refs and c in KV_SLABS:
            i = KV_SLABS.index(c)
            kv_refs[0][:, i * MIX_W:(i + 1) * MIX_W] = s.astype(BF16)
            kv_refs[1 + i][...] = s.T


def _row_tile(n, want):
    t = math.gcd(n, want)
    assert t % 8 == 0
    return t


def _norm_proj(x, gain, w16):
    n, d = x.shape
    wout = w16.shape[1]
    tm = _row_tile(n, 256)
    return pl.pallas_call(
        _norm_proj_kernel,
        grid=(n // tm,),
        in_specs=[pl.BlockSpec((tm, d), lambda i: (i, 0)),
                  pl.BlockSpec((1, d), lambda i: (0, 0)),
                  pl.BlockSpec((d, wout), lambda i: (0, 0))],
        out_specs=pl.BlockSpec((tm, wout), lambda i: (i, 0)),
        out_shape=jax.ShapeDtypeStruct((n, wout), F32),
        compiler_params=_params("parallel"),
        name="norm_proj",
    )(x, gain, w16)


def _mixer_proj(x, gain, w16, tab, kv_seqs=None):
    n, d = x.shape
    tm = _row_tile(n, 256)
    tab_blocks = tab.shape[0] // tm
    out_specs = [pl.BlockSpec((tm, D_IN), lambda i: (i, 0))]
    out_shape = [jax.ShapeDtypeStruct((n, D_IN), F32)]
    if kv_seqs is not None:
        nseq, seqlen = kv_seqs
        tiles = seqlen // tm
        nkv = len(KV_SLABS)
        out_specs += [pl.BlockSpec((tm, nkv * MIX_W), lambda i: (i, 0))]
        out_specs += [pl.BlockSpec((None, MIX_W, tm), lambda i: (i // tiles, 0, i % tiles))] * nkv
        out_shape += [jax.ShapeDtypeStruct((n, nkv * MIX_W), BF16)]
        out_shape += [jax.ShapeDtypeStruct((nseq, MIX_W, seqlen), F32)] * nkv
    return pl.pallas_call(
        _mixer_proj_kernel,
        grid=(n // tm,),
        in_specs=[pl.BlockSpec((tm, d), lambda i: (i, 0)),
                  pl.BlockSpec((1, d), lambda i: (0, 0)),
                  pl.BlockSpec((d, D_IN), lambda i: (0, 0)),
                  pl.BlockSpec((tm, 4 * MIX_W), lambda i: (i % tab_blocks, 0))],
        out_specs=out_specs,
        out_shape=out_shape,
        compiler_params=_params("parallel"),
        name="mixer_proj",
    )(x, gain, w16, tab)


def _rope_table(pos):
    def one(group):
        half = group // 2
        inv = ROPE_THETA ** (-jnp.arange(half, dtype=F32) * 2.0 / group)
        ang = pos.astype(F32)[:, None] * inv[None, :]
        cos, sin = jnp.cos(ang), jnp.sin(ang)
        reps = MIX_W // group
        return (jnp.tile(jnp.concatenate([cos, cos], -1), (1, reps)),
                jnp.tile(jnp.concatenate([-sin, sin], -1), (1, reps)))
    cr, sr = one(HEAD_W)
    cd, sd = one(DIFF_DH)
    return jnp.concatenate([cr, sr, cd, sd], axis=-1)


_RET_GAMMA = [1.0 - 2.0 ** (-5.0 - h) for h in range(HEADS)]
_RET_LOG_GAMMA = [float(np.log(g)) for g in _RET_GAMMA]


def _state_step(st, q_dec, k_dec, v, decay_row):
    inter = _dot_nt(q_dec.astype(BF16), st.astype(BF16))
    upd = _dot_tn(v.astype(BF16), k_dec.astype(BF16))
    r = lax.broadcasted_iota(jnp.int32, (MIX_W, MIX_W), 0) // HEAD_W
    c = lax.broadcasted_iota(jnp.int32, (MIX_W, MIX_W), 1) // HEAD_W
    return inter, st * decay_row + jnp.where(r == c, upd, 0.0)


def _retention_kernel(q_ref, k_ref, v_ref, g_ref, gain_ref, st0_ref, o_ref, st_ref, *, chunk):
    @pl.when(pl.program_id(1) == 0)
    def _():
        st_ref[...] = st0_ref[...]

    q, k, v = q_ref[...], k_ref[...], v_ref[...]
    head = _lane_head()
    lg = jnp.zeros((1, MIX_W), F32)
    for h in range(HEADS):
        lg = jnp.where(head == h, _RET_LOG_GAMMA[h], lg)
    steps = lax.broadcasted_iota(jnp.int32, (chunk, 1), 0).astype(F32) + 1.0
    b = steps * lg
    b_last = float(chunk) * lg
    o, st_new = _state_step(st_ref[0], q * jnp.exp(b), k * jnp.exp(b_last - b), v,
                            jnp.exp(b_last))
    st_ref[0] = st_new

    ri = lax.broadcasted_iota(jnp.int32, (chunk, chunk), 0)
    ci = lax.broadcasted_iota(jnp.int32, (chunk, chunk), 1)
    causal = ri >= ci
    dist = (ri - ci).astype(F32)
    k16, v16 = k.astype(BF16), v.astype(BF16)
    for h in range(HEADS):
        sc = _dot_nt(jnp.where(head == h, q, 0.0).astype(BF16), k16)
        dec = jnp.where(causal, jnp.exp(dist * _RET_LOG_GAMMA[h]), 0.0)
        o = o + jnp.where(head == h, _dot((sc * dec).astype(BF16), v16), 0.0)

    g = g_ref[...]
    o_ref[...] = _head_rms(o, gain_ref[...], _block_diag16()) * (g * jax.nn.sigmoid(g))


def _hgrn_kernel(q_ref, f_ref, v_ref, g_ref, gain_ref, lbl_ref, st0_ref, o_ref, st_ref,
                 kk_scr, b_scr, *, chunk, nsub, layer):
    @pl.when(pl.program_id(1) == 0)
    def _():
        st_ref[...] = st0_ref[...]

    lbl = lbl_ref[...]
    e = jnp.exp(lbl - jnp.max(lbl, axis=0, keepdims=True))
    lb_w = e / jnp.sum(e, axis=0, keepdims=True)
    lb = lb_w[0:1, :]
    for i in range(1, layer + 1):
        lb = lb + lb_w[i:i + 1, :]
    lb = lb - lb_w[0:1, :]

    z_all = f_ref[...]
    kk_scr[...] = (1.0 - lb) * jax.nn.sigmoid(-z_all)
    log2_f = jnp.log(lb + (1.0 - lb) * jax.nn.sigmoid(z_all)) * LOG2_E
    ri = lax.broadcasted_iota(jnp.int32, (chunk, chunk), 0)
    ci = lax.broadcasted_iota(jnp.int32, (chunk, chunk), 1)
    tril16 = jnp.where(ri >= ci, 1.0, 0.0).astype(BF16)
    bd16 = _block_diag16()
    rows8 = lax.broadcasted_iota(jnp.int32, (SUBLANES, 1), 0)
    groups = chunk // SUBLANES
    r_blk = lax.broadcasted_iota(jnp.int32, (MIX_W, MIX_W), 0) // HEAD_W
    c_blk = lax.broadcasted_iota(jnp.int32, (MIX_W, MIX_W), 1) // HEAD_W

    bs, q_decs, updates, decays = [], [], [], []
    for s in range(nsub):
        lo = s * chunk
        q, v, k = q_ref[lo:lo + chunk, :], v_ref[lo:lo + chunk, :], kk_scr[lo:lo + chunk, :]
        lf = log2_f[lo:lo + chunk, :]
        b = _split_dot(lf, tril16, terms=3, left=True)
        b_scr[lo:lo + chunk, :] = b
        b_last = b[chunk - 1:chunk, :]
        bs.append(b)
        q_decs.append((q * jnp.exp2(b)).astype(BF16))
        upd = _dot_tn(v.astype(BF16), (k * jnp.exp2(b_last - b)).astype(BF16))
        updates.append(jnp.where(r_blk == c_blk, upd, 0.0))
        decays.append(jnp.exp2(b_last))
    states = [st_ref[0]]
    for s in range(nsub):
        states.append(states[-1] * decays[s] + updates[s])
    st_ref[0] = states[-1]

    for s in range(nsub):
        lo = s * chunk
        q, b = q_ref[lo:lo + chunk, :], bs[s]
        inter = _dot_nt(q_decs[s], states[s].astype(BF16))
        qg = [q[r * SUBLANES:(r + 1) * SUBLANES, :] for r in range(groups)]
        bg = [b[r * SUBLANES:(r + 1) * SUBLANES, :] for r in range(groups)]
        slabs, where = [], []
        for j in range(chunk):
            kj = kk_scr[lo + j:lo + j + 1, :]
            bj = b_scr[lo + j:lo + j + 1, :]
            for r in range(j // SUBLANES, groups):
                first = r * SUBLANES
                t = qg[r] * kj * jnp.exp2(bg[r] - bj)
                slabs.append(jnp.where(rows8 + first >= j, t, 0.0) if first < j else t)
                where.append((j, r))
        pair_sum = _dot(jnp.concatenate(slabs, axis=0).astype(BF16), bd16)
        og = [inter[r * SUBLANES:(r + 1) * SUBLANES, :] for r in range(groups)]
        for i, (j, r) in enumerate(where):
            og[r] = og[r] + pair_sum[i * SUBLANES:(i + 1) * SUBLANES, :] * v_ref[lo + j:lo + j + 1, :]
        o = jnp.concatenate(og, axis=0) if groups > 1 else og[0]
        g = g_ref[lo:lo + chunk, :]
        o_ref[lo:lo + chunk, :] = _head_rms(o, gain_ref[...], bd16) * (g * jax.nn.sigmoid(g))


def _slab_spec(rows, slab, nchunk):
    return pl.BlockSpec((rows, MIX_W), lambda s, c: (s * nchunk + c, slab))


def _recurrence(kind, proj32, nseq, seqlen, chunk, gain, st0, lb_logits=None, layer=0, nsub=1):
    rows = chunk * nsub
    nstep = seqlen // rows
    n = nseq * seqlen
    row = pl.BlockSpec((1, MIX_W), lambda s, c: (0, 0))
    st_spec = pl.BlockSpec((1, MIX_W, MIX_W), lambda s, c: (s, 0, 0))
    out_specs = [pl.BlockSpec((rows, MIX_W), lambda s, c: (s * nstep + c, 0)), st_spec]
    out_shape = [jax.ShapeDtypeStruct((n, MIX_W), F32),
                 jax.ShapeDtypeStruct((nseq, MIX_W, MIX_W), F32)]
    if kind == "ret":
        assert nsub == 1
        slabs = (S_RQ, S_RK, S_RV, S_RG)
        body = functools.partial(_retention_kernel, chunk=chunk)
        extra_specs, extra, scratch = [row, st_spec], (gain, st0), []
    else:
        slabs = (S_GQ, S_GF, S_GI, S_GG)
        body = functools.partial(_hgrn_kernel, chunk=chunk, nsub=nsub, layer=layer)
        extra_specs = [row, pl.BlockSpec(lb_logits.shape, lambda s, c: (0, 0)), st_spec]
        extra = (gain, lb_logits, st0)
        scratch = [pltpu.VMEM((rows, MIX_W), F32), pltpu.VMEM((rows, MIX_W), F32)]
    return pl.pallas_call(
        body,
        grid=(nseq, nstep),
        in_specs=[_slab_spec(rows, s, nstep) for s in slabs] + extra_specs,
        out_specs=out_specs,
        out_shape=out_shape,
        scratch_shapes=scratch,
        compiler_params=_params("parallel", "arbitrary"),
        name="recurrence_" + kind,
    )(proj32, proj32, proj32, proj32, *extra)


def _decode_rec_kernel(q_ref, k_ref, v_ref, g_ref, gain_ref, *rest, per_channel, ls, layer):
    if per_channel:
        lbl_ref, s0_ref, o_ref, s_ref, f_scr, kk_scr, oacc_scr = rest
        lbl = lbl_ref[...]
        e = jnp.exp(lbl - jnp.max(lbl, axis=0, keepdims=True))
        lb_w = e / jnp.sum(e, axis=0, keepdims=True)
        lb = lb_w[0]
        for i in range(1, layer + 1):
            lb = lb + lb_w[i]
        lb = lb - lb_w[0]
        for t in range(ls):
            z = k_ref[t]
            f_scr[t] = lb + (1.0 - lb) * jax.nn.sigmoid(z)
            kk_scr[t] = (1.0 - lb) * jax.nn.sigmoid(-z)
        key_ref = kk_scr
    else:
        s0_ref, o_ref, s_ref, oacc_scr = rest
        h = pl.program_id(0)
        gamma = jnp.float32(_RET_GAMMA[0])
        for i in range(1, HEADS):
            gamma = jnp.where(h == i, jnp.float32(_RET_GAMMA[i]), gamma)
        key_ref = k_ref
    oacc_scr[...] = jnp.zeros_like(oacc_scr)

    def channel(c, carry):
        s = s0_ref[c]
        for t in range(ls):
            f = f_scr[t, pl.ds(c, 1), :] if per_channel else gamma
            s = f * s + key_ref[t, pl.ds(c, 1), :] * v_ref[t]
            oacc_scr[t] += q_ref[t, pl.ds(c, 1), :] * s
        s_ref[c] = s
        return carry

    lax.fori_loop(0, HEAD_W, channel, 0)
    for t in range(ls):
        o = oacc_scr[t]
        ms = jnp.mean(o * o, axis=0, keepdims=True)
        g = g_ref[t]
        o_ref[t] = o * lax.rsqrt(ms + NORM_EPS) * gain_ref[...] * (g * jax.nn.sigmoid(g))


def _decode_recurrence(kind, proj_cm, gain_cm, s0_cm, lbl_cm=None, layer=0):
    _, ls, _, _, nb = proj_cm.shape
    per_channel = kind == "hgrn"
    slabs = (S_GQ, S_GF, S_GI, S_GG) if per_channel else (S_RQ, S_RK, S_RV, S_RG)

    def act(slab):
        return pl.BlockSpec((None, ls, None, HEAD_W, nb), lambda h: (slab, 0, h, 0, 0))

    st_spec = pl.BlockSpec((None, HEAD_W, HEAD_W, nb), lambda h: (h, 0, 0, 0))
    in_specs = [act(s) for s in slabs] + [pl.BlockSpec((None, HEAD_W, nb), lambda h: (h, 0, 0))]
    args = [proj_cm] * 4 + [gain_cm]
    scratch = [pltpu.VMEM((ls, HEAD_W, nb), F32)]
    if per_channel:
        in_specs.append(pl.BlockSpec((lbl_cm.shape[0], None, HEAD_W, nb), lambda h: (0, h, 0, 0)))
        args.append(lbl_cm)
        scratch = [pltpu.VMEM((ls, HEAD_W, nb), F32)] * 3
    return pl.pallas_call(
        functools.partial(_decode_rec_kernel, per_channel=per_channel, ls=ls, layer=layer),
        grid=(HEADS,),
        in_specs=in_specs + [st_spec],
        out_specs=[pl.BlockSpec((ls, None, HEAD_W, nb), lambda h: (0, h, 0, 0)), st_spec],
        out_shape=[jax.ShapeDtypeStruct((ls, HEADS, HEAD_W, nb), F32),
                   jax.ShapeDtypeStruct(s0_cm.shape, F32)],
        scratch_shapes=scratch,
        compiler_params=_params("parallel"),
        name="decode_rec_" + kind,
    )(*args, s0_cm)


def _bd_to_state(bd):
    b = bd.shape[0]
    r = bd.reshape(b, HEADS, HEAD_W, HEADS, HEAD_W)
    st = jnp.stack([r[:, h, :, h, :] for h in range(HEADS)], axis=1)
    return jnp.swapaxes(st, 2, 3)


def _sb_prompt_kernel(q_ref, k_ref, v_ref, o_ref, qs_ref, carry_ref, acc_ref, *, tq, tk):
    qi = pl.program_id(1)
    head = _lane_head()
    q = q_ref[...].astype(F32) * (HEAD_W ** -0.5)
    for h in range(HEADS):
        qs_ref[h * tq:(h + 1) * tq, :] = jnp.where(head == h, q, 0.0).astype(BF16)
    ri = lax.broadcasted_iota(jnp.int32, (tk, tk), 0)
    ci = lax.broadcasted_iota(jnp.int32, (tk, tk), 1)
    later16 = jnp.where(ri > ci, 1.0, 0.0).astype(BF16)
    reps = tk // LANES

    def block(kb, mask):
        start = pl.multiple_of(kb * tk, tk)
        z = _dot_nt(qs_ref[...], k_ref[pl.ds(start, tk), :])
        ls = _log_sigmoid(z)
        u = ls - z
        if mask is not None:
            u = jnp.where(mask, u, 0.0)
        between = jnp.concatenate([carry_ref[...]] * reps, axis=1) + _split_dot(u, later16)
        w = jnp.exp(ls + between)
        if mask is not None:
            w = jnp.where(mask, w, 0.0)
        acc_ref[...] += _dot(w.astype(BF16), v_ref[pl.ds(start, tk), :])
        carry = carry_ref[...] + jnp.sum(u, axis=1, keepdims=True)
        carry_ref[...] = carry
        return jnp.max(carry)

    acc_ref[...] = jnp.zeros_like(acc_ref)
    carry_ref[...] = jnp.zeros_like(carry_ref)
    qpos = qi * tq + lax.broadcasted_iota(jnp.int32, (HEADS * tq, 1), 0) % tq
    diag = qi * tq // tk
    kpos = diag * tk + lax.broadcasted_iota(jnp.int32, (1, tk), 1)
    alive = block(diag, kpos < qpos)

    def cond(state):
        return jnp.logical_and(state[0] >= 0, state[1] > SB_DEAD_LOG)

    def body(state):
        return state[0] - 1, block(state[0], None)

    lax.while_loop(cond, body, (diag - 1, alive))
    acc = acc_ref[...]
    o = jnp.zeros((tq, MIX_W), F32)
    for h in range(HEADS):
        o = jnp.where(head == h, acc[h * tq:(h + 1) * tq, :], o)
    o_ref[...] = o


def _sb_prompt(proj32, kv16, nseq, seqlen):
    tq = tk = 256
    nq = seqlen // tq
    return pl.pallas_call(
        functools.partial(_sb_prompt_kernel, tq=tq, tk=tk),
        grid=(nseq, nq),
        in_specs=[pl.BlockSpec((tq, MIX_W), lambda b, i: (b * nq + i, S_SQ)),
                  pl.BlockSpec((seqlen, MIX_W), lambda b, i: (b, KV_SLABS.index(S_SK))),
                  pl.BlockSpec((seqlen, MIX_W), lambda b, i: (b, KV_SLABS.index(S_SV)))],
        out_specs=pl.BlockSpec((tq, MIX_W), lambda b, i: (b * nq + i, 0)),
        out_shape=jax.ShapeDtypeStruct((nseq * seqlen, MIX_W), F32),
        scratch_shapes=[pltpu.VMEM((HEADS * tq, MIX_W), BF16),
                        pltpu.VMEM((HEADS * tq, LANES), F32),
                        pltpu.VMEM((HEADS * tq, MIX_W), F32)],
        compiler_params=_params("parallel", "arbitrary"),
        name="sb_prompt",
    )(proj32, kv16, kv16)


def _diff_lambda(lam_ref, layer):
    lp = lam_ref[...]
    lam_init = 0.8 - 0.6 * math.exp(-0.3 * layer)
    lam = (jnp.exp(jnp.sum(lp[0:1, :] * lp[1:2, :], axis=1, keepdims=True))
           - jnp.exp(jnp.sum(lp[2:3, :] * lp[3:4, :], axis=1, keepdims=True)) + lam_init)
    return lam, lam_init


def _combo_masks():
    lane = lax.broadcasted_iota(jnp.int32, (1, MIX_W), 1)
    return [(lane // DIFF_DH) == (2 * (c // 2) + (c % 2)) for c in range(2 * HEADS)]


def _diff_prompt_kernel(q_ref, k_ref, v_ref, lam_ref, gain_ref, o_ref, qs_ref, m_ref, acc_ref,
                        *, tq, tk, layer):
    qi = pl.program_id(1)
    head = _lane_head()
    q = q_ref[...] * (DIFF_DH ** -0.5 * LOG2_E)
    for c, mk in enumerate(_combo_masks()):
        qs_ref[c * tq:(c + 1) * tq, :] = jnp.where(mk, q, 0.0).astype(BF16)
    m_ref[...] = jnp.full(m_ref.shape, MASK_VALUE, F32)
    acc_ref[...] = jnp.zeros_like(acc_ref)
    qpos = qi * tq + lax.broadcasted_iota(jnp.int32, (tq, 1), 0)
    reps = tk // LANES

    def block(kb, masked):
        start = pl.multiple_of(kb * tk, tk)
        kblk = k_ref[pl.ds(start, tk), :]
        vblk = v_ref[pl.ds(start, tk), :]
        if masked:
            mask = (start + lax.broadcasted_iota(jnp.int32, (1, tk), 1)) <= qpos
        vaug = [jnp.where(head == h, vblk, jnp.ones_like(vblk)) for h in range(HEADS)]
        for c in range(2 * HEADS):
            rows = slice(c * tq, (c + 1) * tq)
            s = _dot_nt(qs_ref[rows, :], kblk)
            if masked:
                s = jnp.where(mask, s, MASK_VALUE)
            m_old = m_ref[rows, :]
            m_new = jnp.maximum(m_old, jnp.max(s, axis=1, keepdims=True))
            m_ref[rows, :] = m_new
            p = jnp.exp2(s - jnp.concatenate([m_new] * reps, axis=1)).astype(BF16)
            alpha = jnp.exp2(m_old - m_new)
            acc_ref[rows, :] = (jnp.concatenate([alpha] * (MIX_W // LANES), axis=1) * acc_ref[rows, :]
                                + _dot(p, vaug[c // 2]))

    def unmasked(kb, carry):
        block(kb, False)
        return carry

    diag = qi * tq // tk
    lax.fori_loop(0, diag, unmasked, 0)
    block(diag, True)

    lam, lam_init = _diff_lambda(lam_ref, layer)
    halves = []
    for half in range(MIX_W // LANES):
        mine = slice(half * LANES, (half + 1) * LANES)
        other = slice((1 - half) * LANES, (2 - half) * LANES)
        o = jnp.zeros((tq, LANES), F32)
        for h in range(half * 2, half * 2 + 2):
            r1 = slice(2 * h * tq, (2 * h + 1) * tq)
            r2 = slice((2 * h + 1) * tq, (2 * h + 2) * tq)
            o1 = acc_ref[r1, mine] / acc_ref[r1, other]
            o2 = acc_ref[r2, mine] / acc_ref[r2, other]
            o = jnp.where(head[:, mine] == h, o1 - lam * o2, o)
        halves.append(o)
    o = jnp.concatenate(halves, axis=1)
    o_ref[...] = _head_rms(o, gain_ref[...], _block_diag16()) * (1.0 - lam_init)


def _diff_prompt(proj32, kv16, nseq, seqlen, lam_p, gain, layer):
    tq = tk = 512
    nq = seqlen // tq
    return pl.pallas_call(
        functools.partial(_diff_prompt_kernel, tq=tq, tk=tk, layer=layer),
        grid=(nseq, nq),
        in_specs=[pl.BlockSpec((tq, MIX_W), lambda b, i: (b * nq + i, S_DQ)),
                  pl.BlockSpec((seqlen, MIX_W), lambda b, i: (b, KV_SLABS.index(S_DK))),
                  pl.BlockSpec((seqlen, MIX_W), lambda b, i: (b, KV_SLABS.index(S_DV))),
                  pl.BlockSpec(lam_p.shape, lambda b, i: (0, 0)),
                  pl.BlockSpec((1, MIX_W), lambda b, i: (0, 0))],
        out_specs=pl.BlockSpec((tq, MIX_W), lambda b, i: (b * nq + i, 0)),
        out_shape=jax.ShapeDtypeStruct((nseq * seqlen, MIX_W), F32),
        scratch_shapes=[pltpu.VMEM((2 * HEADS * tq, MIX_W), BF16),
                        pltpu.VMEM((2 * HEADS * tq, LANES), F32),
                        pltpu.VMEM((2 * HEADS * tq, MIX_W), F32)],
        compiler_params=_params("parallel", "arbitrary"),
        name="diff_prompt",
    )(proj32, kv16, kv16, lam_p, gain)


def _stack_rows(x, masks):
    return jnp.concatenate([jnp.where(mk, x, 0.0) for mk in masks], axis=0)


def _sample_attn_kernel(pt_ref, q_ref, sk_ref, sv_ref, dq_ref, dk_ref, dv_ref, *rest,
                        ls, n_pages, layer, group):
    del pt_ref
    n_ops = 4 * group * n_pages
    pages, (lam_ref, gain_ref, osb_ref, odf_ref, sb_acc, sb_carry) = rest[:n_ops], rest[n_ops:]
    head = _lane_head()
    sb_masks = [head == h for h in range(HEADS)]
    key = lax.broadcasted_iota(jnp.int32, (1, PAGE_SIZE), 1)
    t_sb = lax.broadcasted_iota(jnp.int32, (HEADS * ls, 1), 0) % ls
    t_df = lax.broadcasted_iota(jnp.int32, (2 * HEADS * ls, 1), 0) % ls
    lam, lam_init = _diff_lambda(lam_ref, layer)

    def seq_pages(cache, g):
        lo = (cache * group + g) * n_pages
        return pages[lo:lo + n_pages]

    def pad_page(x):
        return jnp.concatenate([x, jnp.zeros((PAGE_SIZE - ls, MIX_W), F32)], axis=0).astype(BF16)

    def later16(n):
        ri = lax.broadcasted_iota(jnp.int32, (n, n), 0)
        ci = lax.broadcasted_iota(jnp.int32, (n, n), 1)
        return jnp.where(ri > ci, 1.0, 0.0).astype(BF16)

    def sb_block(g, z, pv_fn, mask):
        lsg = _log_sigmoid(z)
        u = lsg - z
        if mask is not None:
            u = jnp.where(mask, u, 0.0)
        between = sb_carry[g] + _split_dot(u, later16(z.shape[1]))
        w = jnp.exp(lsg + between)
        if mask is not None:
            w = jnp.where(mask, w, 0.0)
        sb_acc[g] += pv_fn(w.astype(BF16))
        sb_carry[g] += jnp.sum(u, axis=1, keepdims=True)

    def sb_past(g, qs, blk):
        psk, psv = seq_pages(0, g), seq_pages(1, g)
        hi = n_pages - 1 - 2 * blk
        kt = jnp.concatenate([psk[hi - 1][...], psk[hi][...]], axis=1).astype(BF16)
        vt = jnp.concatenate([psv[hi - 1][...], psv[hi][...]], axis=1).astype(BF16)
        sb_block(g, _dot(qs, kt), lambda w: _dot_nt(w, vt), None)

    def sb_rest(g, qs, blk):
        if blk < n_pages // 2:
            @pl.when(jnp.max(sb_carry[g]) > SB_DEAD_LOG)
            def _():
                sb_past(g, qs, blk)
                sb_rest(g, qs, blk + 1)

    def stacked_queries(g):
        rows = slice(g * ls, (g + 1) * ls)
        qs = _stack_rows(q_ref[rows, :] * (HEAD_W ** -0.5), sb_masks).astype(BF16)
        qd = _stack_rows(dq_ref[rows, :] * (DIFF_DH ** -0.5), _combo_masks()).astype(BF16)
        return rows, qs, qd

    for g in range(group):
        rows, qs, qd = stacked_queries(g)
        sb_acc[g] = jnp.zeros(sb_acc.shape[1:], F32)
        sb_carry[g] = jnp.zeros(sb_carry.shape[1:], F32)
        sv_new = pad_page(sv_ref[rows, :])
        sb_block(g, _dot_nt(qs, pad_page(sk_ref[rows, :])), lambda w, sv_new=sv_new: _dot(w, sv_new),
                 key < t_sb)
        sb_past(g, qs, 0)

        s_new = jnp.where(key <= t_df, _dot_nt(qd, pad_page(dk_ref[rows, :])), MASK_VALUE)
        kt_all = jnp.concatenate([r[...].astype(BF16) for r in seq_pages(2, g)], axis=1)
        s_past = _dot(qd, kt_all)
        m = jnp.maximum(jnp.max(s_new, axis=1, keepdims=True), jnp.max(s_past, axis=1, keepdims=True))
        p_new = jnp.exp(s_new - m)
        p_past = jnp.exp(s_past - m)
        l = jnp.sum(p_new, axis=1, keepdims=True) + jnp.sum(p_past, axis=1, keepdims=True)
        vt_all = jnp.concatenate([r[...].astype(BF16) for r in seq_pages(3, g)], axis=1)
        on = (_dot(p_new.astype(BF16), pad_page(dv_ref[rows, :]))
              + _dot_nt(p_past.astype(BF16), vt_all)) / l
        o = jnp.zeros((ls, MIX_W), F32)
        for h in range(HEADS):
            o1 = on[(2 * h) * ls:(2 * h + 1) * ls, :]
            o2 = on[(2 * h + 1) * ls:(2 * h + 2) * ls, :]
            o = jnp.where(head == h, o1 - lam * o2, o)
        odf_ref[rows, :] = _head_rms(o, gain_ref[...], _block_diag16()) * (1.0 - lam_init)

    for g in range(group):
        rows, qs, _ = stacked_queries(g)
        sb_rest(g, qs, 1)
        acc = sb_acc[g]
        o = jnp.zeros((ls, MIX_W), F32)
        for h in range(HEADS):
            o = jnp.where(head == h, acc[h * ls:(h + 1) * ls, :], o)
        osb_ref[rows, :] = o


def _sample_attn(proj32, page_table, caches, nseq, ls, lam_p, gain, layer):
    n_pages = page_table.shape[1]
    assert n_pages % 2 == 0
    group = math.gcd(nseq, 2)

    def slab(c):
        return pl.BlockSpec((group * ls, MIX_W), lambda b, pt: (b, c))

    def page(g, p):
        return pl.BlockSpec((None, None, MIX_W, PAGE_SIZE),
                            lambda b, pt: (layer, pt[b * group + g, p], 0, 0))

    out = pl.BlockSpec((group * ls, MIX_W), lambda b, pt: (b, 0))
    grid_spec = pltpu.PrefetchScalarGridSpec(
        num_scalar_prefetch=1,
        grid=(nseq // group,),
        in_specs=[slab(S_SQ), slab(S_SK), slab(S_SV), slab(S_DQ), slab(S_DK), slab(S_DV)]
                 + [page(g, p) for _ in range(4) for g in range(group) for p in range(n_pages)]
                 + [pl.BlockSpec(lam_p.shape, lambda b, pt: (0, 0)),
                    pl.BlockSpec((1, MIX_W), lambda b, pt: (0, 0))],
        out_specs=[out, out],
        scratch_shapes=[pltpu.VMEM((group, HEADS * ls, MIX_W), F32),
                        pltpu.VMEM((group, HEADS * ls, 1), F32)])
    page_args = [c for c in caches for _ in range(group * n_pages)]
    return pl.pallas_call(
        functools.partial(_sample_attn_kernel, ls=ls, n_pages=n_pages, layer=layer, group=group),
        grid_spec=grid_spec,
        out_shape=[jax.ShapeDtypeStruct((nseq * ls, MIX_W), F32)] * 2,
        compiler_params=_params("arbitrary"),
        name="sample_attn",
    )(page_table, *([proj32] * 6), *page_args, lam_p, gain)


def _merge_kernel(x_ref, b0_ref, b1_ref, b2_ref, b3_ref, pre_ref, wg_ref, wbr_ref, wo_ref,
                  post_ref, o_ref):
    x = x_ref[...]
    h = _rms(x, pre_ref[...]).astype(BF16)
    merged = None
    for i, b_ref in enumerate((b0_ref, b1_ref, b2_ref, b3_ref)):
        gate = jax.nn.sigmoid(_dot(h, wg_ref[:, i * D_MODEL:(i + 1) * D_MODEL]))
        term = gate * _dot(b_ref[...].astype(BF16), wbr_ref[i])
        merged = term if merged is None else merged + term
    o_ref[...] = x + _rms(_dot(merged.astype(BF16), wo_ref[...]), post_ref[...])


def _const_spec(shape):
    nd = len(shape)
    return pl.BlockSpec(shape, lambda i: (0,) * nd, pipeline_mode=pl.Buffered(1))


def _merge(x, branches, pre, wg16, wbr16, wo16, post):
    n = x.shape[0]
    tm = _row_tile(n, 256)
    tok = pl.BlockSpec((tm, D_MODEL), lambda i: (i, 0))
    br = pl.BlockSpec((tm, MIX_W), lambda i: (i, 0))
    return pl.pallas_call(
        _merge_kernel,
        grid=(n // tm,),
        in_specs=[tok, br, br, br, br, _const_spec((1, D_MODEL)), _const_spec(wg16.shape),
                  _const_spec(wbr16.shape), _const_spec(wo16.shape), _const_spec((1, D_MODEL))],
        out_specs=tok,
        out_shape=jax.ShapeDtypeStruct((n, D_MODEL), F32),
        compiler_params=_params("parallel"),
        name="merge",
    )(x, *branches, pre, wg16, wbr16, wo16, post)


def _cross_attend(q, mk, mv, channel_major):
    head = _lane_head()
    rows = q.shape[0]
    mk = mk.astype(BF16)
    mv = mv.astype(BF16)
    qs = _stack_rows(q * (HEAD_W ** -0.5), [head == h for h in range(HEADS)]).astype(BF16)
    s = _dot(qs, mk) if channel_major else _dot_nt(qs, mk)
    p = jnp.exp(s - jnp.max(s, axis=1, keepdims=True))
    p16 = p.astype(BF16)
    pv = (_dot_nt(p16, mv) if channel_major else _dot(p16, mv)) / jnp.sum(p, axis=1, keepdims=True)
    o = jnp.zeros(q.shape, F32)
    for h in range(HEADS):
        o = jnp.where(head == h, pv[h * rows:(h + 1) * rows, :], o)
    return o


def _cross_attn_kernel(q_ref, mk_ref, mv_ref, o_ref, *, group, seqlen):
    for j in range(group):
        rows = slice(j * seqlen, (j + 1) * seqlen)
        o_ref[rows, :] = _cross_attend(q_ref[rows, :], mk_ref[j], mv_ref[j], channel_major=True)


def _cross_attn(cq, mem_k, mem_v, layer, nseq, seqlen):
    group = math.gcd(nseq, 4)
    mem = pl.BlockSpec((None, group, MIX_W, N_MEM), lambda b: (layer, b, 0, 0))
    tok = pl.BlockSpec((group * seqlen, MIX_W), lambda b: (b, 0))
    return pl.pallas_call(
        functools.partial(_cross_attn_kernel, group=group, seqlen=seqlen),
        grid=(nseq // group,),
        in_specs=[tok, mem, mem],
        out_specs=tok,
        out_shape=jax.ShapeDtypeStruct(cq.shape, F32),
        compiler_params=_params("parallel"),
        name="cross_attn",
    )(cq, mem_k, mem_v)


def _post_kernel(x_ref, *refs, attend):
    if attend:
        cpre_ref, wq_ref, mk_ref, mv_ref, *refs = refs
        cq = _dot(_rms(x_ref[...], cpre_ref[...]).astype(BF16), wq_ref[...])
        co = _cross_attend(cq, mk_ref[...], mv_ref[...], channel_major=False)
    else:
        co_ref, *refs = refs
        co = co_ref[...]
    wco_ref, cpost_ref, mpre_ref, wup_ref, wdn_ref, mpost_ref, o_ref = refs
    x = x_ref[...] + _rms(_dot(co.astype(BF16), wco_ref[...]), cpost_ref[...])
    h = _rms(x, mpre_ref[...]).astype(BF16)
    u = jnp.square(jnp.maximum(_dot(h, wup_ref[...]), 0.0))
    o_ref[...] = x + _rms(_dot(u.astype(BF16), wdn_ref[...]), mpost_ref[...])


def _post(x, attn_args, wco16, cpost, mpre, wup16, wdn16, mpost):
    n = x.shape[0]
    tm = _row_tile(n, 256)
    tok = pl.BlockSpec((tm, D_MODEL), lambda i: (i, 0))
    attend = len(attn_args) > 1
    if attend:
        cpre, wq16, mk, mv, seqlen = attn_args
        tiles = seqlen // tm
        mem = pl.BlockSpec((None, N_MEM, MIX_W), lambda i: (i // tiles, 0, 0))
        attn_specs = [_const_spec((1, D_MODEL)), _const_spec(wq16.shape), mem, mem]
        attn_in = (cpre, wq16, mk, mv)
    else:
        attn_specs = [pl.BlockSpec((tm, MIX_W), lambda i: (i, 0))]
        attn_in = attn_args
    return pl.pallas_call(
        functools.partial(_post_kernel, attend=attend),
        grid=(n // tm,),
        in_specs=[tok] + attn_specs + [_const_spec(wco16.shape),
                  _const_spec((1, D_MODEL)), _const_spec((1, D_MODEL)), _const_spec(wup16.shape),
                  _const_spec(wdn16.shape), _const_spec((1, D_MODEL))],
        out_specs=tok,
        out_shape=jax.ShapeDtypeStruct((n, D_MODEL), F32),
        compiler_params=_params("parallel"),
        name="ca_out_mlp",
    )(x, *attn_in, wco16, cpost, mpre, wup16, wdn16, mpost)


def _trunk_layer(x, nseq, seqlen, rope_tab, mem_k, mem_v, cached_mem_layer, w, rec_fn, attn_fn,
                 emit_kv):
    proj32, *kv = _mixer_proj(x, w["mix_pre_norm"], w["w_in"], rope_tab,
                              (nseq, seqlen) if emit_kv else None)
    o_ret, st_ret, o_hgrn, st_hgrn, new_kv = rec_fn(proj32, kv)
    o_sb, o_diff = attn_fn(proj32, kv)
    x = _merge(x, (o_ret, o_sb, o_diff, o_hgrn), w["mix_pre_norm"], w["w_gate"], w["w_br"],
               w["w_out"], w["mix_post_norm"])
    if cached_mem_layer is None:
        assert seqlen % _row_tile(nseq * seqlen, 256) == 0
        attn_args = (w["ca_pre_norm"], w["w_ca_q"], mem_k, mem_v, seqlen)
    else:
        cq = _norm_proj(x, w["ca_pre_norm"], w["w_ca_q"])
        attn_args = (_cross_attn(cq, mem_k, mem_v, cached_mem_layer, nseq, seqlen),)
    x = _post(x, attn_args, w["w_ca_o"], w["ca_post_norm"], w["mlp_pre_norm"], w["w_mlp_up"],
              w["w_mlp_down"], w["mlp_post_norm"])
    return x, new_kv, st_ret, st_hgrn


def kernel(x_prompt, x_sample, mem_prompt, cache_sb_k, cache_sb_v, cache_diff_k, cache_diff_v, cache_mem_k, cache_mem_v, state_ret, state_hgrn, page_table, mix_pre_norm, mix_post_norm, ca_pre_norm, ca_post_norm, mlp_pre_norm, mlp_post_norm, mem_norm, w_in, w_gate, ret_norm, diff_lambda, diff_norm, hgrn_lb_logits, hgrn_norm, w_br_ret, w_br_sb, w_br_diff, w_br_hgrn, w_out, w_ca_q, w_ca_k, w_ca_v, w_ca_o, w_mlp_up, w_mlp_down):
    depth = w_in.shape[0]
    B, L, _ = x_prompt.shape
    DB, LS, _ = x_sample.shape
    n_pages = page_table.shape[1]
    past_len = n_pages * PAGE_SIZE

    tab_p = _rope_table(jnp.arange(L, dtype=jnp.int32))
    tab_s = jnp.tile(_rope_table(past_len + jnp.arange(LS, dtype=jnp.int32)), (256 // LS, 1))
    caches = [jnp.transpose(c, (0, 1, 3, 4, 2)).reshape(c.shape[0], c.shape[1], MIX_W, PAGE_SIZE)
              for c in (cache_sb_k, cache_sb_v, cache_diff_k, cache_diff_v)]
    mem_cache = [jnp.transpose(c, (0, 1, 3, 4, 2)).reshape(c.shape[0], c.shape[1], MIX_W, N_MEM)
                 for c in (cache_mem_k, cache_mem_v)]
    w_br = jnp.stack([w_br_ret, w_br_sb, w_br_diff, w_br_hgrn], axis=1).astype(BF16)
    w_ca_kv = jnp.concatenate([w_ca_k, w_ca_v], axis=-1).astype(BF16)
    diff_gain = jnp.tile(diff_norm, (1, HEADS))
    def lanes_cm(a):
        return jnp.broadcast_to(a.reshape(depth, HEADS, HEAD_W, 1), (depth, HEADS, HEAD_W, DB))
    ret_gain_cm, hgrn_gain_cm, lbl_cm = lanes_cm(ret_norm), lanes_cm(hgrn_norm), lanes_cm(hgrn_lb_logits)
    st_ret_cm = state_ret.transpose(0, 2, 3, 4, 1)
    st_hgrn_cm = state_hgrn.transpose(0, 2, 3, 4, 1)

    xp = x_prompt.reshape(B * L, D_MODEL)
    xs = x_sample.reshape(DB * LS, D_MODEL)
    mem = mem_prompt.reshape(B * N_MEM, D_MODEL)
    zero_state = jnp.zeros((B, MIX_W, MIX_W), F32)
    outs_p, outs_s = [], []
    for l in range(depth):
        row = lambda a: a[l][None, :]
        w = {
            "mix_pre_norm": row(mix_pre_norm), "mix_post_norm": row(mix_post_norm),
            "ca_pre_norm": row(ca_pre_norm), "ca_post_norm": row(ca_post_norm),
            "mlp_pre_norm": row(mlp_pre_norm), "mlp_post_norm": row(mlp_post_norm),
            "ret_norm": row(ret_norm), "hgrn_norm": row(hgrn_norm),
            "hgrn_lb_logits": hgrn_lb_logits,
            "w_in": w_in[l].astype(BF16), "w_gate": w_gate[l].astype(BF16), "w_br": w_br[l],
            "w_out": w_out[l].astype(BF16), "w_ca_q": w_ca_q[l].astype(BF16),
            "w_ca_o": w_ca_o[l].astype(BF16), "w_mlp_up": w_mlp_up[l].astype(BF16),
            "w_mlp_down": w_mlp_down[l].astype(BF16),
        }
        lam_p, dgain = diff_lambda[l], diff_gain[l][None, :]

        mkv = _norm_proj(mem, row(mem_norm), w_ca_kv[l])
        mk_p = mkv[:, :MIX_W].reshape(B, N_MEM, MIX_W)
        mv_p = mkv[:, MIX_W:].reshape(B, N_MEM, MIX_W)

        def rec_p(proj32, kv, l=l, w=w):
            o_ret, st_ret = _recurrence("ret", proj32, B, L, 256, w["ret_norm"], zero_state)
            o_hgrn, st_hgrn = _recurrence("hgrn", proj32, B, L, 32, w["hgrn_norm"], zero_state,
                                          hgrn_lb_logits, l, nsub=8)
            return o_ret, _bd_to_state(st_ret), o_hgrn, _bd_to_state(st_hgrn), list(kv[1:])

        def attn_p(proj32, kv, l=l, lam_p=lam_p, dgain=dgain):
            return (_sb_prompt(proj32, kv[0], B, L),
                    _diff_prompt(proj32, kv[0], B, L, lam_p, dgain, l))

        xp, kv_p, ret_p, hgrn_p = _trunk_layer(xp, B, L, tab_p, mk_p, mv_p, None, w, rec_p, attn_p,
                                               True)
        outs_p.append((kv_p, ret_p, hgrn_p, mk_p, mv_p))

        def rec_s(proj32, kv, l=l):
            del kv
            pcm = proj32.reshape(DB, LS, N_SLABS, HEADS, HEAD_W).transpose(2, 1, 3, 4, 0)
            o_ret, st_ret = _decode_recurrence("ret", pcm, ret_gain_cm[l], st_ret_cm[l])
            o_hgrn, st_hgrn = _decode_recurrence("hgrn", pcm, hgrn_gain_cm[l], st_hgrn_cm[l],
                                                 lbl_cm, l)
            rows = lambda o: o.transpose(3, 0, 1, 2).reshape(DB * LS, MIX_W)
            return (rows(o_ret), st_ret.transpose(3, 0, 1, 2),
                    rows(o_hgrn), st_hgrn.transpose(3, 0, 1, 2), [pcm[c] for c in KV_SLABS])

        def attn_s(proj32, kv, l=l, lam_p=lam_p, dgain=dgain):
            del kv
            return _sample_attn(proj32, page_table, caches, DB, LS, lam_p, dgain, l)

        xs, kv_s, ret_s, hgrn_s = _trunk_layer(xs, DB, LS, tab_s, mem_cache[0], mem_cache[1], l,
                                               w, rec_s, attn_s, False)
        outs_s.append((kv_s, ret_s, hgrn_s))

    def kv_prompt(i):
        a = jnp.stack([o[0][i] for o in outs_p], axis=0)
        return a.reshape(depth, B, HEADS, HEAD_W, L).transpose(0, 1, 4, 2, 3)

    def kv_sample(i):
        return jnp.stack([o[0][i] for o in outs_s], axis=0).transpose(0, 4, 1, 2, 3)

    def states(outs, i):
        return jnp.stack([o[i] for o in outs], axis=0)

    def memkv(i):
        return jnp.stack([o[i].reshape(B, N_MEM, HEADS, HEAD_W) for o in outs_p], axis=0)

    return (xp.reshape(B, L, D_MODEL), xs.reshape(DB, LS, D_MODEL),
            kv_prompt(0), kv_prompt(1), kv_prompt(2), kv_prompt(3),
            memkv(3), memkv(4), states(outs_p, 1), states(outs_p, 2),
            kv_sample(0), kv_sample(1), kv_sample(2), kv_sample(3),
            states(outs_s, 1), states(outs_s, 2))
```

```python
import functools
import math

import jax
import jax.numpy as jnp
import numpy as np
from jax import lax
from jax.experimental import pallas as pl
from jax.experimental.pallas import tpu as pltpu

F32 = jnp.float32
BF16 = jnp.bfloat16

D_MODEL = 1024
N_MEM = 256
HEADS = 4
HEAD_W = 64
MIX_W = HEADS * HEAD_W
DIFF_DH = 32
D_FF = 4 * D_MODEL
N_BRANCH = 4
PAGE_SIZE = 128
ROPE_THETA = 10000.0
NORM_EPS = 1e-6
MASK_VALUE = -1e30
LOG2_E = math.log2(math.e)
LANES = 128
SUBLANES = 8
assert MIX_W == 2 * LANES and 2 * HEAD_W == LANES
N_SLABS = 14
D_IN = N_SLABS * MIX_W
(S_RQ, S_RK, S_RV, S_RG, S_SQ, S_SK, S_SV, S_DQ, S_DK, S_DV,
 S_GQ, S_GF, S_GI, S_GG) = range(N_SLABS)
KV_SLABS = (S_SK, S_SV, S_DK, S_DV)

SB_DEAD_LOG = -104.0

WIDE_TOKEN_TILE = 512

V7X_VMEM_BYTES = 64 * 1024 * 1024
VMEM_LIMIT = (V7X_VMEM_BYTES * 7) // 8

_NT = (((1,), (1,)), ((), ()))
_TN = (((0,), (0,)), ((), ()))


def _params(*sem):
    return pltpu.CompilerParams(dimension_semantics=sem, vmem_limit_bytes=VMEM_LIMIT)


def _dot(a, b):
    return jnp.dot(a, b, preferred_element_type=F32)


def _dot_nt(a, b):
    return lax.dot_general(a, b, _NT, preferred_element_type=F32)


def _dot_tn(a, b):
    return lax.dot_general(a, b, _TN, preferred_element_type=F32)


def _split_dot(x, m16, terms=2, left=False):
    out = None
    r = x
    for _ in range(terms):
        p = r.astype(BF16)
        d = _dot(m16, p) if left else _dot(p, m16)
        out = d if out is None else out + d
        r = r - p.astype(F32)
    return out


def _rms(x, g):
    ms = jnp.mean(x * x, axis=-1, keepdims=True)
    return x * lax.rsqrt(ms + NORM_EPS) * g


def _lane_head(width=MIX_W):
    return lax.broadcasted_iota(jnp.int32, (1, width), 1) // HEAD_W


def _block_diag16():
    r = lax.broadcasted_iota(jnp.int32, (MIX_W, MIX_W), 0) // HEAD_W
    c = lax.broadcasted_iota(jnp.int32, (MIX_W, MIX_W), 1) // HEAD_W
    return jnp.where(r == c, 1.0, 0.0).astype(BF16)


def _head_rms(o, gain, bd16):
    ms = _split_dot(o * o, bd16) * (1.0 / HEAD_W)
    return o * lax.rsqrt(ms + NORM_EPS) * gain


def _log_sigmoid(z):
    return jnp.minimum(z, 0.0) - jnp.log(1.0 + jnp.exp(-jnp.abs(z)))


def _rope_slab(s, cos, sin, half):
    lane = lax.broadcasted_iota(jnp.int32, (1, MIX_W), 1)
    first = (lane % (2 * half)) < half
    partner = jnp.where(first, pltpu.roll(s, MIX_W - half, 1), pltpu.roll(s, half, 1))
    return s * cos + partner * sin


def _norm_proj_kernel(x_ref, g_ref, w_ref, o32_ref):
    h = _rms(x_ref[...], g_ref[...]).astype(BF16)
    o32_ref[...] = _dot(h, w_ref[...])


def _mixer_proj_kernel(x_ref, g_ref, w_ref, tab_ref, o32_ref, *kv_refs):
    h = _rms(x_ref[...], g_ref[...]).astype(BF16)
    y = _dot(h, w_ref[...])
    cos_r = tab_ref[:, 0 * MIX_W:1 * MIX_W]
    sin_r = tab_ref[:, 1 * MIX_W:2 * MIX_W]
    cos_d = tab_ref[:, 2 * MIX_W:3 * MIX_W]
    sin_d = tab_ref[:, 3 * MIX_W:4 * MIX_W]
    for c in range(N_SLABS):
        s = y[:, c * MIX_W:(c + 1) * MIX_W]
        if c == S_RQ:
            s = _rope_slab(s, cos_r, sin_r, HEAD_W // 2)
        elif c == S_RK:
            s = _rope_slab(s, cos_r, sin_r, HEAD_W // 2) * (HEAD_W ** -0.5)
        elif c in (S_DQ, S_DK):
            s = _rope_slab(s, cos_d, sin_d, DIFF_DH // 2)
        o32_ref[:, c * MIX_W:(c + 1) * MIX_W] = s
        if kv_refs and c in KV_SLABS:
            i = KV_SLABS.index(c)
            kv_refs[0][:, i * MIX_W:(i + 1) * MIX_W] = s.astype(BF16)
            kv_refs[1 + i][...] = s.T


def _row_tile(n, want):
    t = math.gcd(n, want)
    assert t % 8 == 0
    return t


def _norm_proj(x, gain, w16):
    n, d = x.shape
    wout = w16.shape[1]
    tm = _row_tile(n, 256)
    return pl.pallas_call(
        _norm_proj_kernel,
        grid=(n // tm,),
        in_specs=[pl.BlockSpec((tm, d), lambda i: (i, 0)),
                  pl.BlockSpec((1, d), lambda i: (0, 0)),
                  pl.BlockSpec((d, wout), lambda i: (0, 0))],
        out_specs=pl.BlockSpec((tm, wout), lambda i: (i, 0)),
        out_shape=jax.ShapeDtypeStruct((n, wout), F32),
        compiler_params=_params("parallel"),
        name="norm_proj",
    )(x, gain, w16)


def _mixer_proj(x, gain, w16, tab, kv_seqs=None):
    n, d = x.shape
    tm = _row_tile(n, 256)
    tab_blocks = tab.shape[0] // tm
    out_specs = [pl.BlockSpec((tm, D_IN), lambda i: (i, 0))]
    out_shape = [jax.ShapeDtypeStruct((n, D_IN), F32)]
    if kv_seqs is not None:
        nseq, seqlen = kv_seqs
        tiles = seqlen // tm
        nkv = len(KV_SLABS)
        out_specs += [pl.BlockSpec((tm, nkv * MIX_W), lambda i: (i, 0))]
        out_specs += [pl.BlockSpec((None, MIX_W, tm), lambda i: (i // tiles, 0, i % tiles))] * nkv
        out_shape += [jax.ShapeDtypeStruct((n, nkv * MIX_W), BF16)]
        out_shape += [jax.ShapeDtypeStruct((nseq, MIX_W, seqlen), F32)] * nkv
    return pl.pallas_call(
        _mixer_proj_kernel,
        grid=(n // tm,),
        in_specs=[pl.BlockSpec((tm, d), lambda i: (i, 0)),
                  pl.BlockSpec((1, d), lambda i: (0, 0)),
                  pl.BlockSpec((d, D_IN), lambda i: (0, 0)),
                  pl.BlockSpec((tm, 4 * MIX_W), lambda i: (i % tab_blocks, 0))],
        out_specs=out_specs,
        out_shape=out_shape,
        compiler_params=_params("parallel"),
        name="mixer_proj",
    )(x, gain, w16, tab)


def _rope_table(pos):
    def one(group):
        half = group // 2
        inv = ROPE_THETA ** (-jnp.arange(half, dtype=F32) * 2.0 / group)
        ang = pos.astype(F32)[:, None] * inv[None, :]
        cos, sin = jnp.cos(ang), jnp.sin(ang)
        reps = MIX_W // group
        return (jnp.tile(jnp.concatenate([cos, cos], -1), (1, reps)),
                jnp.tile(jnp.concatenate([-sin, sin], -1), (1, reps)))
    cr, sr = one(HEAD_W)
    cd, sd = one(DIFF_DH)
    return jnp.concatenate([cr, sr, cd, sd], axis=-1)


_RET_GAMMA = [1.0 - 2.0 ** (-5.0 - h) for h in range(HEADS)]
_RET_LOG_GAMMA = [float(np.log(g)) for g in _RET_GAMMA]


def _state_step(st, q_dec, k_dec, v, decay_row):
    inter = _dot_nt(q_dec.astype(BF16), st.astype(BF16))
    upd = _dot_tn(v.astype(BF16), k_dec.astype(BF16))
    r = lax.broadcasted_iota(jnp.int32, (MIX_W, MIX_W), 0) // HEAD_W
    c = lax.broadcasted_iota(jnp.int32, (MIX_W, MIX_W), 1) // HEAD_W
    return inter, st * decay_row + jnp.where(r == c, upd, 0.0)


def _retention_kernel(q_ref, k_ref, v_ref, g_ref, gain_ref, st0_ref, o_ref, st_ref, *, chunk):
    @pl.when(pl.program_id(1) == 0)
    def _():
        st_ref[...] = st0_ref[...]

    q, k, v = q_ref[...], k_ref[...], v_ref[...]
    head = _lane_head()
    lg = jnp.zeros((1, MIX_W), F32)
    for h in range(HEADS):
        lg = jnp.where(head == h, _RET_LOG_GAMMA[h], lg)
    steps = lax.broadcasted_iota(jnp.int32, (chunk, 1), 0).astype(F32) + 1.0
    b = steps * lg
    b_last = float(chunk) * lg
    o, st_new = _state_step(st_ref[0], q * jnp.exp(b), k * jnp.exp(b_last - b), v,
                            jnp.exp(b_last))
    st_ref[0] = st_new

    ri = lax.broadcasted_iota(jnp.int32, (chunk, chunk), 0)
    ci = lax.broadcasted_iota(jnp.int32, (chunk, chunk), 1)
    causal = ri >= ci
    dist = (ri - ci).astype(F32)
    k16, v16 = k.astype(BF16), v.astype(BF16)
    for h in range(HEADS):
        sc = _dot_nt(jnp.where(head == h, q, 0.0).astype(BF16), k16)
        dec = jnp.where(causal, jnp.exp(dist * _RET_LOG_GAMMA[h]), 0.0)
        o = o + jnp.where(head == h, _dot((sc * dec).astype(BF16), v16), 0.0)

    g = g_ref[...]
    o_ref[...] = _head_rms(o, gain_ref[...], _block_diag16()) * (g * jax.nn.sigmoid(g))


def _hgrn_kernel(q_ref, f_ref, v_ref, g_ref, gain_ref, lbl_ref, st0_ref, o_ref, st_ref,
                 kk_scr, b_scr, *, chunk, nsub, layer):
    @pl.when(pl.program_id(1) == 0)
    def _():
        st_ref[...] = st0_ref[...]

    lbl = lbl_ref[...]
    e = jnp.exp(lbl - jnp.max(lbl, axis=0, keepdims=True))
    lb_w = e / jnp.sum(e, axis=0, keepdims=True)
    lb = lb_w[0:1, :]
    for i in range(1, layer + 1):
        lb = lb + lb_w[i:i + 1, :]
    lb = lb - lb_w[0:1, :]

    z_all = f_ref[...]
    kk_scr[...] = (1.0 - lb) * jax.nn.sigmoid(-z_all)
    log2_f = jnp.log(lb + (1.0 - lb) * jax.nn.sigmoid(z_all)) * LOG2_E
    ri = lax.broadcasted_iota(jnp.int32, (chunk, chunk), 0)
    ci = lax.broadcasted_iota(jnp.int32, (chunk, chunk), 1)
    tril16 = jnp.where(ri >= ci, 1.0, 0.0).astype(BF16)
    bd16 = _block_diag16()
    rows8 = lax.broadcasted_iota(jnp.int32, (SUBLANES, 1), 0)
    groups = chunk // SUBLANES
    r_blk = lax.broadcasted_iota(jnp.int32, (MIX_W, MIX_W), 0) // HEAD_W
    c_blk = lax.broadcasted_iota(jnp.int32, (MIX_W, MIX_W), 1) // HEAD_W

    bs, q_decs, updates, decays = [], [], [], []
    for s in range(nsub):
        lo = s * chunk
        q, v, k = q_ref[lo:lo + chunk, :], v_ref[lo:lo + chunk, :], kk_scr[lo:lo + chunk, :]
        lf = log2_f[lo:lo + chunk, :]
        b = _split_dot(lf, tril16, terms=3, left=True)
        b_scr[lo:lo + chunk, :] = b
        b_last = b[chunk - 1:chunk, :]
        bs.append(b)
        q_decs.append((q * jnp.exp2(b)).astype(BF16))
        upd = _dot_tn(v.astype(BF16), (k * jnp.exp2(b_last - b)).astype(BF16))
        updates.append(jnp.where(r_blk == c_blk, upd, 0.0))
        decays.append(jnp.exp2(b_last))
    states = [st_ref[0]]
    for s in range(nsub):
        states.append(states[-1] * decays[s] + updates[s])
    st_ref[0] = states[-1]

    for s in range(nsub):
        lo = s * chunk
        q, b = q_ref[lo:lo + chunk, :], bs[s]
        inter = _dot_nt(q_decs[s], states[s].astype(BF16))
        qg = [q[r * SUBLANES:(r + 1) * SUBLANES, :] for r in range(groups)]
        bg = [b[r * SUBLANES:(r + 1) * SUBLANES, :] for r in range(groups)]
        slabs, where = [], []
        for j in range(chunk):
            kj = kk_scr[lo + j:lo + j + 1, :]
            bj = b_scr[lo + j:lo + j + 1, :]
            for r in range(j // SUBLANES, groups):
                first = r * SUBLANES
                t = qg[r] * kj * jnp.exp2(bg[r] - bj)
                slabs.append(jnp.where(rows8 + first >= j, t, 0.0) if first < j else t)
                where.append((j, r))
        pair_sum = _dot(jnp.concatenate(slabs, axis=0).astype(BF16), bd16)
        og = [inter[r * SUBLANES:(r + 1) * SUBLANES, :] for r in range(groups)]
        for i, (j, r) in enumerate(where):
            og[r] = og[r] + pair_sum[i * SUBLANES:(i + 1) * SUBLANES, :] * v_ref[lo + j:lo + j + 1, :]
        o = jnp.concatenate(og, axis=0) if groups > 1 else og[0]
        g = g_ref[lo:lo + chunk, :]
        o_ref[lo:lo + chunk, :] = _head_rms(o, gain_ref[...], bd16) * (g * jax.nn.sigmoid(g))


def _slab_spec(rows, slab, nchunk):
    return pl.BlockSpec((rows, MIX_W), lambda s, c: (s * nchunk + c, slab))


def _recurrence(kind, proj32, nseq, seqlen, chunk, gain, st0, lb_logits=None, layer=0, nsub=1):
    rows = chunk * nsub
    nstep = seqlen // rows
    n = nseq * seqlen
    row = pl.BlockSpec((1, MIX_W), lambda s, c: (0, 0))
    st_spec = pl.BlockSpec((1, MIX_W, MIX_W), lambda s, c: (s, 0, 0))
    out_specs = [pl.BlockSpec((rows, MIX_W), lambda s, c: (s * nstep + c, 0)), st_spec]
    out_shape = [jax.ShapeDtypeStruct((n, MIX_W), F32),
                 jax.ShapeDtypeStruct((nseq, MIX_W, MIX_W), F32)]
    if kind == "ret":
        assert nsub == 1
        slabs = (S_RQ, S_RK, S_RV, S_RG)
        body = functools.partial(_retention_kernel, chunk=chunk)
        extra_specs, extra, scratch = [row, st_spec], (gain, st0), []
    else:
        slabs = (S_GQ, S_GF, S_GI, S_GG)
        body = functools.partial(_hgrn_kernel, chunk=chunk, nsub=nsub, layer=layer)
        extra_specs = [row, pl.BlockSpec(lb_logits.shape, lambda s, c: (0, 0)), st_spec]
        extra = (gain, lb_logits, st0)
        scratch = [pltpu.VMEM((rows, MIX_W), F32), pltpu.VMEM((rows, MIX_W), F32)]
    return pl.pallas_call(
        body,
        grid=(nseq, nstep),
        in_specs=[_slab_spec(rows, s, nstep) for s in slabs] + extra_specs,
        out_specs=out_specs,
        out_shape=out_shape,
        scratch_shapes=scratch,
        compiler_params=_params("parallel", "arbitrary"),
        name="recurrence_" + kind,
    )(proj32, proj32, proj32, proj32, *extra)


def _decode_rec_kernel(q_ref, k_ref, v_ref, g_ref, gain_ref, *rest, per_channel, ls, layer):
    if per_channel:
        lbl_ref, s0_ref, o_ref, s_ref, f_scr, kk_scr, oacc_scr = rest
        lbl = lbl_ref[...]
        e = jnp.exp(lbl - jnp.max(lbl, axis=0, keepdims=True))
        lb_w = e / jnp.sum(e, axis=0, keepdims=True)
        lb = lb_w[0]
        for i in range(1, layer + 1):
            lb = lb + lb_w[i]
        lb = lb - lb_w[0]
        for t in range(ls):
            z = k_ref[t]
            f_scr[t] = lb + (1.0 - lb) * jax.nn.sigmoid(z)
            kk_scr[t] = (1.0 - lb) * jax.nn.sigmoid(-z)
        key_ref = kk_scr
    else:
        s0_ref, o_ref, s_ref, oacc_scr = rest
        h = pl.program_id(0)
        gamma = jnp.float32(_RET_GAMMA[0])
        for i in range(1, HEADS):
            gamma = jnp.where(h == i, jnp.float32(_RET_GAMMA[i]), gamma)
        key_ref = k_ref
    oacc_scr[...] = jnp.zeros_like(oacc_scr)

    def channel(c, carry):
        s = s0_ref[c]
        for t in range(ls):
            f = f_scr[t, pl.ds(c, 1), :] if per_channel else gamma
            s = f * s + key_ref[t, pl.ds(c, 1), :] * v_ref[t]
            oacc_scr[t] += q_ref[t, pl.ds(c, 1), :] * s
        s_ref[c] = s
        return carry

    lax.fori_loop(0, HEAD_W, channel, 0)
    for t in range(ls):
        o = oacc_scr[t]
        ms = jnp.mean(o * o, axis=0, keepdims=True)
        g = g_ref[t]
        o_ref[t] = o * lax.rsqrt(ms + NORM_EPS) * gain_ref[...] * (g * jax.nn.sigmoid(g))


def _decode_recurrence(kind, proj_cm, gain_cm, s0_cm, lbl_cm=None, layer=0):
    _, ls, _, _, nb = proj_cm.shape
    per_channel = kind == "hgrn"
    slabs = (S_GQ, S_GF, S_GI, S_GG) if per_channel else (S_RQ, S_RK, S_RV, S_RG)

    def act(slab):
        return pl.BlockSpec((None, ls, None, HEAD_W, nb), lambda h: (slab, 0, h, 0, 0))

    st_spec = pl.BlockSpec((None, HEAD_W, HEAD_W, nb), lambda h: (h, 0, 0, 0))
    in_specs = [act(s) for s in slabs] + [pl.BlockSpec((None, HEAD_W, nb), lambda h: (h, 0, 0))]
    args = [proj_cm] * 4 + [gain_cm]
    scratch = [pltpu.VMEM((ls, HEAD_W, nb), F32)]
    if per_channel:
        in_specs.append(pl.BlockSpec((lbl_cm.shape[0], None, HEAD_W, nb), lambda h: (0, h, 0, 0)))
        args.append(lbl_cm)
        scratch = [pltpu.VMEM((ls, HEAD_W, nb), F32)] * 3
    return pl.pallas_call(
        functools.partial(_decode_rec_kernel, per_channel=per_channel, ls=ls, layer=layer),
        grid=(HEADS,),
        in_specs=in_specs + [st_spec],
        out_specs=[pl.BlockSpec((ls, None, HEAD_W, nb), lambda h: (0, h, 0, 0)), st_spec],
        out_shape=[jax.ShapeDtypeStruct((ls, HEADS, HEAD_W, nb), F32),
                   jax.ShapeDtypeStruct(s0_cm.shape, F32)],
        scratch_shapes=scratch,
        compiler_params=_params("parallel"),
        name="decode_rec_" + kind,
    )(*args, s0_cm)


def _bd_to_state(bd):
    b = bd.shape[0]
    r = bd.reshape(b, HEADS, HEAD_W, HEADS, HEAD_W)
    st = jnp.stack([r[:, h, :, h, :] for h in range(HEADS)], axis=1)
    return jnp.swapaxes(st, 2, 3)


def _sb_prompt_kernel(q_ref, k_ref, v_ref, o_ref, qs_ref, carry_ref, acc_ref, *, tq, tk):
    qi = pl.program_id(1)
    head = _lane_head()
    q = q_ref[...].astype(F32) * (HEAD_W ** -0.5)
    for h in range(HEADS):
        qs_ref[h * tq:(h + 1) * tq, :] = jnp.where(head == h, q, 0.0).astype(BF16)
    ri = lax.broadcasted_iota(jnp.int32, (tk, tk), 0)
    ci = lax.broadcasted_iota(jnp.int32, (tk, tk), 1)
    later16 = jnp.where(ri > ci, 1.0, 0.0).astype(BF16)
    reps = tk // LANES

    def block(kb, mask):
        start = pl.multiple_of(kb * tk, tk)
        z = _dot_nt(qs_ref[...], k_ref[pl.ds(start, tk), :])
        ls = _log_sigmoid(z)
        u = ls - z
        if mask is not None:
            u = jnp.where(mask, u, 0.0)
        between = jnp.concatenate([carry_ref[...]] * reps, axis=1) + _split_dot(u, later16)
        w = jnp.exp(ls + between)
        if mask is not None:
            w = jnp.where(mask, w, 0.0)
        acc_ref[...] += _dot(w.astype(BF16), v_ref[pl.ds(start, tk), :])
        carry = carry_ref[...] + jnp.sum(u, axis=1, keepdims=True)
        carry_ref[...] = carry
        return jnp.max(carry)

    acc_ref[...] = jnp.zeros_like(acc_ref)
    carry_ref[...] = jnp.zeros_like(carry_ref)
    qpos = qi * tq + lax.broadcasted_iota(jnp.int32, (HEADS * tq, 1), 0) % tq
    diag = qi * tq // tk
    kpos = diag * tk + lax.broadcasted_iota(jnp.int32, (1, tk), 1)
    alive = block(diag, kpos < qpos)

    def cond(state):
        return jnp.logical_and(state[0] >= 0, state[1] > SB_DEAD_LOG)

    def body(state):
        return state[0] - 1, block(state[0], None)

    lax.while_loop(cond, body, (diag - 1, alive))
    acc = acc_ref[...]
    o = jnp.zeros((tq, MIX_W), F32)
    for h in range(HEADS):
        o = jnp.where(head == h, acc[h * tq:(h + 1) * tq, :], o)
    o_ref[...] = o


def _sb_prompt(proj32, kv16, nseq, seqlen):
    tq = tk = 256
    nq = seqlen // tq
    return pl.pallas_call(
        functools.partial(_sb_prompt_kernel, tq=tq, tk=tk),
        grid=(nseq, nq),
        in_specs=[pl.BlockSpec((tq, MIX_W), lambda b, i: (b * nq + i, S_SQ)),
                  pl.BlockSpec((seqlen, MIX_W), lambda b, i: (b, KV_SLABS.index(S_SK))),
                  pl.BlockSpec((seqlen, MIX_W), lambda b, i: (b, KV_SLABS.index(S_SV)))],
        out_specs=pl.BlockSpec((tq, MIX_W), lambda b, i: (b * nq + i, 0)),
        out_shape=jax.ShapeDtypeStruct((nseq * seqlen, MIX_W), F32),
        scratch_shapes=[pltpu.VMEM((HEADS * tq, MIX_W), BF16),
                        pltpu.VMEM((HEADS * tq, LANES), F32),
                        pltpu.VMEM((HEADS * tq, MIX_W), F32)],
        compiler_params=_params("parallel", "arbitrary"),
        name="sb_prompt",
    )(proj32, kv16, kv16)


def _diff_lambda(lam_ref, layer):
    lp = lam_ref[...]
    lam_init = 0.8 - 0.6 * math.exp(-0.3 * layer)
    lam = (jnp.exp(jnp.sum(lp[0:1, :] * lp[1:2, :], axis=1, keepdims=True))
           - jnp.exp(jnp.sum(lp[2:3, :] * lp[3:4, :], axis=1, keepdims=True)) + lam_init)
    return lam, lam_init


def _combo_masks():
    lane = lax.broadcasted_iota(jnp.int32, (1, MIX_W), 1)
    return [(lane // DIFF_DH) == (2 * (c // 2) + (c % 2)) for c in range(2 * HEADS)]


def _diff_prompt_kernel(q_ref, k_ref, v_ref, lam_ref, gain_ref, o_ref, qs_ref, m_ref, acc_ref,
                        *, tq, tk, layer):
    qi = pl.program_id(1)
    head = _lane_head()
    q = q_ref[...] * (DIFF_DH ** -0.5 * LOG2_E)
    for c, mk in enumerate(_combo_masks()):
        qs_ref[c * tq:(c + 1) * tq, :] = jnp.where(mk, q, 0.0).astype(BF16)
    m_ref[...] = jnp.full(m_ref.shape, MASK_VALUE, F32)
    acc_ref[...] = jnp.zeros_like(acc_ref)
    qpos = qi * tq + lax.broadcasted_iota(jnp.int32, (tq, 1), 0)
    reps = tk // LANES

    def block(kb, masked):
        start = pl.multiple_of(kb * tk, tk)
        kblk = k_ref[pl.ds(start, tk), :]
        vblk = v_ref[pl.ds(start, tk), :]
        if masked:
            mask = (start + lax.broadcasted_iota(jnp.int32, (1, tk), 1)) <= qpos
        vaug = [jnp.where(head == h, vblk, jnp.ones_like(vblk)) for h in range(HEADS)]
        for c in range(2 * HEADS):
            rows = slice(c * tq, (c + 1) * tq)
            s = _dot_nt(qs_ref[rows, :], kblk)
            if masked:
                s = jnp.where(mask, s, MASK_VALUE)
            m_old = m_ref[rows, :]
            m_new = jnp.maximum(m_old, jnp.max(s, axis=1, keepdims=True))
            m_ref[rows, :] = m_new
            p = jnp.exp2(s - jnp.concatenate([m_new] * reps, axis=1)).astype(BF16)
            alpha = jnp.exp2(m_old - m_new)
            acc_ref[rows, :] = (jnp.concatenate([alpha] * (MIX_W // LANES), axis=1) * acc_ref[rows, :]
                                + _dot(p, vaug[c // 2]))

    def unmasked(kb, carry):
        block(kb, False)
        return carry

    diag = qi * tq // tk
    lax.fori_loop(0, diag, unmasked, 0)
    block(diag, True)

    lam, lam_init = _diff_lambda(lam_ref, layer)
    halves = []
    for half in range(MIX_W // LANES):
        mine = slice(half * LANES, (half + 1) * LANES)
        other = slice((1 - half) * LANES, (2 - half) * LANES)
        o = jnp.zeros((tq, LANES), F32)
        for h in range(half * 2, half * 2 + 2):
            r1 = slice(2 * h * tq, (2 * h + 1) * tq)
            r2 = slice((2 * h + 1) * tq, (2 * h + 2) * tq)
            o1 = acc_ref[r1, mine] / acc_ref[r1, other]
            o2 = acc_ref[r2, mine] / acc_ref[r2, other]
            o = jnp.where(head[:, mine] == h, o1 - lam * o2, o)
        halves.append(o)
    o = jnp.concatenate(halves, axis=1)
    o_ref[...] = _head_rms(o, gain_ref[...], _block_diag16()) * (1.0 - lam_init)


def _diff_prompt(proj32, kv16, nseq, seqlen, lam_p, gain, layer):
    tq = tk = 512
    nq = seqlen // tq
    return pl.pallas_call(
        functools.partial(_diff_prompt_kernel, tq=tq, tk=tk, layer=layer),
        grid=(nseq, nq),
        in_specs=[pl.BlockSpec((tq, MIX_W), lambda b, i: (b * nq + i, S_DQ)),
                  pl.BlockSpec((seqlen, MIX_W), lambda b, i: (b, KV_SLABS.index(S_DK))),
                  pl.BlockSpec((seqlen, MIX_W), lambda b, i: (b, KV_SLABS.index(S_DV))),
                  pl.BlockSpec(lam_p.shape, lambda b, i: (0, 0)),
                  pl.BlockSpec((1, MIX_W), lambda b, i: (0, 0))],
        out_specs=pl.BlockSpec((tq, MIX_W), lambda b, i: (b * nq + i, 0)),
        out_shape=jax.ShapeDtypeStruct((nseq * seqlen, MIX_W), F32),
        scratch_shapes=[pltpu.VMEM((2 * HEADS * tq, MIX_W), BF16),
                        pltpu.VMEM((2 * HEADS * tq, LANES), F32),
                        pltpu.VMEM((2 * HEADS * tq, MIX_W), F32)],
        compiler_params=_params("parallel", "arbitrary"),
        name="diff_prompt",
    )(proj32, kv16, kv16, lam_p, gain)


def _stack_rows(x, masks):
    return jnp.concatenate([jnp.where(mk, x, 0.0) for mk in masks], axis=0)


def _sample_attn_kernel(pt_ref, q_ref, sk_ref, sv_ref, dq_ref, dk_ref, dv_ref, *rest,
                        ls, n_pages, layer, group):
    del pt_ref
    n_ops = 4 * group * n_pages
    pages, (lam_ref, gain_ref, osb_ref, odf_ref, sb_acc, sb_carry) = rest[:n_ops], rest[n_ops:]
    head = _lane_head()
    sb_masks = [head == h for h in range(HEADS)]
    key = lax.broadcasted_iota(jnp.int32, (1, PAGE_SIZE), 1)
    t_sb = lax.broadcasted_iota(jnp.int32, (HEADS * ls, 1), 0) % ls
    t_df = lax.broadcasted_iota(jnp.int32, (2 * HEADS * ls, 1), 0) % ls
    lam, lam_init = _diff_lambda(lam_ref, layer)

    def seq_pages(cache, g):
        lo = (cache * group + g) * n_pages
        return pages[lo:lo + n_pages]

    def pad_page(x):
        return jnp.concatenate([x, jnp.zeros((PAGE_SIZE - ls, MIX_W), F32)], axis=0).astype(BF16)

    def later16(n):
        ri = lax.broadcasted_iota(jnp.int32, (n, n), 0)
        ci = lax.broadcasted_iota(jnp.int32, (n, n), 1)
        return jnp.where(ri > ci, 1.0, 0.0).astype(BF16)

    def sb_block(g, z, pv_fn, mask):
        lsg = _log_sigmoid(z)
        u = lsg - z
        if mask is not None:
            u = jnp.where(mask, u, 0.0)
        between = sb_carry[g] + _split_dot(u, later16(z.shape[1]))
        w = jnp.exp(lsg + between)
        if mask is not None:
            w = jnp.where(mask, w, 0.0)
        sb_acc[g] += pv_fn(w.astype(BF16))
        sb_carry[g] += jnp.sum(u, axis=1, keepdims=True)

    def sb_past(g, qs, blk):
        psk, psv = seq_pages(0, g), seq_pages(1, g)
        hi = n_pages - 1 - 2 * blk
        kt = jnp.concatenate([psk[hi - 1][...], psk[hi][...]], axis=1).astype(BF16)
        vt = jnp.concatenate([psv[hi - 1][...], psv[hi][...]], axis=1).astype(BF16)
        sb_block(g, _dot(qs, kt), lambda w: _dot_nt(w, vt), None)

    def sb_rest(g, qs, blk):
        if blk < n_pages // 2:
            @pl.when(jnp.max(sb_carry[g]) > SB_DEAD_LOG)
            def _():
                sb_past(g, qs, blk)
                sb_rest(g, qs, blk + 1)

    def stacked_queries(g):
        rows = slice(g * ls, (g + 1) * ls)
        qs = _stack_rows(q_ref[rows, :] * (HEAD_W ** -0.5), sb_masks).astype(BF16)
        qd = _stack_rows(dq_ref[rows, :] * (DIFF_DH ** -0.5), _combo_masks()).astype(BF16)
        return rows, qs, qd

    for g in range(group):
        rows, qs, qd = stacked_queries(g)
        sb_acc[g] = jnp.zeros(sb_acc.shape[1:], F32)
        sb_carry[g] = jnp.zeros(sb_carry.shape[1:], F32)
        sv_new = pad_page(sv_ref[rows, :])
        sb_block(g, _dot_nt(qs, pad_page(sk_ref[rows, :])), lambda w, sv_new=sv_new: _dot(w, sv_new),
                 key < t_sb)
        sb_past(g, qs, 0)

        s_new = jnp.where(key <= t_df, _dot_nt(qd, pad_page(dk_ref[rows, :])), MASK_VALUE)
        kt_all = jnp.concatenate([r[...].astype(BF16) for r in seq_pages(2, g)], axis=1)
        s_past = _dot(qd, kt_all)
        m = jnp.maximum(jnp.max(s_new, axis=1, keepdims=True), jnp.max(s_past, axis=1, keepdims=True))
        p_new = jnp.exp(s_new - m)
        p_past = jnp.exp(s_past - m)
        l = jnp.sum(p_new, axis=1, keepdims=True) + jnp.sum(p_past, axis=1, keepdims=True)
        vt_all = jnp.concatenate([r[...].astype(BF16) for r in seq_pages(3, g)], axis=1)
        on = (_dot(p_new.astype(BF16), pad_page(dv_ref[rows, :]))
              + _dot_nt(p_past.astype(BF16), vt_all)) / l
        o = jnp.zeros((ls, MIX_W), F32)
        for h in range(HEADS):
            o1 = on[(2 * h) * ls:(2 * h + 1) * ls, :]
            o2 = on[(2 * h + 1) * ls:(2 * h + 2) * ls, :]
            o = jnp.where(head == h, o1 - lam * o2, o)
        odf_ref[rows, :] = _head_rms(o, gain_ref[...], _block_diag16()) * (1.0 - lam_init)

    for g in range(group):
        rows, qs, _ = stacked_queries(g)
        sb_rest(g, qs, 1)
        acc = sb_acc[g]
        o = jnp.zeros((ls, MIX_W), F32)
        for h in range(HEADS):
            o = jnp.where(head == h, acc[h * ls:(h + 1) * ls, :], o)
        osb_ref[rows, :] = o


def _sample_attn(proj32, page_table, caches, nseq, ls, lam_p, gain, layer):
    n_pages = page_table.shape[1]
    assert n_pages % 2 == 0
    group = 1

    def slab(c):
        return pl.BlockSpec((group * ls, MIX_W), lambda b, pt: (b, c))

    def page(g, p):
        return pl.BlockSpec((None, None, MIX_W, PAGE_SIZE),
                            lambda b, pt: (layer, pt[b * group + g, p], 0, 0))

    out = pl.BlockSpec((group * ls, MIX_W), lambda b, pt: (b, 0))
    grid_spec = pltpu.PrefetchScalarGridSpec(
        num_scalar_prefetch=1,
        grid=(nseq // group,),
        in_specs=[slab(S_SQ), slab(S_SK), slab(S_SV), slab(S_DQ), slab(S_DK), slab(S_DV)]
                 + [page(g, p) for _ in range(4) for g in range(group) for p in range(n_pages)]
                 + [pl.BlockSpec(lam_p.shape, lambda b, pt: (0, 0)),
                    pl.BlockSpec((1, MIX_W), lambda b, pt: (0, 0))],
        out_specs=[out, out],
        scratch_shapes=[pltpu.VMEM((group, HEADS * ls, MIX_W), F32),
                        pltpu.VMEM((group, HEADS * ls, 1), F32)])
    page_args = [c for c in caches for _ in range(group * n_pages)]
    return pl.pallas_call(
        functools.partial(_sample_attn_kernel, ls=ls, n_pages=n_pages, layer=layer, group=group),
        grid_spec=grid_spec,
        out_shape=[jax.ShapeDtypeStruct((nseq * ls, MIX_W), F32)] * 2,
        compiler_params=_params("arbitrary"),
        name="sample_attn",
    )(page_table, *([proj32] * 6), *page_args, lam_p, gain)


def _merge_kernel(x_ref, b0_ref, b1_ref, b2_ref, b3_ref, pre_ref, wg_ref, wbr_ref, wo_ref,
                  post_ref, o_ref):
    x = x_ref[...]
    h = _rms(x, pre_ref[...]).astype(BF16)
    merged = None
    for i, b_ref in enumerate((b0_ref, b1_ref, b2_ref, b3_ref)):
        gate = jax.nn.sigmoid(_dot(h, wg_ref[:, i * D_MODEL:(i + 1) * D_MODEL]))
        term = gate * _dot(b_ref[...].astype(BF16), wbr_ref[i])
        merged = term if merged is None else merged + term
    o_ref[...] = x + _rms(_dot(merged.astype(BF16), wo_ref[...]), post_ref[...])


def _const_spec(shape):
    nd = len(shape)
    return pl.BlockSpec(shape, lambda i: (0,) * nd, pipeline_mode=pl.Buffered(1))


def _merge(x, branches, pre, wg16, wbr16, wo16, post):
    n = x.shape[0]
    tm = _row_tile(n, WIDE_TOKEN_TILE)
    tok = pl.BlockSpec((tm, D_MODEL), lambda i: (i, 0))
    br = pl.BlockSpec((tm, MIX_W), lambda i: (i, 0))
    return pl.pallas_call(
        _merge_kernel,
        grid=(n // tm,),
        in_specs=[tok, br, br, br, br, _const_spec((1, D_MODEL)), _const_spec(wg16.shape),
                  _const_spec(wbr16.shape), _const_spec(wo16.shape), _const_spec((1, D_MODEL))],
        out_specs=tok,
        out_shape=jax.ShapeDtypeStruct((n, D_MODEL), F32),
        compiler_params=_params("parallel"),
        name="merge",
    )(x, *branches, pre, wg16, wbr16, wo16, post)


def _cross_attend(q, mk, mv, channel_major):
    head = _lane_head()
    rows = q.shape[0]
    mk = mk.astype(BF16)
    mv = mv.astype(BF16)
    qs = _stack_rows(q * (HEAD_W ** -0.5), [head == h for h in range(HEADS)]).astype(BF16)
    s = _dot(qs, mk) if channel_major else _dot_nt(qs, mk)
    p = jnp.exp(s - jnp.max(s, axis=1, keepdims=True))
    p16 = p.astype(BF16)
    pv = (_dot_nt(p16, mv) if channel_major else _dot(p16, mv)) / jnp.sum(p, axis=1, keepdims=True)
    o = jnp.zeros(q.shape, F32)
    for h in range(HEADS):
        o = jnp.where(head == h, pv[h * rows:(h + 1) * rows, :], o)
    return o


def _cross_attn_kernel(q_ref, mk_ref, mv_ref, o_ref, *, group, seqlen):
    for j in range(group):
        rows = slice(j * seqlen, (j + 1) * seqlen)
        o_ref[rows, :] = _cross_attend(q_ref[rows, :], mk_ref[j], mv_ref[j], channel_major=True)


def _cross_attn(cq, mem_k, mem_v, layer, nseq, seqlen):
    group = math.gcd(nseq, 4)
    mem = pl.BlockSpec((None, group, MIX_W, N_MEM), lambda b: (layer, b, 0, 0))
    tok = pl.BlockSpec((group * seqlen, MIX_W), lambda b: (b, 0))
    return pl.pallas_call(
        functools.partial(_cross_attn_kernel, group=group, seqlen=seqlen),
        grid=(nseq // group,),
        in_specs=[tok, mem, mem],
        out_specs=tok,
        out_shape=jax.ShapeDtypeStruct(cq.shape, F32),
        compiler_params=_params("parallel"),
        name="cross_attn",
    )(cq, mem_k, mem_v)


def _post_kernel(x_ref, *refs, attend):
    if attend:
        cpre_ref, wq_ref, mk_ref, mv_ref, *refs = refs
        cq = _dot(_rms(x_ref[...], cpre_ref[...]).astype(BF16), wq_ref[...])
        co = _cross_attend(cq, mk_ref[...], mv_ref[...], channel_major=False)
    else:
        co_ref, *refs = refs
        co = co_ref[...]
    wco_ref, cpost_ref, mpre_ref, wup_ref, wdn_ref, mpost_ref, o_ref = refs
    x = x_ref[...] + _rms(_dot(co.astype(BF16), wco_ref[...]), cpost_ref[...])
    h = _rms(x, mpre_ref[...]).astype(BF16)
    u = jnp.square(jnp.maximum(_dot(h, wup_ref[...]), 0.0))
    o_ref[...] = x + _rms(_dot(u.astype(BF16), wdn_ref[...]), mpost_ref[...])


def _post(x, attn_args, wco16, cpost, mpre, wup16, wdn16, mpost):
    n = x.shape[0]
    tm = _row_tile(n, WIDE_TOKEN_TILE)
    tok = pl.BlockSpec((tm, D_MODEL), lambda i: (i, 0))
    attend = len(attn_args) > 1
    if attend:
        cpre, wq16, mk, mv, seqlen = attn_args
        tiles = seqlen // tm
        mem = pl.BlockSpec((None, N_MEM, MIX_W), lambda i: (i // tiles, 0, 0))
        attn_specs = [_const_spec((1, D_MODEL)), _const_spec(wq16.shape), mem, mem]
        attn_in = (cpre, wq16, mk, mv)
    else:
        attn_specs = [pl.BlockSpec((tm, MIX_W), lambda i: (i, 0))]
        attn_in = attn_args
    return pl.pallas_call(
        functools.partial(_post_kernel, attend=attend),
        grid=(n // tm,),
        in_specs=[tok] + attn_specs + [_const_spec(wco16.shape),
                  _const_spec((1, D_MODEL)), _const_spec((1, D_MODEL)), _const_spec(wup16.shape),
                  _const_spec(wdn16.shape), _const_spec((1, D_MODEL))],
        out_specs=tok,
        out_shape=jax.ShapeDtypeStruct((n, D_MODEL), F32),
        compiler_params=_params("parallel"),
        name="ca_out_mlp",
    )(x, *attn_in, wco16, cpost, mpre, wup16, wdn16, mpost)


def _trunk_layer(x, nseq, seqlen, rope_tab, mem_k, mem_v, cached_mem_layer, w, rec_fn, attn_fn,
                 emit_kv):
    proj32, *kv = _mixer_proj(x, w["mix_pre_norm"], w["w_in"], rope_tab,
                              (nseq, seqlen) if emit_kv else None)
    o_ret, st_ret, o_hgrn, st_hgrn, new_kv = rec_fn(proj32, kv)
    o_sb, o_diff = attn_fn(proj32, kv)
    x = _merge(x, (o_ret, o_sb, o_diff, o_hgrn), w["mix_pre_norm"], w["w_gate"], w["w_br"],
               w["w_out"], w["mix_post_norm"])
    if cached_mem_layer is None:
        assert seqlen % _row_tile(nseq * seqlen, WIDE_TOKEN_TILE) == 0
        attn_args = (w["ca_pre_norm"], w["w_ca_q"], mem_k, mem_v, seqlen)
    else:
        cq = _norm_proj(x, w["ca_pre_norm"], w["w_ca_q"])
        attn_args = (_cross_attn(cq, mem_k, mem_v, cached_mem_layer, nseq, seqlen),)
    x = _post(x, attn_args, w["w_ca_o"], w["ca_post_norm"], w["mlp_pre_norm"], w["w_mlp_up"],
              w["w_mlp_down"], w["mlp_post_norm"])
    return x, new_kv, st_ret, st_hgrn


def kernel(x_prompt, x_sample, mem_prompt, cache_sb_k, cache_sb_v, cache_diff_k, cache_diff_v, cache_mem_k, cache_mem_v, state_ret, state_hgrn, page_table, mix_pre_norm, mix_post_norm, ca_pre_norm, ca_post_norm, mlp_pre_norm, mlp_post_norm, mem_norm, w_in, w_gate, ret_norm, diff_lambda, diff_norm, hgrn_lb_logits, hgrn_norm, w_br_ret, w_br_sb, w_br_diff, w_br_hgrn, w_out, w_ca_q, w_ca_k, w_ca_v, w_ca_o, w_mlp_up, w_mlp_down):
    depth = w_in.shape[0]
    B, L, _ = x_prompt.shape
    DB, LS, _ = x_sample.shape
    n_pages = page_table.shape[1]
    past_len = n_pages * PAGE_SIZE

    tab_p = _rope_table(jnp.arange(L, dtype=jnp.int32))
    tab_s = jnp.tile(_rope_table(past_len + jnp.arange(LS, dtype=jnp.int32)), (256 // LS, 1))
    caches = [jnp.transpose(c, (0, 1, 3, 4, 2)).reshape(c.shape[0], c.shape[1], MIX_W, PAGE_SIZE)
              for c in (cache_sb_k, cache_sb_v, cache_diff_k, cache_diff_v)]
    mem_cache = [jnp.transpose(c, (0, 1, 3, 4, 2)).reshape(c.shape[0], c.shape[1], MIX_W, N_MEM)
                 for c in (cache_mem_k, cache_mem_v)]
    w_br = jnp.stack([w_br_ret, w_br_sb, w_br_diff, w_br_hgrn], axis=1).astype(BF16)
    w_ca_kv = jnp.concatenate([w_ca_k, w_ca_v], axis=-1).astype(BF16)
    diff_gain = jnp.tile(diff_norm, (1, HEADS))
    def lanes_cm(a):
        return jnp.broadcast_to(a.reshape(depth, HEADS, HEAD_W, 1), (depth, HEADS, HEAD_W, DB))
    ret_gain_cm, hgrn_gain_cm, lbl_cm = lanes_cm(ret_norm), lanes_cm(hgrn_norm), lanes_cm(hgrn_lb_logits)
    st_ret_cm = state_ret.transpose(0, 2, 3, 4, 1)
    st_hgrn_cm = state_hgrn.transpose(0, 2, 3, 4, 1)

    xp = x_prompt.reshape(B * L, D_MODEL)
    xs = x_sample.reshape(DB * LS, D_MODEL)
    mem = mem_prompt.reshape(B * N_MEM, D_MODEL)
    zero_state = jnp.zeros((B, MIX_W, MIX_W), F32)
    outs_p, outs_s = [], []
    for l in range(depth):
        row = lambda a: a[l][None, :]
        w = {
            "mix_pre_norm": row(mix_pre_norm), "mix_post_norm": row(mix_post_norm),
            "ca_pre_norm": row(ca_pre_norm), "ca_post_norm": row(ca_post_norm),
            "mlp_pre_norm": row(mlp_pre_norm), "mlp_post_norm": row(mlp_post_norm),
            "ret_norm": row(ret_norm), "hgrn_norm": row(hgrn_norm),
            "hgrn_lb_logits": hgrn_lb_logits,
            "w_in": w_in[l].astype(BF16), "w_gate": w_gate[l].astype(BF16), "w_br": w_br[l],
            "w_out": w_out[l].astype(BF16), "w_ca_q": w_ca_q[l].astype(BF16),
            "w_ca_o": w_ca_o[l].astype(BF16), "w_mlp_up": w_mlp_up[l].astype(BF16),
            "w_mlp_down": w_mlp_down[l].astype(BF16),
        }
        lam_p, dgain = diff_lambda[l], diff_gain[l][None, :]

        mkv = _norm_proj(mem, row(mem_norm), w_ca_kv[l])
        mk_p = mkv[:, :MIX_W].reshape(B, N_MEM, MIX_W)
        mv_p = mkv[:, MIX_W:].reshape(B, N_MEM, MIX_W)

        def rec_p(proj32, kv, l=l, w=w):
            o_ret, st_ret = _recurrence("ret", proj32, B, L, 256, w["ret_norm"], zero_state)
            o_hgrn, st_hgrn = _recurrence("hgrn", proj32, B, L, 32, w["hgrn_norm"], zero_state,
                                          hgrn_lb_logits, l, nsub=8)
            return o_ret, _bd_to_state(st_ret), o_hgrn, _bd_to_state(st_hgrn), list(kv[1:])

        def attn_p(proj32, kv, l=l, lam_p=lam_p, dgain=dgain):
            return (_sb_prompt(proj32, kv[0], B, L),
                    _diff_prompt(proj32, kv[0], B, L, lam_p, dgain, l))

        xp, kv_p, ret_p, hgrn_p = _trunk_layer(xp, B, L, tab_p, mk_p, mv_p, None, w, rec_p, attn_p,
                                               True)
        outs_p.append((kv_p, ret_p, hgrn_p, mk_p, mv_p))

        def rec_s(proj32, kv, l=l):
            del kv
            pcm = proj32.reshape(DB, LS, N_SLABS, HEADS, HEAD_W).transpose(2, 1, 3, 4, 0)
            o_ret, st_ret = _decode_recurrence("ret", pcm, ret_gain_cm[l], st_ret_cm[l])
            o_hgrn, st_hgrn = _decode_recurrence("hgrn", pcm, hgrn_gain_cm[l], st_hgrn_cm[l],
                                                 lbl_cm, l)
            rows = lambda o: o.transpose(3, 0, 1, 2).reshape(DB * LS, MIX_W)
            return (rows(o_ret), st_ret.transpose(3, 0, 1, 2),
                    rows(o_hgrn), st_hgrn.transpose(3, 0, 1, 2), [pcm[c] for c in KV_SLABS])

        def attn_s(proj32, kv, l=l, lam_p=lam_p, dgain=dgain):
            del kv
            return _sample_attn(proj32, page_table, caches, DB, LS, lam_p, dgain, l)

        xs, kv_s, ret_s, hgrn_s = _trunk_layer(xs, DB, LS, tab_s, mem_cache[0], mem_cache[1], l,
                                               w, rec_s, attn_s, False)
        outs_s.append((kv_s, ret_s, hgrn_s))

    def kv_prompt(i):
        a = jnp.stack([o[0][i] for o in outs_p], axis=0)
        return a.reshape(depth, B, HEADS, HEAD_W, L).transpose(0, 1, 4, 2, 3)

    def kv_sample(i):
        return jnp.stack([o[0][i] for o in outs_s], axis=0).transpose(0, 4, 1, 2, 3)

    def states(outs, i):
        return jnp.stack([o[i] for o in outs], axis=0)

    def memkv(i):
        return jnp.stack([o[i].reshape(B, N_MEM, HEADS, HEAD_W) for o in outs_p], axis=0)

    return (xp.reshape(B, L, D_MODEL), xs.reshape(DB, LS, D_MODEL),
            kv_prompt(0), kv_prompt(1), kv_prompt(2), kv_prompt(3),
            memkv(3), memkv(4), states(outs_p, 1), states(outs_p, 2),
            kv_sample(0), kv_sample(1), kv_sample(2), kv_sample(3),
            states(outs_s, 1), states(outs_s, 2))
```

```python
import functools
import math

import jax
import jax.numpy as jnp
import numpy as np
from jax import lax
from jax.experimental import pallas as pl
from jax.experimental.pallas import tpu as pltpu

F32 = jnp.float32
BF16 = jnp.bfloat16

D_MODEL = 1024
N_MEM = 256
HEADS = 4
HEAD_W = 64
MIX_W = HEADS * HEAD_W
DIFF_DH = 32
D_FF = 4 * D_MODEL
N_BRANCH = 4
PAGE_SIZE = 128
ROPE_THETA = 10000.0
NORM_EPS = 1e-6
MASK_VALUE = -1e30
LOG2_E = math.log2(math.e)
LANES = 128
SUBLANES = 8
assert MIX_W == 2 * LANES and 2 * HEAD_W == LANES
N_SLABS = 14
D_IN = N_SLABS * MIX_W
(S_RQ, S_RK, S_RV, S_RG, S_SQ, S_SK, S_SV, S_DQ, S_DK, S_DV,
 S_GQ, S_GF, S_GI, S_GG) = range(N_SLABS)
KV_SLABS = (S_SK, S_SV, S_DK, S_DV)

SB_DEAD_LOG = -104.0

WIDE_TOKEN_TILE = 512

V7X_VMEM_BYTES = 64 * 1024 * 1024
VMEM_LIMIT = (V7X_VMEM_BYTES * 7) // 8

_NT = (((1,), (1,)), ((), ()))
_TN = (((0,), (0,)), ((), ()))


def _params(*sem):
    return pltpu.CompilerParams(dimension_semantics=sem, vmem_limit_bytes=VMEM_LIMIT)


def _dot(a, b):
    return jnp.dot(a, b, preferred_element_type=F32)


def _dot_nt(a, b):
    return lax.dot_general(a, b, _NT, preferred_element_type=F32)


def _dot_tn(a, b):
    return lax.dot_general(a, b, _TN, preferred_element_type=F32)


def _split_dot(x, m16, terms=2, left=False):
    out = None
    r = x
    for _ in range(terms):
        p = r.astype(BF16)
        d = _dot(m16, p) if left else _dot(p, m16)
        out = d if out is None else out + d
        r = r - p.astype(F32)
    return out


def _rms(x, g):
    ms = jnp.mean(x * x, axis=-1, keepdims=True)
    return x * lax.rsqrt(ms + NORM_EPS) * g


def _lane_head(width=MIX_W):
    return lax.broadcasted_iota(jnp.int32, (1, width), 1) // HEAD_W


def _block_diag16():
    r = lax.broadcasted_iota(jnp.int32, (MIX_W, MIX_W), 0) // HEAD_W
    c = lax.broadcasted_iota(jnp.int32, (MIX_W, MIX_W), 1) // HEAD_W
    return jnp.where(r == c, 1.0, 0.0).astype(BF16)


def _head_rms(o, gain, bd16):
    ms = _split_dot(o * o, bd16) * (1.0 / HEAD_W)
    return o * lax.rsqrt(ms + NORM_EPS) * gain


def _log_sigmoid(z):
    return jnp.minimum(z, 0.0) - jnp.log(1.0 + jnp.exp(-jnp.abs(z)))


def _rope_slab(s, cos, sin, half):
    lane = lax.broadcasted_iota(jnp.int32, (1, MIX_W), 1)
    first = (lane % (2 * half)) < half
    partner = jnp.where(first, pltpu.roll(s, MIX_W - half, 1), pltpu.roll(s, half, 1))
    return s * cos + partner * sin


def _norm_proj_kernel(x_ref, g_ref, w_ref, o32_ref):
    h = _rms(x_ref[...], g_ref[...]).astype(BF16)
    o32_ref[...] = _dot(h, w_ref[...])


def _mixer_proj_kernel(x_ref, g_ref, w_ref, tab_ref, o32_ref, *kv_refs):
    h = _rms(x_ref[...], g_ref[...]).astype(BF16)
    y = _dot(h, w_ref[...])
    cos_r = tab_ref[:, 0 * MIX_W:1 * MIX_W]
    sin_r = tab_ref[:, 1 * MIX_W:2 * MIX_W]
    cos_d = tab_ref[:, 2 * MIX_W:3 * MIX_W]
    sin_d = tab_ref[:, 3 * MIX_W:4 * MIX_W]
    for c in range(N_SLABS):
        s = y[:, c * MIX_W:(c + 1) * MIX_W]
        if c == S_RQ:
            s = _rope_slab(s, cos_r, sin_r, HEAD_W // 2)
        elif c == S_RK:
            s = _rope_slab(s, cos_r, sin_r, HEAD_W // 2) * (HEAD_W ** -0.5)
        elif c in (S_DQ, S_DK):
            s = _rope_slab(s, cos_d, sin_d, DIFF_DH // 2)
        o32_ref[:, c * MIX_W:(c + 1) * MIX_W] = s
        if kv_refs and c in KV_SLABS:
            i = KV_SLABS.index(c)
            kv_refs[0][:, i * MIX_W:(i + 1) * MIX_W] = s.astype(BF16)
            kv_refs[1 + i][...] = s.T


def _row_tile(n, want):
    t = math.gcd(n, want)
    assert t % 8 == 0
    return t


def _norm_proj(x, gain, w16):
    n, d = x.shape
    wout = w16.shape[1]
    tm = _row_tile(n, 256)
    return pl.pallas_call(
        _norm_proj_kernel,
        grid=(n // tm,),
        in_specs=[pl.BlockSpec((tm, d), lambda i: (i, 0)),
                  pl.BlockSpec((1, d), lambda i: (0, 0)),
                  pl.BlockSpec((d, wout), lambda i: (0, 0))],
        out_specs=pl.BlockSpec((tm, wout), lambda i: (i, 0)),
        out_shape=jax.ShapeDtypeStruct((n, wout), F32),
        compiler_params=_params("parallel"),
        name="norm_proj",
    )(x, gain, w16)


def _mixer_proj(x, gain, w16, tab, kv_seqs=None):
    n, d = x.shape
    tm = _row_tile(n, 256)
    tab_blocks = tab.shape[0] // tm
    out_specs = [pl.BlockSpec((tm, D_IN), lambda i: (i, 0))]
    out_shape = [jax.ShapeDtypeStruct((n, D_IN), F32)]
    if kv_seqs is not None:
        nseq, seqlen = kv_seqs
        tiles = seqlen // tm
        nkv = len(KV_SLABS)
        out_specs += [pl.BlockSpec((tm, nkv * MIX_W), lambda i: (i, 0))]
        out_specs += [pl.BlockSpec((None, MIX_W, tm), lambda i: (i // tiles, 0, i % tiles))] * nkv
        out_shape += [jax.ShapeDtypeStruct((n, nkv * MIX_W), BF16)]
        out_shape += [jax.ShapeDtypeStruct((nseq, MIX_W, seqlen), F32)] * nkv
    return pl.pallas_call(
        _mixer_proj_kernel,
        grid=(n // tm,),
        in_specs=[pl.BlockSpec((tm, d), lambda i: (i, 0)),
                  pl.BlockSpec((1, d), lambda i: (0, 0)),
                  pl.BlockSpec((d, D_IN), lambda i: (0, 0)),
                  pl.BlockSpec((tm, 4 * MIX_W), lambda i: (i % tab_blocks, 0))],
        out_specs=out_specs,
        out_shape=out_shape,
        compiler_params=_params("parallel"),
        name="mixer_proj",
    )(x, gain, w16, tab)


def _rope_table(pos):
    def one(group):
        half = group // 2
        inv = ROPE_THETA ** (-jnp.arange(half, dtype=F32) * 2.0 / group)
        ang = pos.astype(F32)[:, None] * inv[None, :]
        cos, sin = jnp.cos(ang), jnp.sin(ang)
        reps = MIX_W // group
        return (jnp.tile(jnp.concatenate([cos, cos], -1), (1, reps)),
                jnp.tile(jnp.concatenate([-sin, sin], -1), (1, reps)))
    cr, sr = one(HEAD_W)
    cd, sd = one(DIFF_DH)
    return jnp.concatenate([cr, sr, cd, sd], axis=-1)


_RET_GAMMA = [1.0 - 2.0 ** (-5.0 - h) for h in range(HEADS)]
_RET_LOG_GAMMA = [float(np.log(g)) for g in _RET_GAMMA]


def _state_step(st, q_dec, k_dec, v, decay_row):
    inter = _dot_nt(q_dec.astype(BF16), st.astype(BF16))
    upd = _dot_tn(v.astype(BF16), k_dec.astype(BF16))
    r = lax.broadcasted_iota(jnp.int32, (MIX_W, MIX_W), 0) // HEAD_W
    c = lax.broadcasted_iota(jnp.int32, (MIX_W, MIX_W), 1) // HEAD_W
    return inter, st * decay_row + jnp.where(r == c, upd, 0.0)


def _retention_kernel(q_ref, k_ref, v_ref, g_ref, gain_ref, st0_ref, o_ref, st_ref, *, chunk):
    @pl.when(pl.program_id(1) == 0)
    def _():
        st_ref[...] = st0_ref[...]

    q, k, v = q_ref[...], k_ref[...], v_ref[...]
    head = _lane_head()
    lg = jnp.zeros((1, MIX_W), F32)
    for h in range(HEADS):
        lg = jnp.where(head == h, _RET_LOG_GAMMA[h], lg)
    steps = lax.broadcasted_iota(jnp.int32, (chunk, 1), 0).astype(F32) + 1.0
    b = steps * lg
    b_last = float(chunk) * lg
    o, st_new = _state_step(st_ref[0], q * jnp.exp(b), k * jnp.exp(b_last - b), v,
                            jnp.exp(b_last))
    st_ref[0] = st_new

    ri = lax.broadcasted_iota(jnp.int32, (chunk, chunk), 0)
    ci = lax.broadcasted_iota(jnp.int32, (chunk, chunk), 1)
    causal = ri >= ci
    dist = (ri - ci).astype(F32)
    k16, v16 = k.astype(BF16), v.astype(BF16)
    for h in range(HEADS):
        sc = _dot_nt(jnp.where(head == h, q, 0.0).astype(BF16), k16)
        dec = jnp.where(causal, jnp.exp(dist * _RET_LOG_GAMMA[h]), 0.0)
        o = o + jnp.where(head == h, _dot((sc * dec).astype(BF16), v16), 0.0)

    g = g_ref[...]
    o_ref[...] = _head_rms(o, gain_ref[...], _block_diag16()) * (g * jax.nn.sigmoid(g))


def _hgrn_kernel(q_ref, f_ref, v_ref, g_ref, gain_ref, lbl_ref, st0_ref, o_ref, st_ref,
                 kk_scr, b_scr, *, chunk, nsub, layer):
    @pl.when(pl.program_id(1) == 0)
    def _():
        st_ref[...] = st0_ref[...]

    lbl = lbl_ref[...]
    e = jnp.exp(lbl - jnp.max(lbl, axis=0, keepdims=True))
    lb_w = e / jnp.sum(e, axis=0, keepdims=True)
    lb = lb_w[0:1, :]
    for i in range(1, layer + 1):
        lb = lb + lb_w[i:i + 1, :]
    lb = lb - lb_w[0:1, :]

    z_all = f_ref[...]
    kk_scr[...] = (1.0 - lb) * jax.nn.sigmoid(-z_all)
    log2_f = jnp.log(lb + (1.0 - lb) * jax.nn.sigmoid(z_all)) * LOG2_E
    ri = lax.broadcasted_iota(jnp.int32, (chunk, chunk), 0)
    ci = lax.broadcasted_iota(jnp.int32, (chunk, chunk), 1)
    tril16 = jnp.where(ri >= ci, 1.0, 0.0).astype(BF16)
    bd16 = _block_diag16()
    rows8 = lax.broadcasted_iota(jnp.int32, (SUBLANES, 1), 0)
    groups = chunk // SUBLANES
    r_blk = lax.broadcasted_iota(jnp.int32, (MIX_W, MIX_W), 0) // HEAD_W
    c_blk = lax.broadcasted_iota(jnp.int32, (MIX_W, MIX_W), 1) // HEAD_W

    bs, q_decs, updates, decays = [], [], [], []
    for s in range(nsub):
        lo = s * chunk
        q, v, k = q_ref[lo:lo + chunk, :], v_ref[lo:lo + chunk, :], kk_scr[lo:lo + chunk, :]
        lf = log2_f[lo:lo + chunk, :]
        b = _split_dot(lf, tril16, terms=3, left=True)
        b_scr[lo:lo + chunk, :] = b
        b_last = b[chunk - 1:chunk, :]
        bs.append(b)
        q_decs.append((q * jnp.exp2(b)).astype(BF16))
        upd = _dot_tn(v.astype(BF16), (k * jnp.exp2(b_last - b)).astype(BF16))
        updates.append(jnp.where(r_blk == c_blk, upd, 0.0))
        decays.append(jnp.exp2(b_last))
    states = [st_ref[0]]
    for s in range(nsub):
        states.append(states[-1] * decays[s] + updates[s])
    st_ref[0] = states[-1]

    for s in range(nsub):
        lo = s * chunk
        q, b = q_ref[lo:lo + chunk, :], bs[s]
        inter = _dot_nt(q_decs[s], states[s].astype(BF16))
        qg = [q[r * SUBLANES:(r + 1) * SUBLANES, :] for r in range(groups)]
        bg = [b[r * SUBLANES:(r + 1) * SUBLANES, :] for r in range(groups)]
        slabs, where = [], []
        for j in range(chunk):
            kj = kk_scr[lo + j:lo + j + 1, :]
            bj = b_scr[lo + j:lo + j + 1, :]
            for r in range(j // SUBLANES, groups):
                first = r * SUBLANES
                t = qg[r] * kj * jnp.exp2(bg[r] - bj)
                slabs.append(jnp.where(rows8 + first >= j, t, 0.0) if first < j else t)
                where.append((j, r))
        pair_sum = _dot(jnp.concatenate(slabs, axis=0).astype(BF16), bd16)
        og = [inter[r * SUBLANES:(r + 1) * SUBLANES, :] for r in range(groups)]
        for i, (j, r) in enumerate(where):
            og[r] = og[r] + pair_sum[i * SUBLANES:(i + 1) * SUBLANES, :] * v_ref[lo + j:lo + j + 1, :]
        o = jnp.concatenate(og, axis=0) if groups > 1 else og[0]
        g = g_ref[lo:lo + chunk, :]
        o_ref[lo:lo + chunk, :] = _head_rms(o, gain_ref[...], bd16) * (g * jax.nn.sigmoid(g))


def _slab_spec(rows, slab, nchunk):
    return pl.BlockSpec((rows, MIX_W), lambda s, c: (s * nchunk + c, slab))


def _recurrence(kind, proj32, nseq, seqlen, chunk, gain, st0, lb_logits=None, layer=0, nsub=1):
    rows = chunk * nsub
    nstep = seqlen // rows
    n = nseq * seqlen
    row = pl.BlockSpec((1, MIX_W), lambda s, c: (0, 0))
    st_spec = pl.BlockSpec((1, MIX_W, MIX_W), lambda s, c: (s, 0, 0))
    out_specs = [pl.BlockSpec((rows, MIX_W), lambda s, c: (s * nstep + c, 0)), st_spec]
    out_shape = [jax.ShapeDtypeStruct((n, MIX_W), F32),
                 jax.ShapeDtypeStruct((nseq, MIX_W, MIX_W), F32)]
    if kind == "ret":
        assert nsub == 1
        slabs = (S_RQ, S_RK, S_RV, S_RG)
        body = functools.partial(_retention_kernel, chunk=chunk)
        extra_specs, extra, scratch = [row, st_spec], (gain, st0), []
    else:
        slabs = (S_GQ, S_GF, S_GI, S_GG)
        body = functools.partial(_hgrn_kernel, chunk=chunk, nsub=nsub, layer=layer)
        extra_specs = [row, pl.BlockSpec(lb_logits.shape, lambda s, c: (0, 0)), st_spec]
        extra = (gain, lb_logits, st0)
        scratch = [pltpu.VMEM((rows, MIX_W), F32), pltpu.VMEM((rows, MIX_W), F32)]
    return pl.pallas_call(
        body,
        grid=(nseq, nstep),
        in_specs=[_slab_spec(rows, s, nstep) for s in slabs] + extra_specs,
        out_specs=out_specs,
        out_shape=out_shape,
        scratch_shapes=scratch,
        compiler_params=_params("parallel", "arbitrary"),
        name="recurrence_" + kind,
    )(proj32, proj32, proj32, proj32, *extra)


def _decode_rec_kernel(q_ref, k_ref, v_ref, g_ref, gain_ref, *rest, per_channel, ls, layer):
    if per_channel:
        lbl_ref, s0_ref, o_ref, s_ref, f_scr, kk_scr, oacc_scr = rest
        lbl = lbl_ref[...]
        e = jnp.exp(lbl - jnp.max(lbl, axis=0, keepdims=True))
        lb_w = e / jnp.sum(e, axis=0, keepdims=True)
        lb = lb_w[0]
        for i in range(1, layer + 1):
            lb = lb + lb_w[i]
        lb = lb - lb_w[0]
        for t in range(ls):
            z = k_ref[t]
            f_scr[t] = lb + (1.0 - lb) * jax.nn.sigmoid(z)
            kk_scr[t] = (1.0 - lb) * jax.nn.sigmoid(-z)
        key_ref = kk_scr
    else:
        s0_ref, o_ref, s_ref, oacc_scr = rest
        h = pl.program_id(0)
        gamma = jnp.float32(_RET_GAMMA[0])
        for i in range(1, HEADS):
            gamma = jnp.where(h == i, jnp.float32(_RET_GAMMA[i]), gamma)
        key_ref = k_ref
    oacc_scr[...] = jnp.zeros_like(oacc_scr)

    def channel(c, carry):
        s = s0_ref[c]
        for t in range(ls):
            f = f_scr[t, pl.ds(c, 1), :] if per_channel else gamma
            s = f * s + key_ref[t, pl.ds(c, 1), :] * v_ref[t]
            oacc_scr[t] += q_ref[t, pl.ds(c, 1), :] * s
        s_ref[c] = s
        return carry

    lax.fori_loop(0, HEAD_W, channel, 0)
    for t in range(ls):
        o = oacc_scr[t]
        ms = jnp.mean(o * o, axis=0, keepdims=True)
        g = g_ref[t]
        o_ref[t] = o * lax.rsqrt(ms + NORM_EPS) * gain_ref[...] * (g * jax.nn.sigmoid(g))


def _decode_recurrence(kind, proj_cm, gain_cm, s0_cm, lbl_cm=None, layer=0):
    _, ls, _, _, nb = proj_cm.shape
    per_channel = kind == "hgrn"
    slabs = (S_GQ, S_GF, S_GI, S_GG) if per_channel else (S_RQ, S_RK, S_RV, S_RG)

    def act(slab):
        return pl.BlockSpec((None, ls, None, HEAD_W, nb), lambda h: (slab, 0, h, 0, 0))

    st_spec = pl.BlockSpec((None, HEAD_W, HEAD_W, nb), lambda h: (h, 0, 0, 0))
    in_specs = [act(s) for s in slabs] + [pl.BlockSpec((None, HEAD_W, nb), lambda h: (h, 0, 0))]
    args = [proj_cm] * 4 + [gain_cm]
    scratch = [pltpu.VMEM((ls, HEAD_W, nb), F32)]
    if per_channel:
        in_specs.append(pl.BlockSpec((lbl_cm.shape[0], None, HEAD_W, nb), lambda h: (0, h, 0, 0)))
        args.append(lbl_cm)
        scratch = [pltpu.VMEM((ls, HEAD_W, nb), F32)] * 3
    return pl.pallas_call(
        functools.partial(_decode_rec_kernel, per_channel=per_channel, ls=ls, layer=layer),
        grid=(HEADS,),
        in_specs=in_specs + [st_spec],
        out_specs=[pl.BlockSpec((ls, None, HEAD_W, nb), lambda h: (0, h, 0, 0)), st_spec],
        out_shape=[jax.ShapeDtypeStruct((ls, HEADS, HEAD_W, nb), F32),
                   jax.ShapeDtypeStruct(s0_cm.shape, F32)],
        scratch_shapes=scratch,
        compiler_params=_params("parallel"),
        name="decode_rec_" + kind,
    )(*args, s0_cm)


def _bd_to_state(bd):
    b = bd.shape[0]
    r = bd.reshape(b, HEADS, HEAD_W, HEADS, HEAD_W)
    st = jnp.stack([r[:, h, :, h, :] for h in range(HEADS)], axis=1)
    return jnp.swapaxes(st, 2, 3)


def _sb_prompt_kernel(q_ref, k_ref, v_ref, o_ref, qs_ref, carry_ref, acc_ref, *, tq, tk):
    qi = pl.program_id(1)
    head = _lane_head()
    q = q_ref[...].astype(F32) * (HEAD_W ** -0.5)
    for h in range(HEADS):
        qs_ref[h * tq:(h + 1) * tq, :] = jnp.where(head == h, q, 0.0).astype(BF16)
    ri = lax.broadcasted_iota(jnp.int32, (tk, tk), 0)
    ci = lax.broadcasted_iota(jnp.int32, (tk, tk), 1)
    later16 = jnp.where(ri > ci, 1.0, 0.0).astype(BF16)
    reps = tk // LANES

    def block(kb, mask):
        start = pl.multiple_of(kb * tk, tk)
        z = _dot_nt(qs_ref[...], k_ref[pl.ds(start, tk), :])
        ls = _log_sigmoid(z)
        u = ls - z
        if mask is not None:
            u = jnp.where(mask, u, 0.0)
        between = jnp.concatenate([carry_ref[...]] * reps, axis=1) + _split_dot(u, later16)
        w = jnp.exp(ls + between)
        if mask is not None:
            w = jnp.where(mask, w, 0.0)
        acc_ref[...] += _dot(w.astype(BF16), v_ref[pl.ds(start, tk), :])
        carry = carry_ref[...] + jnp.sum(u, axis=1, keepdims=True)
        carry_ref[...] = carry
        return jnp.max(carry)

    acc_ref[...] = jnp.zeros_like(acc_ref)
    carry_ref[...] = jnp.zeros_like(carry_ref)
    qpos = qi * tq + lax.broadcasted_iota(jnp.int32, (HEADS * tq, 1), 0) % tq
    diag = qi * tq // tk
    kpos = diag * tk + lax.broadcasted_iota(jnp.int32, (1, tk), 1)
    alive = block(diag, kpos < qpos)

    def cond(state):
        return jnp.logical_and(state[0] >= 0, state[1] > SB_DEAD_LOG)

    def body(state):
        return state[0] - 1, block(state[0], None)

    lax.while_loop(cond, body, (diag - 1, alive))
    acc = acc_ref[...]
    o = jnp.zeros((tq, MIX_W), F32)
    for h in range(HEADS):
        o = jnp.where(head == h, acc[h * tq:(h + 1) * tq, :], o)
    o_ref[...] = o


def _sb_prompt(proj32, kv16, nseq, seqlen):
    tq = tk = 256
    nq = seqlen // tq
    return pl.pallas_call(
        functools.partial(_sb_prompt_kernel, tq=tq, tk=tk),
        grid=(nseq, nq),
        in_specs=[pl.BlockSpec((tq, MIX_W), lambda b, i: (b * nq + i, S_SQ)),
                  pl.BlockSpec((seqlen, MIX_W), lambda b, i: (b, KV_SLABS.index(S_SK))),
                  pl.BlockSpec((seqlen, MIX_W), lambda b, i: (b, KV_SLABS.index(S_SV)))],
        out_specs=pl.BlockSpec((tq, MIX_W), lambda b, i: (b * nq + i, 0)),
        out_shape=jax.ShapeDtypeStruct((nseq * seqlen, MIX_W), F32),
        scratch_shapes=[pltpu.VMEM((HEADS * tq, MIX_W), BF16),
                        pltpu.VMEM((HEADS * tq, LANES), F32),
                        pltpu.VMEM((HEADS * tq, MIX_W), F32)],
        compiler_params=_params("parallel", "arbitrary"),
        name="sb_prompt",
    )(proj32, kv16, kv16)


def _diff_lambda(lam_ref, layer):
    lp = lam_ref[...]
    lam_init = 0.8 - 0.6 * math.exp(-0.3 * layer)
    lam = (jnp.exp(jnp.sum(lp[0:1, :] * lp[1:2, :], axis=1, keepdims=True))
           - jnp.exp(jnp.sum(lp[2:3, :] * lp[3:4, :], axis=1, keepdims=True)) + lam_init)
    return lam, lam_init


def _combo_masks():
    lane = lax.broadcasted_iota(jnp.int32, (1, MIX_W), 1)
    return [(lane // DIFF_DH) == (2 * (c // 2) + (c % 2)) for c in range(2 * HEADS)]


def _diff_prompt_kernel(q_ref, k_ref, v_ref, lam_ref, gain_ref, o_ref, qs_ref, m_ref, acc_ref,
                        *, tq, tk, layer):
    qi = pl.program_id(1)
    head = _lane_head()
    q = q_ref[...] * (DIFF_DH ** -0.5 * LOG2_E)
    for c, mk in enumerate(_combo_masks()):
        qs_ref[c * tq:(c + 1) * tq, :] = jnp.where(mk, q, 0.0).astype(BF16)
    m_ref[...] = jnp.full(m_ref.shape, MASK_VALUE, F32)
    acc_ref[...] = jnp.zeros_like(acc_ref)
    qpos = qi * tq + lax.broadcasted_iota(jnp.int32, (tq, 1), 0)
    reps = tk // LANES

    def block(kb, masked):
        start = pl.multiple_of(kb * tk, tk)
        kblk = k_ref[pl.ds(start, tk), :]
        vblk = v_ref[pl.ds(start, tk), :]
        if masked:
            mask = (start + lax.broadcasted_iota(jnp.int32, (1, tk), 1)) <= qpos
        vaug = [jnp.where(head == h, vblk, jnp.ones_like(vblk)) for h in range(HEADS)]
        for c in range(2 * HEADS):
            rows = slice(c * tq, (c + 1) * tq)
            s = _dot_nt(qs_ref[rows, :], kblk)
            if masked:
                s = jnp.where(mask, s, MASK_VALUE)
            m_old = m_ref[rows, :]
            m_new = jnp.maximum(m_old, jnp.max(s, axis=1, keepdims=True))
            m_ref[rows, :] = m_new
            p = jnp.exp2(s - jnp.concatenate([m_new] * reps, axis=1)).astype(BF16)
            alpha = jnp.exp2(m_old - m_new)
            acc_ref[rows, :] = (jnp.concatenate([alpha] * (MIX_W // LANES), axis=1) * acc_ref[rows, :]
                                + _dot(p, vaug[c // 2]))

    def unmasked(kb, carry):
        block(kb, False)
        return carry

    diag = qi * tq // tk
    lax.fori_loop(0, diag, unmasked, 0)
    block(diag, True)

    lam, lam_init = _diff_lambda(lam_ref, layer)
    halves = []
    for half in range(MIX_W // LANES):
        mine = slice(half * LANES, (half + 1) * LANES)
        other = slice((1 - half) * LANES, (2 - half) * LANES)
        o = jnp.zeros((tq, LANES), F32)
        for h in range(half * 2, half * 2 + 2):
            r1 = slice(2 * h * tq, (2 * h + 1) * tq)
            r2 = slice((2 * h + 1) * tq, (2 * h + 2) * tq)
            o1 = acc_ref[r1, mine] / acc_ref[r1, other]
            o2 = acc_ref[r2, mine] / acc_ref[r2, other]
            o = jnp.where(head[:, mine] == h, o1 - lam * o2, o)
        halves.append(o)
    o = jnp.concatenate(halves, axis=1)
    o_ref[...] = _head_rms(o, gain_ref[...], _block_diag16()) * (1.0 - lam_init)


def _diff_prompt(proj32, kv16, nseq, seqlen, lam_p, gain, layer):
    tq = tk = 512
    nq = seqlen // tq
    return pl.pallas_call(
        functools.partial(_diff_prompt_kernel, tq=tq, tk=tk, layer=layer),
        grid=(nseq, nq),
        in_specs=[pl.BlockSpec((tq, MIX_W), lambda b, i: (b * nq + i, S_DQ)),
                  pl.BlockSpec((seqlen, MIX_W), lambda b, i: (b, KV_SLABS.index(S_DK))),
                  pl.BlockSpec((seqlen, MIX_W), lambda b, i: (b, KV_SLABS.index(S_DV))),
                  pl.BlockSpec(lam_p.shape, lambda b, i: (0, 0)),
                  pl.BlockSpec((1, MIX_W), lambda b, i: (0, 0))],
        out_specs=pl.BlockSpec((tq, MIX_W), lambda b, i: (b * nq + i, 0)),
        out_shape=jax.ShapeDtypeStruct((nseq * seqlen, MIX_W), F32),
        scratch_shapes=[pltpu.VMEM((2 * HEADS * tq, MIX_W), BF16),
                        pltpu.VMEM((2 * HEADS * tq, LANES), F32),
                        pltpu.VMEM((2 * HEADS * tq, MIX_W), F32)],
        compiler_params=_params("parallel", "arbitrary"),
        name="diff_prompt",
    )(proj32, kv16, kv16, lam_p, gain)


def _stack_rows(x, masks):
    return jnp.concatenate([jnp.where(mk, x, 0.0) for mk in masks], axis=0)


def _sample_attn_kernel(pt_ref, q_ref, sk_ref, sv_ref, dq_ref, dk_ref, dv_ref, *rest,
                        ls, n_pages, sb_pages, layer, group):
    del pt_ref
    counts = (sb_pages, sb_pages, n_pages, n_pages)
    n_ops = group * sum(counts)
    pages = rest[:n_ops]
    lam_ref, gain_ref, osb_ref, odf_ref, alive_ref, sb_acc, sb_carry = rest[n_ops:]
    head = _lane_head()
    sb_masks = [head == h for h in range(HEADS)]
    key = lax.broadcasted_iota(jnp.int32, (1, PAGE_SIZE), 1)
    t_sb = lax.broadcasted_iota(jnp.int32, (HEADS * ls, 1), 0) % ls
    t_df = lax.broadcasted_iota(jnp.int32, (2 * HEADS * ls, 1), 0) % ls
    lam, lam_init = _diff_lambda(lam_ref, layer)

    def seq_pages(cache, g):
        lo = group * sum(counts[:cache]) + g * counts[cache]
        return pages[lo:lo + counts[cache]]

    def pad_page(x):
        return jnp.concatenate([x, jnp.zeros((PAGE_SIZE - ls, MIX_W), F32)], axis=0).astype(BF16)

    def later16(n):
        ri = lax.broadcasted_iota(jnp.int32, (n, n), 0)
        ci = lax.broadcasted_iota(jnp.int32, (n, n), 1)
        return jnp.where(ri > ci, 1.0, 0.0).astype(BF16)

    def sb_block(g, z, pv_fn, mask):
        lsg = _log_sigmoid(z)
        u = lsg - z
        if mask is not None:
            u = jnp.where(mask, u, 0.0)
        between = sb_carry[g] + _split_dot(u, later16(z.shape[1]))
        w = jnp.exp(lsg + between)
        if mask is not None:
            w = jnp.where(mask, w, 0.0)
        sb_acc[g] += pv_fn(w.astype(BF16))
        sb_carry[g] += jnp.sum(u, axis=1, keepdims=True)

    def sb_past(g, qs, blk):
        psk, psv = seq_pages(0, g), seq_pages(1, g)
        hi = sb_pages - 1 - 2 * blk
        kt = jnp.concatenate([psk[hi - 1][...], psk[hi][...]], axis=1).astype(BF16)
        vt = jnp.concatenate([psv[hi - 1][...], psv[hi][...]], axis=1).astype(BF16)
        sb_block(g, _dot(qs, kt), lambda w: _dot_nt(w, vt), None)

    def sb_rest(g, qs, blk):
        if blk < sb_pages // 2:
            @pl.when(jnp.max(sb_carry[g]) > SB_DEAD_LOG)
            def _():
                sb_past(g, qs, blk)
                sb_rest(g, qs, blk + 1)

    def stacked_queries(g):
        rows = slice(g * ls, (g + 1) * ls)
        qs = _stack_rows(q_ref[rows, :] * (HEAD_W ** -0.5), sb_masks).astype(BF16)
        qd = _stack_rows(dq_ref[rows, :] * (DIFF_DH ** -0.5), _combo_masks()).astype(BF16)
        return rows, qs, qd

    for g in range(group):
        rows, qs, qd = stacked_queries(g)
        sb_acc[g] = jnp.zeros(sb_acc.shape[1:], F32)
        sb_carry[g] = jnp.zeros(sb_carry.shape[1:], F32)
        sv_new = pad_page(sv_ref[rows, :])
        sb_block(g, _dot_nt(qs, pad_page(sk_ref[rows, :])), lambda w, sv_new=sv_new: _dot(w, sv_new),
                 key < t_sb)
        sb_past(g, qs, 0)

        s_new = jnp.where(key <= t_df, _dot_nt(qd, pad_page(dk_ref[rows, :])), MASK_VALUE)
        kt_all = jnp.concatenate([r[...].astype(BF16) for r in seq_pages(2, g)], axis=1)
        s_past = _dot(qd, kt_all)
        m = jnp.maximum(jnp.max(s_new, axis=1, keepdims=True), jnp.max(s_past, axis=1, keepdims=True))
        p_new = jnp.exp(s_new - m)
        p_past = jnp.exp(s_past - m)
        l = jnp.sum(p_new, axis=1, keepdims=True) + jnp.sum(p_past, axis=1, keepdims=True)
        vt_all = jnp.concatenate([r[...].astype(BF16) for r in seq_pages(3, g)], axis=1)
        on = (_dot(p_new.astype(BF16), pad_page(dv_ref[rows, :]))
              + _dot_nt(p_past.astype(BF16), vt_all)) / l
        o = jnp.zeros((ls, MIX_W), F32)
        for h in range(HEADS):
            o1 = on[(2 * h) * ls:(2 * h + 1) * ls, :]
            o2 = on[(2 * h + 1) * ls:(2 * h + 2) * ls, :]
            o = jnp.where(head == h, o1 - lam * o2, o)
        odf_ref[rows, :] = _head_rms(o, gain_ref[...], _block_diag16()) * (1.0 - lam_init)

    for g in range(group):
        rows, qs, _ = stacked_queries(g)
        sb_rest(g, qs, 1)
        acc = sb_acc[g]
        o = jnp.zeros((ls, MIX_W), F32)
        for h in range(HEADS):
            o = jnp.where(head == h, acc[h * ls:(h + 1) * ls, :], o)
        osb_ref[rows, :] = o
        alive_ref[g] = jnp.zeros(alive_ref.shape[1:], F32) + jnp.max(sb_carry[g])


def _sample_attn(proj32, page_table, caches, nseq, ls, lam_p, gain, layer, sb_pages):
    n_pages = page_table.shape[1]
    assert n_pages % 2 == 0 and sb_pages % 2 == 0 and sb_pages <= n_pages
    group = 1
    counts = (sb_pages, sb_pages, n_pages, n_pages)

    def slab(c):
        return pl.BlockSpec((group * ls, MIX_W), lambda b, pt: (b, c))

    def page(g, p):
        return pl.BlockSpec((None, None, MIX_W, PAGE_SIZE),
                            lambda b, pt: (layer, pt[b * group + g, p], 0, 0))

    out = pl.BlockSpec((group * ls, MIX_W), lambda b, pt: (b, 0))
    grid_spec = pltpu.PrefetchScalarGridSpec(
        num_scalar_prefetch=1,
        grid=(nseq // group,),
        in_specs=[slab(S_SQ), slab(S_SK), slab(S_SV), slab(S_DQ), slab(S_DK), slab(S_DV)]
                 + [page(g, n_pages - n + p) for n in counts for g in range(group) for p in range(n)]
                 + [pl.BlockSpec(lam_p.shape, lambda b, pt: (0, 0)),
                    pl.BlockSpec((1, MIX_W), lambda b, pt: (0, 0))],
        out_specs=[out, out, pl.BlockSpec((group, SUBLANES, LANES), lambda b, pt: (b, 0, 0))],
        scratch_shapes=[pltpu.VMEM((group, HEADS * ls, MIX_W), F32),
                        pltpu.VMEM((group, HEADS * ls, 1), F32)])
    page_args = [c for c, n in zip(caches, counts) for _ in range(group * n)]
    return pl.pallas_call(
        functools.partial(_sample_attn_kernel, ls=ls, n_pages=n_pages, sb_pages=sb_pages,
                          layer=layer, group=group),
        grid_spec=grid_spec,
        out_shape=[jax.ShapeDtypeStruct((nseq * ls, MIX_W), F32)] * 2
                  + [jax.ShapeDtypeStruct((nseq, SUBLANES, LANES), F32)],
        compiler_params=_params("arbitrary"),
        name="sample_attn",
    )(page_table, *([proj32] * 6), *page_args, lam_p, gain)


def _merge_kernel(x_ref, b0_ref, b1_ref, b2_ref, b3_ref, pre_ref, wg_ref, wbr_ref, wo_ref,
                  post_ref, o_ref):
    x = x_ref[...]
    h = _rms(x, pre_ref[...]).astype(BF16)
    merged = None
    for i, b_ref in enumerate((b0_ref, b1_ref, b2_ref, b3_ref)):
        gate = jax.nn.sigmoid(_dot(h, wg_ref[:, i * D_MODEL:(i + 1) * D_MODEL]))
        term = gate * _dot(b_ref[...].astype(BF16), wbr_ref[i])
        merged = term if merged is None else merged + term
    o_ref[...] = x + _rms(_dot(merged.astype(BF16), wo_ref[...]), post_ref[...])


def _const_spec(shape):
    nd = len(shape)
    return pl.BlockSpec(shape, lambda i: (0,) * nd, pipeline_mode=pl.Buffered(1))


def _merge(x, branches, pre, wg16, wbr16, wo16, post):
    n = x.shape[0]
    tm = _row_tile(n, WIDE_TOKEN_TILE)
    tok = pl.BlockSpec((tm, D_MODEL), lambda i: (i, 0))
    br = pl.BlockSpec((tm, MIX_W), lambda i: (i, 0))
    return pl.pallas_call(
        _merge_kernel,
        grid=(n // tm,),
        in_specs=[tok, br, br, br, br, _const_spec((1, D_MODEL)), _const_spec(wg16.shape),
                  _const_spec(wbr16.shape), _const_spec(wo16.shape), _const_spec((1, D_MODEL))],
        out_specs=tok,
        out_shape=jax.ShapeDtypeStruct((n, D_MODEL), F32),
        compiler_params=_params("parallel"),
        name="merge",
    )(x, *branches, pre, wg16, wbr16, wo16, post)


def _cross_attend(q, mk, mv, channel_major):
    head = _lane_head()
    rows = q.shape[0]
    mk = mk.astype(BF16)
    mv = mv.astype(BF16)
    qs = _stack_rows(q * (HEAD_W ** -0.5), [head == h for h in range(HEADS)]).astype(BF16)
    s = _dot(qs, mk) if channel_major else _dot_nt(qs, mk)
    p = jnp.exp(s - jnp.max(s, axis=1, keepdims=True))
    p16 = p.astype(BF16)
    pv = (_dot_nt(p16, mv) if channel_major else _dot(p16, mv)) / jnp.sum(p, axis=1, keepdims=True)
    o = jnp.zeros(q.shape, F32)
    for h in range(HEADS):
        o = jnp.where(head == h, pv[h * rows:(h + 1) * rows, :], o)
    return o


def _cross_attn_kernel(q_ref, mk_ref, mv_ref, o_ref, *, group, seqlen):
    for j in range(group):
        rows = slice(j * seqlen, (j + 1) * seqlen)
        o_ref[rows, :] = _cross_attend(q_ref[rows, :], mk_ref[j], mv_ref[j], channel_major=True)


def _cross_attn(cq, mem_k, mem_v, layer, nseq, seqlen):
    group = math.gcd(nseq, 4)
    mem = pl.BlockSpec((None, group, MIX_W, N_MEM), lambda b: (layer, b, 0, 0))
    tok = pl.BlockSpec((group * seqlen, MIX_W), lambda b: (b, 0))
    return pl.pallas_call(
        functools.partial(_cross_attn_kernel, group=group, seqlen=seqlen),
        grid=(nseq // group,),
        in_specs=[tok, mem, mem],
        out_specs=tok,
        out_shape=jax.ShapeDtypeStruct(cq.shape, F32),
        compiler_params=_params("parallel"),
        name="cross_attn",
    )(cq, mem_k, mem_v)


def _post_kernel(x_ref, *refs, attend):
    if attend:
        cpre_ref, wq_ref, mk_ref, mv_ref, *refs = refs
        cq = _dot(_rms(x_ref[...], cpre_ref[...]).astype(BF16), wq_ref[...])
        co = _cross_attend(cq, mk_ref[...], mv_ref[...], channel_major=False)
    else:
        co_ref, *refs = refs
        co = co_ref[...]
    wco_ref, cpost_ref, mpre_ref, wup_ref, wdn_ref, mpost_ref, o_ref = refs
    x = x_ref[...] + _rms(_dot(co.astype(BF16), wco_ref[...]), cpost_ref[...])
    h = _rms(x, mpre_ref[...]).astype(BF16)
    u = jnp.square(jnp.maximum(_dot(h, wup_ref[...]), 0.0))
    o_ref[...] = x + _rms(_dot(u.astype(BF16), wdn_ref[...]), mpost_ref[...])


def _post(x, attn_args, wco16, cpost, mpre, wup16, wdn16, mpost):
    n = x.shape[0]
    tm = _row_tile(n, WIDE_TOKEN_TILE)
    tok = pl.BlockSpec((tm, D_MODEL), lambda i: (i, 0))
    attend = len(attn_args) > 1
    if attend:
        cpre, wq16, mk, mv, seqlen = attn_args
        tiles = seqlen // tm
        mem = pl.BlockSpec((None, N_MEM, MIX_W), lambda i: (i // tiles, 0, 0))
        attn_specs = [_const_spec((1, D_MODEL)), _const_spec(wq16.shape), mem, mem]
        attn_in = (cpre, wq16, mk, mv)
    else:
        attn_specs = [pl.BlockSpec((tm, MIX_W), lambda i: (i, 0))]
        attn_in = attn_args
    return pl.pallas_call(
        functools.partial(_post_kernel, attend=attend),
        grid=(n // tm,),
        in_specs=[tok] + attn_specs + [_const_spec(wco16.shape),
                  _const_spec((1, D_MODEL)), _const_spec((1, D_MODEL)), _const_spec(wup16.shape),
                  _const_spec(wdn16.shape), _const_spec((1, D_MODEL))],
        out_specs=tok,
        out_shape=jax.ShapeDtypeStruct((n, D_MODEL), F32),
        compiler_params=_params("parallel"),
        name="ca_out_mlp",
    )(x, *attn_in, wco16, cpost, mpre, wup16, wdn16, mpost)


def _trunk_layer(x, nseq, seqlen, rope_tab, mem_k, mem_v, cached_mem_layer, w, rec_fn, attn_fn,
                 emit_kv):
    proj32, *kv = _mixer_proj(x, w["mix_pre_norm"], w["w_in"], rope_tab,
                              (nseq, seqlen) if emit_kv else None)
    o_ret, st_ret, o_hgrn, st_hgrn, new_kv = rec_fn(proj32, kv)
    o_sb, o_diff = attn_fn(proj32, kv)
    x = _merge(x, (o_ret, o_sb, o_diff, o_hgrn), w["mix_pre_norm"], w["w_gate"], w["w_br"],
               w["w_out"], w["mix_post_norm"])
    if cached_mem_layer is None:
        assert seqlen % _row_tile(nseq * seqlen, WIDE_TOKEN_TILE) == 0
        attn_args = (w["ca_pre_norm"], w["w_ca_q"], mem_k, mem_v, seqlen)
    else:
        cq = _norm_proj(x, w["ca_pre_norm"], w["w_ca_q"])
        attn_args = (_cross_attn(cq, mem_k, mem_v, cached_mem_layer, nseq, seqlen),)
    x = _post(x, attn_args, w["w_ca_o"], w["ca_post_norm"], w["mlp_pre_norm"], w["w_mlp_up"],
              w["w_mlp_down"], w["mlp_post_norm"])
    return x, new_kv, st_ret, st_hgrn


def kernel(x_prompt, x_sample, mem_prompt, cache_sb_k, cache_sb_v, cache_diff_k, cache_diff_v, cache_mem_k, cache_mem_v, state_ret, state_hgrn, page_table, mix_pre_norm, mix_post_norm, ca_pre_norm, ca_post_norm, mlp_pre_norm, mlp_post_norm, mem_norm, w_in, w_gate, ret_norm, diff_lambda, diff_norm, hgrn_lb_logits, hgrn_norm, w_br_ret, w_br_sb, w_br_diff, w_br_hgrn, w_out, w_ca_q, w_ca_k, w_ca_v, w_ca_o, w_mlp_up, w_mlp_down):
    depth = w_in.shape[0]
    B, L, _ = x_prompt.shape
    DB, LS, _ = x_sample.shape
    n_pages = page_table.shape[1]
    past_len = n_pages * PAGE_SIZE

    tab_p = _rope_table(jnp.arange(L, dtype=jnp.int32))
    tab_s = jnp.tile(_rope_table(past_len + jnp.arange(LS, dtype=jnp.int32)), (256 // LS, 1))
    caches = [jnp.transpose(c, (0, 1, 3, 4, 2)).reshape(c.shape[0], c.shape[1], MIX_W, PAGE_SIZE)
              for c in (cache_sb_k, cache_sb_v, cache_diff_k, cache_diff_v)]
    mem_cache = [jnp.transpose(c, (0, 1, 3, 4, 2)).reshape(c.shape[0], c.shape[1], MIX_W, N_MEM)
                 for c in (cache_mem_k, cache_mem_v)]
    w_br = jnp.stack([w_br_ret, w_br_sb, w_br_diff, w_br_hgrn], axis=1).astype(BF16)
    w_ca_kv = jnp.concatenate([w_ca_k, w_ca_v], axis=-1).astype(BF16)
    diff_gain = jnp.tile(diff_norm, (1, HEADS))
    def lanes_cm(a):
        return jnp.broadcast_to(a.reshape(depth, HEADS, HEAD_W, 1), (depth, HEADS, HEAD_W, DB))
    ret_gain_cm, hgrn_gain_cm, lbl_cm = lanes_cm(ret_norm), lanes_cm(hgrn_norm), lanes_cm(hgrn_lb_logits)
    st_ret_cm = state_ret.transpose(0, 2, 3, 4, 1)
    st_hgrn_cm = state_hgrn.transpose(0, 2, 3, 4, 1)

    xp = x_prompt.reshape(B * L, D_MODEL)
    xs = x_sample.reshape(DB * LS, D_MODEL)
    mem = mem_prompt.reshape(B * N_MEM, D_MODEL)
    zero_state = jnp.zeros((B, MIX_W, MIX_W), F32)
    outs_p, outs_s = [], []
    for l in range(depth):
        row = lambda a: a[l][None, :]
        w = {
            "mix_pre_norm": row(mix_pre_norm), "mix_post_norm": row(mix_post_norm),
            "ca_pre_norm": row(ca_pre_norm), "ca_post_norm": row(ca_post_norm),
            "mlp_pre_norm": row(mlp_pre_norm), "mlp_post_norm": row(mlp_post_norm),
            "ret_norm": row(ret_norm), "hgrn_norm": row(hgrn_norm),
            "hgrn_lb_logits": hgrn_lb_logits,
            "w_in": w_in[l].astype(BF16), "w_gate": w_gate[l].astype(BF16), "w_br": w_br[l],
            "w_out": w_out[l].astype(BF16), "w_ca_q": w_ca_q[l].astype(BF16),
            "w_ca_o": w_ca_o[l].astype(BF16), "w_mlp_up": w_mlp_up[l].astype(BF16),
            "w_mlp_down": w_mlp_down[l].astype(BF16),
        }
        lam_p, dgain = diff_lambda[l], diff_gain[l][None, :]

        mkv = _norm_proj(mem, row(mem_norm), w_ca_kv[l])
        mk_p = mkv[:, :MIX_W].reshape(B, N_MEM, MIX_W)
        mv_p = mkv[:, MIX_W:].reshape(B, N_MEM, MIX_W)

        def rec_p(proj32, kv, l=l, w=w):
            o_ret, st_ret = _recurrence("ret", proj32, B, L, 256, w["ret_norm"], zero_state)
            o_hgrn, st_hgrn = _recurrence("hgrn", proj32, B, L, 32, w["hgrn_norm"], zero_state,
                                          hgrn_lb_logits, l, nsub=8)
            return o_ret, _bd_to_state(st_ret), o_hgrn, _bd_to_state(st_hgrn), list(kv[1:])

        def attn_p(proj32, kv, l=l, lam_p=lam_p, dgain=dgain):
            return (_sb_prompt(proj32, kv[0], B, L),
                    _diff_prompt(proj32, kv[0], B, L, lam_p, dgain, l))

        xp, kv_p, ret_p, hgrn_p = _trunk_layer(xp, B, L, tab_p, mk_p, mv_p, None, w, rec_p, attn_p,
                                               True)
        outs_p.append((kv_p, ret_p, hgrn_p, mk_p, mv_p))

        def rec_s(proj32, kv, l=l):
            del kv
            pcm = proj32.reshape(DB, LS, N_SLABS, HEADS, HEAD_W).transpose(2, 1, 3, 4, 0)
            o_ret, st_ret = _decode_recurrence("ret", pcm, ret_gain_cm[l], st_ret_cm[l])
            o_hgrn, st_hgrn = _decode_recurrence("hgrn", pcm, hgrn_gain_cm[l], st_hgrn_cm[l],
                                                 lbl_cm, l)
            rows = lambda o: o.transpose(3, 0, 1, 2).reshape(DB * LS, MIX_W)
            return (rows(o_ret), st_ret.transpose(3, 0, 1, 2),
                    rows(o_hgrn), st_hgrn.transpose(3, 0, 1, 2), [pcm[c] for c in KV_SLABS])

        def attn_s(proj32, kv, l=l, lam_p=lam_p, dgain=dgain):
            del kv
            run = functools.partial(_sample_attn, proj32, page_table, caches, DB, LS, lam_p, dgain, l)
            o_sb, o_diff, alive = run(2)
            return lax.cond(jnp.max(alive) > SB_DEAD_LOG, lambda: tuple(run(n_pages)[:2]),
                            lambda: (o_sb, o_diff))

        xs, kv_s, ret_s, hgrn_s = _trunk_layer(xs, DB, LS, tab_s, mem_cache[0], mem_cache[1], l,
                                               w, rec_s, attn_s, False)
        outs_s.append((kv_s, ret_s, hgrn_s))

    def kv_prompt(i):
        a = jnp.stack([o[0][i] for o in outs_p], axis=0)
        return a.reshape(depth, B, HEADS, HEAD_W, L).transpose(0, 1, 4, 2, 3)

    def kv_sample(i):
        return jnp.stack([o[0][i] for o in outs_s], axis=0).transpose(0, 4, 1, 2, 3)

    def states(outs, i):
        return jnp.stack([o[i] for o in outs], axis=0)

    def memkv(i):
        return jnp.stack([o[i].reshape(B, N_MEM, HEADS, HEAD_W) for o in outs_p], axis=0)

    return (xp.reshape(B, L, D_MODEL), xs.reshape(DB, LS, D_MODEL),
            kv_prompt(0), kv_prompt(1), kv_prompt(2), kv_prompt(3),
            memkv(3), memkv(4), states(outs_p, 1), states(outs_p, 2),
            kv_sample(0), kv_sample(1), kv_sample(2), kv_sample(3),
            states(outs_s, 1), states(outs_s, 2))
```
